```python
import math
import jax, jax.numpy as jnp
from jax import lax
import numpy as np

D_MODEL = 2048
BATCH = 16
SEQ = 256
DEPTH = 4
DEC_BATCH = 8
DEC_SEQ = 1024
PAST_LEN = 256

GRID_W = 64
HEAD_DIM = 128
GROUP_WIDTH = D_MODEL // 4
NA_HEADS = GROUP_WIDTH // HEAD_DIM
NA_WIN_ROWS = 8
NA_WIN_COLS = 16
GLA_HEADS = 4
GLA_DV = GROUP_WIDTH // GLA_HEADS
GLA_DK = GLA_DV // 2
GLA_GATE_RANK = 16
GLA_TAU = 16.0
GQA_HEADS = GROUP_WIDTH // HEAD_DIM
GQA_KV_HEADS = GQA_HEADS // 2
GDN_HEADS = GROUP_WIDTH // HEAD_DIM
GDN_DK = HEAD_DIM
GDN_DV = HEAD_DIM
CONV_K = 5
CHUNK = 64
Q_BLOCK = 128
ROPE_THETA = 10000.0
FFN_DIM = ((8 * D_MODEL // 3 + 127) // 128) * 128
N_MOD = 9
EPS = 1e-6
NEG_INF = -1e30

IN_SPLITS = (
    ('na_q', NA_HEADS * HEAD_DIM), ('na_k', NA_HEADS * HEAD_DIM), ('na_v', NA_HEADS * HEAD_DIM),
    ('gla_q', GLA_HEADS * GLA_DK), ('gla_k', GLA_HEADS * GLA_DK), ('gla_v', GLA_HEADS * GLA_DV),
    ('gla_r', GLA_HEADS * GLA_DV), ('gla_gf', GLA_GATE_RANK), ('gla_gb', GLA_GATE_RANK),
    ('gqa_q', GQA_HEADS * HEAD_DIM), ('gqa_k', GQA_KV_HEADS * HEAD_DIM), ('gqa_v', GQA_KV_HEADS * HEAD_DIM),
    ('gdn_qkv', 3 * GDN_HEADS * HEAD_DIM), ('gdn_z', GDN_HEADS * GDN_DV),
    ('gdn_b', 2 * GDN_HEADS), ('gdn_a', 2 * GDN_HEADS),
)
IN_COLS = sum(width for _, width in IN_SPLITS)

kernel_name = 'hybrid_diffusion_parallel_heads_step'


def rms_norm(x, g):
    xf = x.astype(jnp.float32)
    y = xf * lax.rsqrt(jnp.mean(xf * xf, axis=-1, keepdims=True) + EPS)
    return (y * g.astype(jnp.float32)).astype(x.dtype)


def l2_norm(x):
    xf = x.astype(jnp.float32)
    return xf * lax.rsqrt(jnp.sum(xf * xf, axis=-1, keepdims=True) + EPS)


def split_cols(z):
    out, off = {}, 0
    for name, width in IN_SPLITS:
        out[name] = z[..., off:off + width]
        off += width
    return out


def to_heads(x, n):
    B, T, _ = x.shape
    return x.reshape(B, T, n, -1).transpose(0, 2, 1, 3)


def from_heads(x):
    B, n, T, d = x.shape
    return x.transpose(0, 2, 1, 3).reshape(B, T, n * d)


def flip_t(x):
    return x[:, :, ::-1]


def adaln(cond, w_mod, b_mod):
    m = jax.nn.silu(cond) @ w_mod + b_mod
    return jnp.split(m[:, None, :], N_MOD, axis=-1)


def swiglu(x, w_gu, w_down):
    g, u = jnp.split(x @ w_gu, 2, axis=-1)
    return (jax.nn.silu(g) * u) @ w_down


def axial_angles(T):
    t = jnp.arange(T)
    row = (t // GRID_W).astype(jnp.float32)
    col = (t % GRID_W).astype(jnp.float32)
    half = HEAD_DIM // 2
    inv = ROPE_THETA ** (-jnp.arange(0, half, 2, dtype=jnp.float32) / half)
    return row[:, None] * inv[None, :], col[:, None] * inv[None, :]


def rope_half(x, ang):
    x1, x2 = jnp.split(x, 2, axis=-1)
    cos, sin = jnp.cos(ang).astype(x.dtype), jnp.sin(ang).astype(x.dtype)
    return jnp.concatenate([x1 * cos - x2 * sin, x1 * sin + x2 * cos], axis=-1)


def apply_axial_rope(x, ang_r, ang_c):
    xr, xc = jnp.split(x, 2, axis=-1)
    return jnp.concatenate([rope_half(xr, ang_r), rope_half(xc, ang_c)], axis=-1)


def dense_attention(q, k, v):
    B, Hk, G, T, d = q.shape
    nb = T // Q_BLOCK
    qb = q.reshape(B, Hk, G, nb, Q_BLOCK, d).transpose(3, 0, 1, 2, 4, 5)
    scale = d ** -0.5

    def one_block(qi):
        s = jnp.einsum('bhgqd,bhkd->bhgqk', qi, k).astype(jnp.float32) * scale
        p = jax.nn.softmax(s, axis=-1).astype(v.dtype)
        return jnp.einsum('bhgqk,bhkd->bhgqd', p, v)

    o = lax.map(one_block, qb)
    return o.transpose(1, 2, 3, 0, 4, 5).reshape(B, Hk, G, T, d)


def neighbourhood_attention(q, k, v, ck, cv, rpb):
    B, H, T, d = q.shape
    rows = T // GRID_W
    wr = min(NA_WIN_ROWS, rows)
    r = jnp.arange(rows)
    krow = jnp.clip(r - wr // 2, 0, rows - wr)[:, None] + jnp.arange(wr)[None, :]
    col = jnp.arange(GRID_W)
    c0 = jnp.clip(col - NA_WIN_COLS // 2, 0, GRID_W - NA_WIN_COLS)
    in_win = (col[None, :] >= c0[:, None]) & (col[None, :] < c0[:, None] + NA_WIN_COLS)
    dr = krow - r[:, None] + (NA_WIN_ROWS - 1)
    dc = jnp.clip(col[None, :] - col[:, None] + (NA_WIN_COLS - 1), 0, 2 * NA_WIN_COLS - 2)
    bias = rpb[:, dr[:, None, :, None], dc[None, :, None, :]].astype(jnp.float32)
    qg = q.reshape(B, H, rows, GRID_W, d)
    kg = k.reshape(B, H, rows, GRID_W, d)[:, :, krow]
    vg = v.reshape(B, H, rows, GRID_W, d)[:, :, krow]
    scale = d ** -0.5
    s_loc = jnp.einsum('bhrqd,bhrjkd->bhrqjk', qg, kg).astype(jnp.float32) * scale + bias[None]
    s_loc = jnp.where(in_win[:, None, :], s_loc, NEG_INF)
    s_ctx = jnp.einsum('bhrqd,bhld->bhrql', qg, ck).astype(jnp.float32) * scale
    L = ck.shape[2]
    s = jnp.concatenate([s_ctx, s_loc.reshape(B, H, rows, GRID_W, wr * GRID_W)], axis=-1)
    p = jax.nn.softmax(s, axis=-1).astype(v.dtype)
    o = (jnp.einsum('bhrql,bhld->bhrqd', p[..., :L], cv)
         + jnp.einsum('bhrqjk,bhrjkd->bhrqd', p[..., L:].reshape(B, H, rows, GRID_W, wr, GRID_W), vg))
    return o.reshape(B, H, T, d)


def to_chunks(x):
    B, H, T = x.shape[:3]
    y = x.astype(jnp.float32).reshape((B, H, T // CHUNK, CHUNK) + x.shape[3:])
    return jnp.moveaxis(y, 2, 0)


def from_chunks(o):
    n, B, H, C, e = o.shape
    return jnp.moveaxis(o, 0, 2).reshape(B, H, n * C, e)


def gla_scan(q, k, v, log_a, s0):
    causal = jnp.tril(jnp.ones((CHUNK, CHUNK), bool))

    def step(S, inp):
        qi, ki, vi, gi = inp
        b = jnp.cumsum(gi, axis=2)
        diff = b[:, :, :, None, :] - b[:, :, None, :, :]
        decay = jnp.where(causal[:, :, None], jnp.exp(jnp.minimum(diff, 0.0)), 0.0)
        att = jnp.einsum('bhqd,bhkd,bhqkd->bhqk', qi, ki, decay)
        o = jnp.einsum('bhqd,bhde->bhqe', qi * jnp.exp(b), S) + jnp.einsum('bhqk,bhke->bhqe', att, vi)
        b_last = b[:, :, -1:, :]
        S = S * jnp.exp(b_last[:, :, 0, :, None]) + jnp.einsum('bhkd,bhke->bhde', ki * jnp.exp(b_last - b), vi)
        return S, o

    S, o = lax.scan(step, s0.astype(jnp.float32), (to_chunks(q), to_chunks(k), to_chunks(v), to_chunks(log_a)))
    return from_chunks(o), S


def gdn_scan(q, k, v, g, beta, s0):
    dv = v.shape[-1]
    causal = jnp.tril(jnp.ones((CHUNK, CHUNK), bool))
    strict = jnp.tril(jnp.ones((CHUNK, CHUNK), bool), -1)
    eye = jnp.eye(CHUNK, dtype=jnp.float32)

    def step(S, inp):
        qi, ki, vi, gi, bi = inp
        b = jnp.cumsum(gi, axis=-1)
        diff = b[..., :, None] - b[..., None, :]
        decay = jnp.where(causal, jnp.exp(jnp.minimum(diff, 0.0)), 0.0)
        kb = ki * bi[..., None]
        low = jnp.where(strict, jnp.einsum('bhqd,bhkd->bhqk', kb, ki) * decay, 0.0)
        rhs = jnp.concatenate([vi * bi[..., None], kb * jnp.exp(b)[..., None]], axis=-1)
        sol = lax.linalg.triangular_solve(low + eye, rhs, left_side=True, lower=True, unit_diagonal=True)
        u, w = sol[..., :dv], sol[..., dv:]
        v_new = u - w @ S
        att = jnp.einsum('bhqd,bhkd->bhqk', qi, ki) * decay
        o = (qi * jnp.exp(b)[..., None]) @ S + att @ v_new
        b_last = b[..., -1:]
        S = S * jnp.exp(b_last)[..., None] + jnp.einsum('bhkd,bhke->bhde', ki * jnp.exp(b_last - b)[..., None], v_new)
        return S, o

    xs = (to_chunks(q), to_chunks(k), to_chunks(v), to_chunks(g), to_chunks(beta))
    S, o = lax.scan(step, s0.astype(jnp.float32), xs)
    return from_chunks(o), S


def centred_depthwise_conv(x, w):
    C = x.shape[-1]
    pad = CONV_K // 2
    return lax.conv_general_dilated(x, w[:, None, :].astype(x.dtype), window_strides=(1,),
                                    padding=((pad, pad),), dimension_numbers=('NWC', 'WIO', 'NWC'),
                                    feature_group_count=C)


def token_mixing(h, P, l, ctx):
    B, T, _ = h.shape
    f32 = jnp.float32
    is_ctx = ctx is None
    z = split_cols(h @ P['w_in'][l])

    qn = rms_norm(to_heads(z['na_q'], NA_HEADS), P['na_qk_norm'][l, 0])
    kn = rms_norm(to_heads(z['na_k'], NA_HEADS), P['na_qk_norm'][l, 1])
    vn = to_heads(z['na_v'], NA_HEADS)
    if is_ctx:
        o_na = dense_attention(qn[:, :, None], kn, vn)[:, :, 0]
    else:
        o_na = neighbourhood_attention(qn, kn, vn, ctx['na_k'], ctx['na_v'], P['na_rpb'][l])

    qg = to_heads(z['gla_q'], GLA_HEADS) * (GLA_DK ** -0.5)
    kg = to_heads(z['gla_k'], GLA_HEADS)
    vg = to_heads(z['gla_v'], GLA_HEADS)
    la_f = to_heads(jax.nn.log_sigmoid((z['gla_gf'] @ P['gla_gate_up'][l, 0] + P['gla_gate_bias'][l, 0]).astype(f32)) / GLA_TAU, GLA_HEADS)
    la_b = to_heads(jax.nn.log_sigmoid((z['gla_gb'] @ P['gla_gate_up'][l, 1] + P['gla_gate_bias'][l, 1]).astype(f32)) / GLA_TAU, GLA_HEADS)
    if is_ctx:
        s0f = s0b = jnp.zeros((B, GLA_HEADS, GLA_DK, GLA_DV), f32)
    else:
        s0f, s0b = ctx['gla'][:, 0], ctx['gla'][:, 1]
    o_f, sf_gla = gla_scan(qg, kg, vg, la_f, s0f)
    o_b, sb_gla = gla_scan(flip_t(qg), flip_t(kg), flip_t(vg), flip_t(la_b), s0b)
    o_gla = from_heads(rms_norm(o_f + flip_t(o_b), P['gla_out_norm'][l])) * jax.nn.silu(z['gla_r'].astype(f32))

    qa = rms_norm(to_heads(z['gqa_q'], GQA_HEADS), P['gqa_qk_norm'][l, 0])
    ka = rms_norm(to_heads(z['gqa_k'], GQA_KV_HEADS), P['gqa_qk_norm'][l, 1])
    va = to_heads(z['gqa_v'], GQA_KV_HEADS)
    if is_ctx:
        keys, vals = ka, va
    else:
        ang_r, ang_c = axial_angles(T)
        qa = apply_axial_rope(qa, ang_r, ang_c)
        keys = jnp.concatenate([ctx['gqa_k'].astype(ka.dtype), apply_axial_rope(ka, ang_r, ang_c)], axis=2)
        vals = jnp.concatenate([ctx['gqa_v'].astype(va.dtype), va], axis=2)
    o_a = dense_attention(qa.reshape(B, GQA_KV_HEADS, GQA_HEADS // GQA_KV_HEADS, T, HEAD_DIM), keys, vals)
    o_gqa = from_heads(o_a.reshape(B, GQA_HEADS, T, HEAD_DIM))

    qkv = jax.nn.silu(centred_depthwise_conv(z['gdn_qkv'], P['gdn_conv'][l]))
    qd, kd, vd = jnp.split(qkv, 3, axis=-1)
    qd = l2_norm(to_heads(qd, GDN_HEADS)) * (GDN_DK ** -0.5)
    kd = l2_norm(to_heads(kd, GDN_HEADS))
    vd = to_heads(vd, GDN_HEADS)
    beta = jax.nn.sigmoid(z['gdn_b'].astype(f32)).reshape(B, T, 2, GDN_HEADS).transpose(2, 0, 3, 1)
    a_in = z['gdn_a'].astype(f32).reshape(B, T, 2, GDN_HEADS).transpose(2, 0, 3, 1)
    a_log = P['gdn_a_log'][l].astype(f32)
    dt_b = P['gdn_dt_bias'][l].astype(f32)
    g_f = -jnp.exp(a_log[0])[None, :, None] * jax.nn.softplus(a_in[0] + dt_b[0][None, :, None])
    g_b = -jnp.exp(a_log[1])[None, :, None] * jax.nn.softplus(a_in[1] + dt_b[1][None, :, None])
    if is_ctx:
        d0f = d0b = jnp.zeros((B, GDN_HEADS, GDN_DK, GDN_DV), f32)
    else:
        d0f, d0b = ctx['gdn'][:, 0], ctx['gdn'][:, 1]
    od_f, sf_gdn = gdn_scan(qd, kd, vd, g_f, beta[0], d0f)
    od_b, sb_gdn = gdn_scan(flip_t(qd), flip_t(kd), flip_t(vd), flip_t(g_b), flip_t(beta[1]), d0b)
    o_gdn = from_heads(rms_norm(od_f + flip_t(od_b), P['gdn_out_norm'][l])) * jax.nn.silu(z['gdn_z'].astype(f32))

    merged = jnp.concatenate([from_heads(o_na), o_gla.astype(h.dtype), o_gqa, o_gdn.astype(h.dtype)], axis=-1)
    out = merged @ P['w_out'][l]
    if is_ctx:
        return out, (kn, vn, ka, va, jnp.stack([sf_gla, sb_gla], axis=1), jnp.stack([sf_gdn, sb_gdn], axis=1))
    return out, None


def trunk_layer(x, cond, P, l, ctx):
    sh1, sc1, g1, sh2, sc2, g2, sh3, sc3, g3 = adaln(cond, P['w_mod'][l], P['b_mod'][l])
    h = rms_norm(x, P['norm_g'][l, 0]) * (1 + sc1) + sh1
    x = x + 0.5 * g1 * swiglu(h, P['ffn_gu'][l, 0], P['ffn_down'][l, 0])
    h = rms_norm(x, P['norm_g'][l, 1]) * (1 + sc2) + sh2
    m, cache = token_mixing(h, P, l, ctx)
    x = x + g2 * m
    h = rms_norm(x, P['norm_g'][l, 2]) * (1 + sc3) + sh3
    x = x + 0.5 * g3 * swiglu(h, P['ffn_gu'][l, 1], P['ffn_down'][l, 1])
    return x, cache


def setup_inputs(seed: int = 0) -> dict:
    key = jax.random.key(seed)
    ks = jax.random.split(key, 28)
    D = D_MODEL

    def nrm(k, shape, s):
        return jax.random.normal(k, shape, jnp.float32) * s

    dt = jnp.exp(jax.random.uniform(ks[25], (DEPTH, 2, GDN_HEADS), jnp.float32, math.log(1e-3), math.log(1e-1)))
    return {
        'x_prompt': nrm(ks[0], (BATCH, SEQ, D), 1.0),
        'x_sample': nrm(ks[1], (DEC_BATCH, DEC_SEQ, D), 1.0),
        'cache_na_k': nrm(ks[2], (DEC_BATCH, DEPTH, NA_HEADS, PAST_LEN, HEAD_DIM), 1.0),
        'cache_na_v': nrm(ks[3], (DEC_BATCH, DEPTH, NA_HEADS, PAST_LEN, HEAD_DIM), 1.0),
        'cache_gqa_k': nrm(ks[4], (DEC_BATCH, DEPTH, GQA_KV_HEADS, PAST_LEN, HEAD_DIM), 1.0),
        'cache_gqa_v': nrm(ks[5], (DEC_BATCH, DEPTH, GQA_KV_HEADS, PAST_LEN, HEAD_DIM), 1.0),
        'state_gla': nrm(ks[6], (DEC_BATCH, DEPTH, 2, GLA_HEADS, GLA_DK, GLA_DV), 0.5),
        'state_gdn': nrm(ks[7], (DEC_BATCH, DEPTH, 2, GDN_HEADS, GDN_DK, GDN_DV), 0.5),
        'c': nrm(ks[8], (DEC_BATCH, D), 1.0),
        'c_ctx': nrm(ks[9], (D,), 1.0),
        'norm_g': 1.0 + nrm(ks[10], (DEPTH, 3, D), 0.02),
        'w_mod': nrm(ks[11], (DEPTH, D, N_MOD * D), 0.5 * D ** -0.5),
        'b_mod': nrm(ks[12], (DEPTH, N_MOD * D), 0.02),
        'ffn_gu': nrm(ks[13], (DEPTH, 2, D, 2 * FFN_DIM), D ** -0.5),
        'ffn_down': nrm(ks[14], (DEPTH, 2, FFN_DIM, D), FFN_DIM ** -0.5),
        'w_in': nrm(ks[15], (DEPTH, D, IN_COLS), D ** -0.5),
        'w_out': nrm(ks[16], (DEPTH, 4 * GROUP_WIDTH, D), (4 * GROUP_WIDTH) ** -0.5),
        'na_qk_norm': 1.0 + nrm(ks[17], (DEPTH, 2, HEAD_DIM), 0.02),
        'na_rpb': nrm(ks[18], (DEPTH, NA_HEADS, 2 * NA_WIN_ROWS - 1, 2 * NA_WIN_COLS - 1), 0.1),
        'gla_gate_up': nrm(ks[19], (DEPTH, 2, GLA_GATE_RANK, GLA_HEADS * GLA_DK), GLA_GATE_RANK ** -0.5),
        'gla_gate_bias': nrm(ks[20], (DEPTH, 2, GLA_HEADS * GLA_DK), 0.1),
        'gla_out_norm': 1.0 + nrm(ks[21], (DEPTH, GLA_DV), 0.02),
        'gqa_qk_norm': 1.0 + nrm(ks[22], (DEPTH, 2, HEAD_DIM), 0.02),
        'gdn_conv': nrm(ks[23], (DEPTH, CONV_K, 3 * GDN_HEADS * HEAD_DIM), CONV_K ** -0.5),
        'gdn_a_log': jnp.log(jax.random.uniform(ks[24], (DEPTH, 2, GDN_HEADS), jnp.float32, 1.0, 16.0)),
        'gdn_dt_bias': dt + jnp.log(-jnp.expm1(-dt)),
        'gdn_out_norm': 1.0 + nrm(ks[26], (DEPTH, GDN_DV), 0.02),
    }


def reference(x_prompt, x_sample, cache_na_k, cache_na_v, cache_gqa_k, cache_gqa_v, state_gla, state_gdn,
              c, c_ctx, norm_g, w_mod, b_mod, ffn_gu, ffn_down, w_in, w_out, na_qk_norm, na_rpb,
              gla_gate_up, gla_gate_bias, gla_out_norm, gqa_qk_norm, gdn_conv, gdn_a_log, gdn_dt_bias,
              gdn_out_norm):
    P = dict(norm_g=norm_g, w_mod=w_mod, b_mod=b_mod, ffn_gu=ffn_gu, ffn_down=ffn_down, w_in=w_in,
             w_out=w_out, na_qk_norm=na_qk_norm, na_rpb=na_rpb, gla_gate_up=gla_gate_up,
             gla_gate_bias=gla_gate_bias, gla_out_norm=gla_out_norm, gqa_qk_norm=gqa_qk_norm,
             gdn_conv=gdn_conv, gdn_a_log=gdn_a_log, gdn_dt_bias=gdn_dt_bias, gdn_out_norm=gdn_out_norm)

    xp = x_prompt
    na_k_l, na_v_l, gqa_k_l, gqa_v_l, gla_l, gdn_l = [], [], [], [], [], []
    for l in range(DEPTH):
        xp, (nk, nv, gk, gv, sg, sd) = trunk_layer(xp, c_ctx[None, :], P, l, None)
        na_k_l.append(nk)
        na_v_l.append(nv)
        gqa_k_l.append(gk)
        gqa_v_l.append(gv)
        gla_l.append(sg)
        gdn_l.append(sd)
    new_na_k = jnp.stack(na_k_l, axis=1)
    new_na_v = jnp.stack(na_v_l, axis=1)
    new_gqa_k = jnp.stack(gqa_k_l, axis=1)
    new_gqa_v = jnp.stack(gqa_v_l, axis=1)
    new_state_gla = jnp.stack(gla_l, axis=1)
    new_state_gdn = jnp.stack(gdn_l, axis=1)

    xs = x_sample
    for l in range(DEPTH):
        ctx = dict(na_k=cache_na_k[:, l], na_v=cache_na_v[:, l], gqa_k=cache_gqa_k[:, l],
                   gqa_v=cache_gqa_v[:, l], gla=state_gla[:, l], gdn=state_gdn[:, l])
        xs, _ = trunk_layer(xs, c, P, l, ctx)

    return (xp, xs, new_na_k, new_na_v, new_gqa_k, new_gqa_v, new_state_gla, new_state_gdn)
```

```python
import functools
import math

import jax
import jax.numpy as jnp
import numpy as np
from jax import lax
from jax.experimental import pallas as pl
from jax.experimental.pallas import tpu as pltpu

D_MODEL = 2048
BATCH = 16
SEQ = 256
DEPTH = 4
DEC_BATCH = 8
DEC_SEQ = 1024
PAST_LEN = 256

GRID_W = 64
HEAD_DIM = 128
GROUP_WIDTH = D_MODEL // 4
NA_HEADS = GROUP_WIDTH // HEAD_DIM
NA_WIN_ROWS = 8
NA_WIN_COLS = 16
GLA_HEADS = 4
GLA_DV = GROUP_WIDTH // GLA_HEADS
GLA_DK = GLA_DV // 2
GLA_GATE_RANK = 16
GLA_TAU = 16.0
GQA_HEADS = GROUP_WIDTH // HEAD_DIM
GQA_KV_HEADS = GQA_HEADS // 2
GDN_HEADS = GROUP_WIDTH // HEAD_DIM
GDN_DK = HEAD_DIM
GDN_DV = HEAD_DIM
CONV_K = 5
CHUNK = 64
ROPE_THETA = 10000.0
FFN_DIM = ((8 * D_MODEL // 3 + 127) // 128) * 128
N_MOD = 9
EPS = 1e-6
NEG_INF = -1e30

IN_SPLITS = (
    ('na_q', NA_HEADS * HEAD_DIM), ('na_k', NA_HEADS * HEAD_DIM), ('na_v', NA_HEADS * HEAD_DIM),
    ('gla_q', GLA_HEADS * GLA_DK), ('gla_k', GLA_HEADS * GLA_DK), ('gla_v', GLA_HEADS * GLA_DV),
    ('gla_r', GLA_HEADS * GLA_DV), ('gla_gf', GLA_GATE_RANK), ('gla_gb', GLA_GATE_RANK),
    ('gqa_q', GQA_HEADS * HEAD_DIM), ('gqa_k', GQA_KV_HEADS * HEAD_DIM), ('gqa_v', GQA_KV_HEADS * HEAD_DIM),
    ('gdn_qkv', 3 * GDN_HEADS * HEAD_DIM), ('gdn_z', GDN_HEADS * GDN_DV),
    ('gdn_b', 2 * GDN_HEADS), ('gdn_a', 2 * GDN_HEADS),
)

LANES = 128
FFN_TILE = 512
FFN_PAD = ((FFN_DIM + FFN_TILE - 1) // FFN_TILE) * FFN_TILE
TOKEN_TILE = 512
VMEM_LIMIT = 56 * 1024 * 1024

Z_ORDER = ('na_q', 'na_k', 'na_v', 'gla_q', 'gla_k', 'gla_v', 'gla_r', 'gqa_q', 'gqa_k', 'gqa_v',
           'gdn_qkv', 'gdn_z')
SMALL_ORDER = ('gla_gf', 'gla_gb', 'gdn_b', 'gdn_a')
SMALL_GB = GLA_GATE_RANK
SMALL_BETA = 2 * GLA_GATE_RANK
SMALL_A = SMALL_BETA + 2 * GDN_HEADS
N_LEVELS = int(math.log2(CHUNK))


def _layout():
    src, off = {}, 0
    for name, width in IN_SPLITS:
        src[name] = (off, width)
        off += width
    perm, zoff, pos = [], {}, 0
    for name in Z_ORDER:
        o, w = src[name]
        zoff[name] = pos
        perm.extend(range(o, o + w))
        pos += w
    zoff['small'] = pos
    n_small = 0
    for name in SMALL_ORDER:
        o, w = src[name]
        perm.extend(range(o, o + w))
        n_small += w
    perm.extend([-1] * (LANES - n_small))
    pos += LANES
    return np.asarray(perm, np.int32), zoff, pos


Z_PERM, Z_OFF, Z_COLS = _layout()
Z_TILE = (Z_COLS // LANES // 7) * LANES if (Z_COLS // LANES) % 7 == 0 else LANES

_f32 = jnp.float32
_bf16 = jnp.bfloat16


def _dot(a, b):
    return jnp.dot(a, b, preferred_element_type=_f32)


def _dot_nt(a, b):
    return lax.dot_general(a, b, (((1,), (1,)), ((), ())), preferred_element_type=_f32)


def _dot_tn(a, b):
    return lax.dot_general(a, b, (((0,), (0,)), ((), ())), preferred_element_type=_f32)


def _bf(x):
    return x.astype(_bf16)


def _sigmoid(x):
    return 1.0 / (1.0 + jnp.exp(-x))


def _silu(x):
    return x * _sigmoid(x)


def _softplus(x):
    return jnp.maximum(x, 0.0) + jnp.log(1.0 + jnp.exp(-jnp.abs(x)))


def _rms(x, w):
    return x * lax.rsqrt(jnp.mean(x * x, axis=-1, keepdims=True) + EPS) * w


def _split3(x):
    hi = _bf(x)
    r1 = x - hi.astype(_f32)
    mid = _bf(r1)
    lo = _bf(r1 - mid.astype(_f32))
    return hi, mid, lo


def _dot_split(a, b):
    a_hi = _bf(a)
    a_lo = _bf(a - a_hi.astype(_f32))
    b_hi = _bf(b)
    b_lo = _bf(b - b_hi.astype(_f32))
    return _dot(jnp.concatenate([a_hi, a_hi, a_lo], axis=1), jnp.concatenate([b_hi, b_lo, b_hi], axis=0))


def _cond_row(i, tile):
    n_ctx = (BATCH * SEQ) // tile
    return jnp.where(i < n_ctx, 0, 1 + (i - n_ctx) // (DEC_SEQ // tile))


def _params(sem):
    return pltpu.CompilerParams(dimension_semantics=sem, vmem_limit_bytes=VMEM_LIMIT)


def _adaln_kernel(c_ref, w_ref, b_ref, o_ref):
    a = _bf(_silu(c_ref[...]))
    o_ref[0] = _dot(a, _bf(w_ref[0])) + b_ref[0]


def adaln_all(cond, w_mod, b_mod):
    nc = cond.shape[0]
    tn = 1024
    n = N_MOD * D_MODEL
    return pl.pallas_call(
        _adaln_kernel,
        grid=(DEPTH, n // tn),
        in_specs=[pl.BlockSpec((nc, D_MODEL), lambda l, j: (0, 0)),
                  pl.BlockSpec((1, D_MODEL, tn), lambda l, j: (l, 0, j)),
                  pl.BlockSpec((1, 1, tn), lambda l, j: (l, 0, j))],
        out_specs=pl.BlockSpec((1, nc, tn), lambda l, j: (l, 0, j)),
        out_shape=jax.ShapeDtypeStruct((DEPTH, nc, n), _f32),
        compiler_params=_params(("parallel", "parallel")),
        name="adaln",
    )(cond, w_mod, b_mod.reshape(DEPTH, 1, n))


def _modulated_norm(x, g, m_ref, base):
    sh = m_ref[0, base:base + 1, :]
    sc = m_ref[0, base + 1:base + 2, :]
    return _rms(x, g) * (1.0 + sc) + sh


def _ffn_kernel(x_ref, m_ref, g_ref, wg_ref, wu_ref, wd_ref, o_ref, h_scr, acc_scr, *, base):
    f = pl.program_id(1)

    @pl.when(f == 0)
    def _():
        h_scr[...] = _bf(_modulated_norm(x_ref[...], g_ref[...], m_ref, base))
        acc_scr[...] = jnp.zeros_like(acc_scr)

    h = h_scr[...]
    gate = _dot(h, wg_ref[...])
    up = _dot(h, wu_ref[...])
    acc_scr[...] += _dot(_bf(_silu(gate) * up), wd_ref[...])

    @pl.when(f == pl.num_programs(1) - 1)
    def _():
        o_ref[...] = x_ref[...] + (0.5 * m_ref[0, base + 2:base + 3, :]) * acc_scr[...]


def ffn(x, mod, norm_g, w_gu, w_down, base):
    m, d = x.shape
    nf = FFN_PAD // FFN_TILE
    tm = TOKEN_TILE
    return pl.pallas_call(
        functools.partial(_ffn_kernel, base=base),
        grid=(m // tm, nf),
        in_specs=[pl.BlockSpec((tm, d), lambda i, f: (i, 0)),
                  pl.BlockSpec((1, N_MOD, d), lambda i, f: (_cond_row(i, tm), 0, 0)),
                  pl.BlockSpec((1, d), lambda i, f: (0, 0)),
                  pl.BlockSpec((d, FFN_TILE), lambda i, f: (0, f)),
                  pl.BlockSpec((d, FFN_TILE), lambda i, f: (0, f + nf)),
                  pl.BlockSpec((FFN_TILE, d), lambda i, f: (f, 0))],
        out_specs=pl.BlockSpec((tm, d), lambda i, f: (i, 0)),
        out_shape=jax.ShapeDtypeStruct((m, d), _f32),
        scratch_shapes=[pltpu.VMEM((tm, d), _bf16), pltpu.VMEM((tm, d), _f32)],
        compiler_params=_params(("parallel", "arbitrary")),
        name="ffn",
    )(x, mod, norm_g.reshape(1, d), w_gu, w_gu, w_down)


def _inproj_kernel(x_ref, m_ref, g_ref, w_ref, o_ref, h_scr):
    @pl.when(pl.program_id(1) == 0)
    def _():
        h_scr[...] = _bf(_modulated_norm(x_ref[...], g_ref[...], m_ref, 3))

    o_ref[...] = _dot(h_scr[...], w_ref[...])


def in_proj(x, mod, norm_g, w_in):
    m, d = x.shape
    tm = TOKEN_TILE
    return pl.pallas_call(
        _inproj_kernel,
        grid=(m // tm, Z_COLS // Z_TILE),
        in_specs=[pl.BlockSpec((tm, d), lambda i, j: (i, 0)),
                  pl.BlockSpec((1, N_MOD, d), lambda i, j: (_cond_row(i, tm), 0, 0)),
                  pl.BlockSpec((1, d), lambda i, j: (0, 0)),
                  pl.BlockSpec((d, Z_TILE), lambda i, j: (0, j))],
        out_specs=pl.BlockSpec((tm, Z_TILE), lambda i, j: (i, j)),
        out_shape=jax.ShapeDtypeStruct((m, Z_COLS), _f32),
        scratch_shapes=[pltpu.VMEM((tm, d), _bf16)],
        compiler_params=_params(("parallel", "arbitrary")),
        name="in_proj",
    )(x, mod, norm_g.reshape(1, d), w_in)


def _outproj_kernel(x_ref, m_ref, a_ref, b_ref, c_ref, d_ref, w_ref, o_ref):
    gw = GROUP_WIDTH
    acc = _dot(a_ref[...], w_ref[0:gw, :])
    acc += _dot(b_ref[...], w_ref[gw:2 * gw, :])
    acc += _dot(c_ref[...], w_ref[2 * gw:3 * gw, :])
    acc += _dot(d_ref[...], w_ref[3 * gw:4 * gw, :])
    o_ref[...] = x_ref[...] + m_ref[0, 5:6, :] * acc


def out_proj(x, mod, o_na, o_gla, o_gqa, o_gdn, w_out):
    m, d = x.shape
    tm = TOKEN_TILE
    grp = pl.BlockSpec((tm, GROUP_WIDTH), lambda i: (i, 0))
    return pl.pallas_call(
        _outproj_kernel,
        grid=(m // tm,),
        in_specs=[pl.BlockSpec((tm, d), lambda i: (i, 0)),
                  pl.BlockSpec((1, N_MOD, d), lambda i: (_cond_row(i, tm), 0, 0)),
                  grp, grp, grp, grp,
                  pl.BlockSpec((4 * GROUP_WIDTH, d), lambda i: (0, 0))],
        out_specs=pl.BlockSpec((tm, d), lambda i: (i, 0)),
        out_shape=jax.ShapeDtypeStruct((m, d), _f32),
        compiler_params=_params(("parallel",)),
        name="out_proj",
    )(x, mod, o_na, o_gla, o_gqa, o_gdn, w_out)


def _softmax_pv(scores, values):
    mx = scores[0].max(axis=-1, keepdims=True)
    for s in scores[1:]:
        mx = jnp.maximum(mx, s.max(axis=-1, keepdims=True))
    num, den = None, None
    for s, v in zip(scores, values):
        p = jnp.exp(s - mx)
        d_ = p.sum(axis=-1, keepdims=True)
        n_ = _dot(_bf(p), v)
        num = n_ if num is None else num + n_
        den = d_ if den is None else den + d_
    return num / den


def _rope(x, cos, sin):
    lane = lax.broadcasted_iota(jnp.int32, x.shape, 1)
    quarter = HEAD_DIM // 4
    partner = jnp.where((lane % (2 * quarter)) < quarter,
                        pltpu.roll(x, HEAD_DIM - quarter, 1), pltpu.roll(x, quarter, 1))
    return x * cos + partner * sin


def _ctx_attn_kernel(nq_ref, nk_ref, nv_ref, gq_ref, gk_ref, gv_ref, nw_ref, gw_ref,
                     ona_ref, ogqa_ref, kn_ref, vn_ref, ka_ref, va_ref):
    hd = HEAD_DIM
    scale = hd ** -0.5
    for h in range(NA_HEADS):
        cols = slice(h * hd, (h + 1) * hd)
        k = _rms(nk_ref[:, cols], nw_ref[1:2, :])
        v = nv_ref[:, cols]
        q = _rms(nq_ref[:, cols], nw_ref[0:1, :])
        kn_ref[0, h] = k
        vn_ref[0, h] = v
        s = _dot_nt(_bf(q), _bf(k)) * scale
        ona_ref[:, cols] = _bf(_softmax_pv([s], [_bf(v)]))
    group = GQA_HEADS // GQA_KV_HEADS
    for kv in range(GQA_KV_HEADS):
        cols = slice(kv * hd, (kv + 1) * hd)
        k = _rms(gk_ref[:, cols], gw_ref[1:2, :])
        v = gv_ref[:, cols]
        ka_ref[0, kv] = k
        va_ref[0, kv] = v
        for g in range(group):
            qcols = slice((kv * group + g) * hd, (kv * group + g + 1) * hd)
            q = _rms(gq_ref[:, qcols], gw_ref[0:1, :])
            s = _dot_nt(_bf(q), _bf(k)) * scale
            ogqa_ref[:, qcols] = _bf(_softmax_pv([s], [_bf(v)]))


def ctx_attention(z, na_w, gqa_w):
    t, hd = SEQ, HEAD_DIM

    def zspec(name, width):
        blk = Z_OFF[name] // width
        assert Z_OFF[name] % width == 0
        return pl.BlockSpec((t, width), lambda b: (b, blk))

    gw = GROUP_WIDTH
    kvw = GQA_KV_HEADS * hd
    outs = pl.pallas_call(
        _ctx_attn_kernel,
        grid=(BATCH,),
        in_specs=[zspec('na_q', gw), zspec('na_k', gw), zspec('na_v', gw),
                  zspec('gqa_q', gw), zspec('gqa_k', kvw), zspec('gqa_v', kvw),
                  pl.BlockSpec((2, hd), lambda b: (0, 0)), pl.BlockSpec((2, hd), lambda b: (0, 0))],
        out_specs=[pl.BlockSpec((t, gw), lambda b: (b, 0)), pl.BlockSpec((t, gw), lambda b: (b, 0)),
                   pl.BlockSpec((1, NA_HEADS, t, hd), lambda b: (b, 0, 0, 0)),
                   pl.BlockSpec((1, NA_HEADS, t, hd), lambda b: (b, 0, 0, 0)),
                   pl.BlockSpec((1, GQA_KV_HEADS, t, hd), lambda b: (b, 0, 0, 0)),
                   pl.BlockSpec((1, GQA_KV_HEADS, t, hd), lambda b: (b, 0, 0, 0))],
        out_shape=[jax.ShapeDtypeStruct((BATCH * t, gw), _bf16), jax.ShapeDtypeStruct((BATCH * t, gw), _bf16),
                   jax.ShapeDtypeStruct((BATCH, NA_HEADS, t, hd), _f32),
                   jax.ShapeDtypeStruct((BATCH, NA_HEADS, t, hd), _f32),
                   jax.ShapeDtypeStruct((BATCH, GQA_KV_HEADS, t, hd), _f32),
                   jax.ShapeDtypeStruct((BATCH, GQA_KV_HEADS, t, hd), _f32)],
        compiler_params=_params(("parallel",)),
        name="ctx_attention",
    )(z, z, z, z, z, z, na_w, gqa_w)
    return outs


NA_QROWS = 4
NA_KROWS = NA_QROWS + NA_WIN_ROWS
GRID_ROWS = DEC_SEQ // GRID_W


def _na_key_row0(blk):
    return min(max(blk * NA_QROWS - NA_WIN_ROWS // 2, 0), GRID_ROWS - NA_KROWS)


def _na_bias_kernel(rpb_ref, o_ref):
    l, h = pl.program_id(0), pl.program_id(1)
    n_dr, n_dc = 2 * NA_WIN_ROWS - 1, 2 * NA_WIN_COLS - 1
    base = (l * NA_HEADS + h) * (n_dr * n_dc)
    qc = lax.broadcasted_iota(jnp.int32, (GRID_W, GRID_W), 0)
    kc = lax.broadcasted_iota(jnp.int32, (GRID_W, GRID_W), 1)
    dc = jnp.clip(kc - qc + (NA_WIN_COLS - 1), 0, n_dc - 1)
    c0 = jnp.clip(qc - NA_WIN_COLS // 2, 0, GRID_W - NA_WIN_COLS)
    in_win = (kc >= c0) & (kc < c0 + NA_WIN_COLS)
    masked = jnp.full((GRID_W, GRID_W), NEG_INF, _f32)
    tiles = []
    for dr in range(n_dr):
        t = jnp.zeros((GRID_W, GRID_W), _f32)
        for j in range(n_dc):
            t = jnp.where(dc == j, rpb_ref[base + dr * n_dc + j], t)
        tiles.append(jnp.where(in_win, t, NEG_INF))
    for blk in range(GRID_ROWS // NA_QROWS):
        k0 = _na_key_row0(blk)
        for qi in range(NA_QROWS):
            r = blk * NA_QROWS + qi
            krow0 = min(max(r - NA_WIN_ROWS // 2, 0), GRID_ROWS - NA_WIN_ROWS)
            for kj in range(NA_KROWS):
                kr = k0 + kj
                ok = krow0 <= kr < krow0 + NA_WIN_ROWS
                tile = tiles[kr - r + NA_WIN_ROWS - 1] if ok else masked
                o_ref[0, 0, blk, qi * GRID_W:(qi + 1) * GRID_W, kj * GRID_W:(kj + 1) * GRID_W] = tile


def na_bias_tables(na_rpb):
    nblk = GRID_ROWS // NA_QROWS
    shape = (DEPTH, NA_HEADS, nblk, NA_QROWS * GRID_W, NA_KROWS * GRID_W)
    return pl.pallas_call(
        _na_bias_kernel,
        grid=(DEPTH, NA_HEADS),
        in_specs=[pl.BlockSpec(memory_space=pltpu.SMEM)],
        out_specs=pl.BlockSpec((1, 1) + shape[2:], lambda l, h: (l, h, 0, 0, 0)),
        out_shape=jax.ShapeDtypeStruct(shape, _f32),
        compiler_params=_params(("parallel", "parallel")),
        name="na_bias",
    )(na_rpb.reshape(-1))


def _na_lat_kernel(q_ref, k_ref, v_ref, ck_ref, cv_ref, bias_ref, nw_ref, o_ref, kn_scr, vb_scr):
    blk = pl.program_id(2)
    scale = HEAD_DIM ** -0.5

    @pl.when(blk == 0)
    def _():
        kn_scr[...] = _bf(_rms(k_ref[...], nw_ref[1:2, :]))
        vb_scr[...] = _bf(v_ref[...])

    nblk = GRID_ROWS // NA_QROWS
    row0 = jnp.int32(_na_key_row0(0))
    for b_ in range(1, nblk):
        row0 = jnp.where(blk == b_, _na_key_row0(b_), row0)
    start = pl.multiple_of(row0 * GRID_W, GRID_W)
    nkeys = NA_KROWS * GRID_W
    q = _bf(_rms(q_ref[...], nw_ref[0:1, :]))
    s_ctx = _dot_nt(q, _bf(ck_ref[0, 0, 0])) * scale
    s_loc = _dot_nt(q, kn_scr[pl.ds(start, nkeys), :]) * scale + bias_ref[0, 0, 0]
    o = _softmax_pv([s_ctx, s_loc], [_bf(cv_ref[0, 0, 0]), vb_scr[pl.ds(start, nkeys), :]])
    o_ref[...] = _bf(o)


def na_latent(z, cache_k, cache_v, bias, na_w, layer):
    t, hd = DEC_SEQ, HEAD_DIM
    tq = NA_QROWS * GRID_W
    nblk = t // tq
    ctx_tiles = (BATCH * SEQ) // tq
    ctx_seqs = (BATCH * SEQ) // t
    qb, kb, vb = Z_OFF['na_q'] // hd, Z_OFF['na_k'] // hd, Z_OFF['na_v'] // hd
    cache_spec = pl.BlockSpec((1, 1, 1, PAST_LEN, hd), lambda b, h, r: (b, layer, h, 0, 0))
    return pl.pallas_call(
        _na_lat_kernel,
        grid=(DEC_BATCH, NA_HEADS, nblk),
        in_specs=[pl.BlockSpec((tq, hd), lambda b, h, r: (ctx_tiles + b * nblk + r, qb + h)),
                  pl.BlockSpec((t, hd), lambda b, h, r: (ctx_seqs + b, kb + h)),
                  pl.BlockSpec((t, hd), lambda b, h, r: (ctx_seqs + b, vb + h)),
                  cache_spec, cache_spec,
                  pl.BlockSpec((1, 1, 1, tq, NA_KROWS * GRID_W), lambda b, h, r: (layer, h, r, 0, 0)),
                  pl.BlockSpec((2, hd), lambda b, h, r: (0, 0))],
        out_specs=pl.BlockSpec((tq, hd), lambda b, h, r: (b * nblk + r, h)),
        out_shape=jax.ShapeDtypeStruct((DEC_BATCH * t, GROUP_WIDTH), _bf16),
        scratch_shapes=[pltpu.VMEM((t, hd), _bf16), pltpu.VMEM((t, hd), _bf16)],
        compiler_params=_params(("parallel", "parallel", "arbitrary")),
        name="na_latent",
    )(z, z, z, cache_k, cache_v, bias, na_w)


GQA_TQ = 256


def _gqa_lat_kernel(q_ref, k_ref, v_ref, ck_ref, cv_ref, cq_ref, sq_ref, ck_all_ref, sk_all_ref, gw_ref,
                    o_ref, kr_scr, vb_scr):
    hd = HEAD_DIM
    scale = hd ** -0.5

    @pl.when(pl.program_id(2) == 0)
    def _():
        kr_scr[...] = _bf(_rope(_rms(k_ref[...], gw_ref[1:2, :]), ck_all_ref[...], sk_all_ref[...]))
        vb_scr[...] = _bf(v_ref[...])

    ck = _bf(ck_ref[0, 0, 0])
    cv = _bf(cv_ref[0, 0, 0])
    for g in range(GQA_HEADS // GQA_KV_HEADS):
        cols = slice(g * hd, (g + 1) * hd)
        q = _bf(_rope(_rms(q_ref[:, cols], gw_ref[0:1, :]), cq_ref[...], sq_ref[...]))
        s_ctx = _dot_nt(q, ck) * scale
        s_loc = _dot_nt(q, kr_scr[...]) * scale
        o_ref[:, cols] = _bf(_softmax_pv([s_ctx, s_loc], [cv, vb_scr[...]]))


def gqa_latent(z, cache_k, cache_v, cos, sin, gqa_w, layer):
    t, hd = DEC_SEQ, HEAD_DIM
    tq = GQA_TQ
    nblk = t // tq
    group = GQA_HEADS // GQA_KV_HEADS
    ctx_tiles = (BATCH * SEQ) // tq
    ctx_seqs = (BATCH * SEQ) // t
    qb = Z_OFF['gqa_q'] // (group * hd)
    kb, vb = Z_OFF['gqa_k'] // hd, Z_OFF['gqa_v'] // hd
    cache_spec = pl.BlockSpec((1, 1, 1, PAST_LEN, hd), lambda b, h, r: (b, layer, h, 0, 0))
    return pl.pallas_call(
        _gqa_lat_kernel,
        grid=(DEC_BATCH, GQA_KV_HEADS, nblk),
        in_specs=[pl.BlockSpec((tq, group * hd), lambda b, h, r: (ctx_tiles + b * nblk + r, qb + h)),
                  pl.BlockSpec((t, hd), lambda b, h, r: (ctx_seqs + b, kb + h)),
                  pl.BlockSpec((t, hd), lambda b, h, r: (ctx_seqs + b, vb + h)),
                  cache_spec, cache_spec,
                  pl.BlockSpec((tq, hd), lambda b, h, r: (r, 0)),
                  pl.BlockSpec((tq, hd), lambda b, h, r: (r, 0)),
                  pl.BlockSpec((t, hd), lambda b, h, r: (0, 0)),
                  pl.BlockSpec((t, hd), lambda b, h, r: (0, 0)),
                  pl.BlockSpec((2, hd), lambda b, h, r: (0, 0))],
        out_specs=pl.BlockSpec((tq, group * hd), lambda b, h, r: (b * nblk + r, h)),
        out_shape=jax.ShapeDtypeStruct((DEC_BATCH * t, GROUP_WIDTH), _bf16),
        scratch_shapes=[pltpu.VMEM((t, hd), _bf16), pltpu.VMEM((t, hd), _bf16)],
        compiler_params=_params(("parallel", "parallel", "arbitrary")),
        name="gqa_latent",
    )(z, z, z, cache_k, cache_v, cos, sin, cos, sin, gqa_w)


def _rope_tables():
    t = np.arange(DEC_SEQ)
    row = (t // GRID_W).astype(np.float32)
    col = (t % GRID_W).astype(np.float32)
    half = HEAD_DIM // 2
    inv = jnp.asarray(ROPE_THETA, _f32) ** (-jnp.arange(0, half, 2, dtype=_f32) / half)
    ang_r = jnp.asarray(row)[:, None] * inv[None, :]
    ang_c = jnp.asarray(col)[:, None] * inv[None, :]
    cos = jnp.concatenate([jnp.cos(ang_r), jnp.cos(ang_r), jnp.cos(ang_c), jnp.cos(ang_c)], axis=-1)
    sin = jnp.concatenate([-jnp.sin(ang_r), jnp.sin(ang_r), -jnp.sin(ang_c), jnp.sin(ang_c)], axis=-1)
    return cos, sin


def _chunk_constants():
    c = CHUNK
    t = np.arange(c)
    tri, a_cat, pair, mq, mk, causal, strict = [], [], [], [], [], [], []
    for d in range(2):
        tau = t if d == 0 else c - 1 - t
        incl = (tau[None, :] <= tau[:, None]).astype(np.float32)
        tri.append(incl)
        causal.append(incl)
        strict.append((tau[None, :] < tau[:, None]).astype(np.float32))
        rows, pm, qm, km = [], [], [], []
        for li in range(N_LEVELS):
            s = c >> (li + 1)
            ref_tau = 2 * s * (tau // (2 * s)) + s - 1
            sel = (tau[None, :] == ref_tau[:, None]).astype(np.float32)
            rows.append(incl - sel @ incl)
            odd = ((tau // s) % 2 == 1).astype(np.float32)
            pm.append(((tau[:, None] // (2 * s)) == (tau[None, :] // (2 * s))).astype(np.float32))
            qm.append(np.repeat(odd[:, None], GLA_DK, axis=1))
            km.append(np.repeat((1.0 - odd)[:, None], GLA_DK, axis=1))
        pm.append(np.eye(c, dtype=np.float32))
        rows.append(incl)
        rows.append((tau[None, :] > tau[:, None]).astype(np.float32))
        a = np.concatenate(rows, axis=0)
        a_cat.append(np.concatenate([a, a, a], axis=1))
        pair.append(np.stack(pm))
        mq.append(np.stack(qm))
        mk.append(np.stack(km))
    tri_cat = np.stack([np.concatenate([x, x, x], axis=1) for x in tri])
    sub = np.stack([pair[d][:N_LEVELS] * mq[d][:, :, :1] * np.swapaxes(mk[d][:, :, :1], 1, 2) for d in range(2)])
    return dict(sub=jnp.asarray(sub), a_cat=jnp.asarray(np.stack(a_cat), _bf16), pair=jnp.asarray(np.stack(pair)),
                mq=jnp.asarray(np.stack(mq)), mk=jnp.asarray(np.stack(mk)),
                tri_cat=jnp.asarray(tri_cat, _bf16),
                causal=jnp.asarray(np.stack([np.stack([causal[d], strict[d]]) for d in range(2)])))


def _chunk_pos(d, c, n):
    return c if d == 0 else n - 1 - c


def _gla_kernel(q_ref, k_ref, v_ref, r_ref, zs_ref, wg_ref, gb_ref, acat_ref, pair_ref, mq_ref, mk_ref,
                s0_ref, nw_ref, o_ref, sn_ref, la_scr, oacc_scr, s_scr, *, t):
    c, dk, dv, nh = CHUNK, GLA_DK, GLA_DV, GLA_HEADS
    n = t // c
    zs = _bf(zs_ref[...])
    for d in range(2):
        x = _dot(zs, wg_ref[d]) + gb_ref[d]
        la_scr[d] = (jnp.minimum(x, 0.0) - jnp.log(1.0 + jnp.exp(-jnp.abs(x)))) * (1.0 / GLA_TAU)
        for h in range(nh):
            s_scr[d * nh + h] = s0_ref[0, d, h]
    ones = jnp.ones((3 * c, LANES), _bf16)

    for d in range(2):
        def chunk(ci, carry, d=d):
            pos = _chunk_pos(d, ci, n)
            rows = pl.ds(pl.multiple_of(pos * c, c), c)
            gcat = jnp.concatenate(_split3(la_scr[d, rows, :]), axis=0)
            dall = _dot(acat_ref[d], gcat)
            blast = _dot_tn(gcat, ones)
            for h in range(nh):
                kc = slice(h * dk, (h + 1) * dk)
                vc = slice(h * dv, (h + 1) * dv)
                q = q_ref[rows, kc] * (dk ** -0.5)
                k = k_ref[rows, kc]
                v = _bf(v_ref[rows, vc])
                att = _dot_nt(_bf(q), _bf(k)) * pair_ref[d, N_LEVELS]
                for li in range(N_LEVELS):
                    f = jnp.exp(-jnp.abs(dall[li * c:(li + 1) * c, kc]))
                    att += _dot_nt(_bf(q * f * mq_ref[d, li]), _bf(k * f * mk_ref[d, li])) * pair_ref[d, li]
                eb = jnp.exp(dall[N_LEVELS * c:(N_LEVELS + 1) * c, kc])
                el = jnp.exp(dall[(N_LEVELS + 1) * c:(N_LEVELS + 2) * c, kc])
                s = s_scr[d * nh + h]
                o = _dot(_bf(q * eb), _bf(s)) + _dot(_bf(att), v)
                s_scr[d * nh + h] = s * jnp.exp(blast[kc, :]) + _dot_tn(_bf(k * el), v)
                if d == 0:
                    oacc_scr[rows, vc] = o
                else:
                    oacc_scr[rows, vc] += o
            return carry

        lax.fori_loop(0, n, chunk, 0)

    for d in range(2):
        for h in range(nh):
            sn_ref[0, d, h] = s_scr[d * nh + h]
    for h in range(nh):
        vc = slice(h * dv, (h + 1) * dv)
        o_ref[:, vc] = _bf(_rms(oacc_scr[:, vc], nw_ref[...]) * _silu(r_ref[:, vc]))


def gla(z, row_blk0, nseq, t, wg, gbias, consts, s0, out_norm):
    qw = GLA_HEADS * GLA_DK
    vw = GLA_HEADS * GLA_DV

    def zspec(name, width):
        blk = Z_OFF[name] // width
        assert Z_OFF[name] % width == 0
        return pl.BlockSpec((t, width), lambda b: (row_blk0 + b, blk))

    def full(a):
        return pl.BlockSpec(a.shape, lambda b, nd=a.ndim: (0,) * nd)

    state_spec = pl.BlockSpec((1, 2, GLA_HEADS, GLA_DK, GLA_DV), lambda b: (b, 0, 0, 0, 0))
    cs = [consts['a_cat'], consts['pair'], consts['mq'], consts['mk']]
    return pl.pallas_call(
        functools.partial(_gla_kernel, t=t),
        grid=(nseq,),
        in_specs=[zspec('gla_q', qw), zspec('gla_k', qw), zspec('gla_v', vw), zspec('gla_r', vw),
                  zspec('small', LANES), full(wg), full(gbias)] + [full(a) for a in cs]
                 + [state_spec, pl.BlockSpec((1, GLA_DV), lambda b: (0, 0))],
        out_specs=[pl.BlockSpec((t, vw), lambda b: (b, 0)), state_spec],
        out_shape=[jax.ShapeDtypeStruct((nseq * t, vw), _bf16),
                   jax.ShapeDtypeStruct((nseq, 2, GLA_HEADS, GLA_DK, GLA_DV), _f32)],
        scratch_shapes=[pltpu.VMEM((2, t, qw), _f32), pltpu.VMEM((t, vw), _f32),
                        pltpu.VMEM((2 * GLA_HEADS, GLA_DK, GLA_DV), _f32)],
        compiler_params=_params(("parallel",)),
        name="gla",
    )(z, z, z, z, z, wg, gbias, *cs, s0, out_norm.reshape(1, GLA_DV))


CONV_PAD = 8
ROW_BLK = 128


def _gdn_kernel(q_ref, k_ref, v_ref, zz_ref, zs_ref, cw_ref, alog_ref, dtb_ref, tri_ref, causal_ref, sub_ref,
                s0_ref, nw_ref, o_ref, sn_ref, xp_scr, qkv_scr, g_scr, beta_scr, oacc_scr, s_scr, *, t):
    c, hd, nh = CHUNK, HEAD_DIM, GDN_HEADS
    n = t // c
    w = nh * hd
    half = CONV_K // 2
    nblk = t // ROW_BLK
    eye = (lax.broadcasted_iota(jnp.int32, (c, c), 0) == lax.broadcasted_iota(jnp.int32, (c, c), 1)).astype(_f32)

    xp_scr[0:CONV_PAD, :] = jnp.zeros((CONV_PAD, w), _f32)
    xp_scr[CONV_PAD + t:2 * CONV_PAD + t, :] = jnp.zeros((CONV_PAD, w), _f32)
    for idx, src in enumerate((q_ref, k_ref, v_ref)):
        def copy_in(i, carry, src=src):
            r0 = pl.multiple_of(i * ROW_BLK, ROW_BLK)
            xp_scr[pl.ds(CONV_PAD + r0, ROW_BLK), :] = src[pl.ds(r0, ROW_BLK), :]
            return carry

        lax.fori_loop(0, nblk, copy_in, 0)

        def conv(i, carry, idx=idx):
            r0 = pl.multiple_of(i * ROW_BLK, ROW_BLK)
            win = xp_scr[pl.ds(r0, ROW_BLK + 2 * CONV_PAD), :]
            y = jnp.zeros((ROW_BLK, w), _f32)
            for j in range(CONV_K):
                lo = CONV_PAD + j - half
                y += win[lo:lo + ROW_BLK, :] * cw_ref[idx, j:j + 1, :]
            y = _silu(y)
            for h in range(nh):
                cols = slice(h * hd, (h + 1) * hd)
                yh = y[:, cols]
                if idx == 0:
                    yh = yh * lax.rsqrt(jnp.sum(yh * yh, axis=-1, keepdims=True) + EPS) * (hd ** -0.5)
                elif idx == 1:
                    yh = yh * lax.rsqrt(jnp.sum(yh * yh, axis=-1, keepdims=True) + EPS)
                qkv_scr[idx, pl.ds(r0, ROW_BLK), cols] = yh
            return carry

        lax.fori_loop(0, nblk, conv, 0)

    zs = zs_ref[...]
    g_scr[...] = -jnp.exp(alog_ref[...]) * _softplus(zs + dtb_ref[...])
    beta_scr[...] = _sigmoid(zs)
    for d in range(2):
        for h in range(nh):
            s_scr[d * nh + h] = s0_ref[0, d, h]

    for d in range(2):
        last = c - 1 if d == 0 else 0

        def chunk(ci, carry, d=d, last=last):
            pos = _chunk_pos(d, ci, n)
            rows = pl.ds(pl.multiple_of(pos * c, c), c)
            gcat = jnp.concatenate(_split3(g_scr[rows, :]), axis=0)
            b_all = _dot(tri_ref[d], gcat)
            b_t = b_all.T
            beta_all = beta_scr[rows, :]
            for h in range(nh):
                cols = slice(h * hd, (h + 1) * hd)
                ia = SMALL_A + d * nh + h
                ib = SMALL_BETA + d * nh + h
                bcol = b_all[:, ia:ia + 1]
                brow = b_t[ia:ia + 1, :]
                decay = jnp.exp(jnp.minimum(bcol - brow, 0.0)) * causal_ref[d, 0]
                beta = beta_all[:, ib:ib + 1]
                q = qkv_scr[0, rows, cols]
                k = qkv_scr[1, rows, cols]
                v = qkv_scr[2, rows, cols]
                kb = k * beta
                kbf = _bf(k)
                low = _dot_nt(_bf(kb), kbf) * decay * causal_ref[d, 1]
                eb = jnp.exp(bcol)
                rhs = jnp.concatenate([v * beta, kb * eb], axis=1)
                x = eye - low * sub_ref[d, N_LEVELS - 1]
                for li in range(N_LEVELS - 2, -1, -1):
                    x = x - _dot_split(_dot_split(x, low * sub_ref[d, li]), x)
                sol = _dot_split(x, rhs)
                u = sol[:, :hd]
                wv = sol[:, hd:]
                s = s_scr[d * nh + h]
                sb = _bf(s)
                v_new = u - _dot(_bf(wv), sb)
                att = _dot_nt(_bf(q), kbf) * decay
                o = _dot(_bf(q * eb), sb) + _dot(_bf(att), _bf(v_new))
                b_last = b_all[last:last + 1, ia:ia + 1]
                el = jnp.exp(b_last - bcol)
                s_scr[d * nh + h] = s * jnp.exp(b_last) + _dot_tn(_bf(k * el), _bf(v_new))
                if d == 0:
                    oacc_scr[rows, cols] = o
                else:
                    oacc_scr[rows, cols] += o
            return carry

        lax.fori_loop(0, n, chunk, 0)

    for d in range(2):
        for h in range(nh):
            sn_ref[0, d, h] = s_scr[d * nh + h]

    def epilogue(i, carry):
        r0 = pl.multiple_of(i * ROW_BLK, ROW_BLK)
        rr = pl.ds(r0, ROW_BLK)
        for h in range(nh):
            cols = slice(h * hd, (h + 1) * hd)
            o_ref[rr, cols] = _bf(_rms(oacc_scr[rr, cols], nw_ref[...]) * _silu(zz_ref[rr, cols]))
        return carry

    lax.fori_loop(0, nblk, epilogue, 0)


def gdn(z, row_blk0, nseq, t, conv_w, alog_lane, dtb_lane, consts, s0, out_norm):
    w = GDN_HEADS * HEAD_DIM
    qblk = Z_OFF['gdn_qkv'] // w
    assert Z_OFF['gdn_qkv'] % w == 0 and Z_OFF['gdn_z'] % w == 0 and Z_OFF['small'] % LANES == 0

    def zspec(blk, width):
        return pl.BlockSpec((t, width), lambda b: (row_blk0 + b, blk))

    def full(a):
        return pl.BlockSpec(a.shape, lambda b, nd=a.ndim: (0,) * nd)

    state_spec = pl.BlockSpec((1, 2, GDN_HEADS, GDN_DK, GDN_DV), lambda b: (b, 0, 0, 0, 0))
    return pl.pallas_call(
        functools.partial(_gdn_kernel, t=t),
        grid=(nseq,),
        in_specs=[zspec(qblk, w), zspec(qblk + 1, w), zspec(qblk + 2, w), zspec(Z_OFF['gdn_z'] // w, w),
                  zspec(Z_OFF['small'] // LANES, LANES), full(conv_w), full(alog_lane), full(dtb_lane),
                  full(consts['tri_cat']), full(consts['causal']), full(consts['sub']),
                  state_spec, pl.BlockSpec((1, GDN_DV), lambda b: (0, 0))],
        out_specs=[pl.BlockSpec((t, w), lambda b: (b, 0)), state_spec],
        out_shape=[jax.ShapeDtypeStruct((nseq * t, w), _bf16),
                   jax.ShapeDtypeStruct((nseq, 2, GDN_HEADS, GDN_DK, GDN_DV), _f32)],
        scratch_shapes=[pltpu.VMEM((t + 2 * CONV_PAD, w), _f32), pltpu.VMEM((3, t, w), _f32),
                        pltpu.VMEM((t, LANES), _f32), pltpu.VMEM((t, LANES), _f32),
                        pltpu.VMEM((t, w), _f32), pltpu.VMEM((2 * GDN_HEADS, GDN_DK, GDN_DV), _f32)],
        compiler_params=_params(("parallel",)),
        name="gdn",
    )(z, z, z, z, z, conv_w, alog_lane, dtb_lane, consts['tri_cat'], consts['causal'], consts['sub'], s0,
      out_norm.reshape(1, GDN_DV))


def _lane_vector(values, offset):
    k = values.shape[-1]
    return jnp.pad(values.astype(_f32), ((0, 0), (offset, LANES - offset - k)))[:, None, :]


def kernel(x_prompt, x_sample, cache_na_k, cache_na_v, cache_gqa_k, cache_gqa_v, state_gla, state_gdn, c, c_ctx, norm_g, w_mod, b_mod, ffn_gu, ffn_down, w_in, w_out, na_qk_norm, na_rpb, gla_gate_up, gla_gate_bias, gla_out_norm, gqa_qk_norm, gdn_conv, gdn_a_log, gdn_dt_bias, gdn_out_norm):
    assert GRID_ROWS % NA_QROWS == 0 and GRID_ROWS >= NA_KROWS and PAST_LEN == SEQ
    d = D_MODEL
    m_ctx = BATCH * SEQ
    assert m_ctx % DEC_SEQ == 0 and DEC_SEQ % TOKEN_TILE == 0
    x = jnp.concatenate([x_prompt.reshape(m_ctx, d), x_sample.reshape(DEC_BATCH * DEC_SEQ, d)], axis=0)

    n_cond = 1 + DEC_BATCH
    cond = jnp.concatenate([c_ctx[None, :], c], axis=0)
    cond = jnp.pad(cond, ((0, (-n_cond) % 8), (0, 0)))
    mod = adaln_all(cond, w_mod, b_mod).reshape(DEPTH, cond.shape[0], N_MOD, d)

    fpad = FFN_PAD - FFN_DIM
    gu = ffn_gu.reshape(DEPTH, 2, d, 2, FFN_DIM)
    gu = jnp.pad(gu, ((0, 0),) * 4 + ((0, fpad),)).astype(_bf16).reshape(DEPTH, 2, d, 2 * FFN_PAD)
    down = jnp.pad(ffn_down, ((0, 0), (0, 0), (0, fpad), (0, 0))).astype(_bf16)
    perm = jnp.asarray(np.maximum(Z_PERM, 0))
    keep = jnp.asarray((Z_PERM >= 0).astype(np.float32))
    w_in_p = (jnp.take(w_in, perm, axis=2) * keep).astype(_bf16)
    w_out_b = w_out.astype(_bf16)

    consts = _chunk_constants()
    cos, sin = _rope_tables()
    bias = na_bias_tables(na_rpb)

    qw = GLA_HEADS * GLA_DK
    wg = jnp.zeros((DEPTH, 2, LANES, qw), _f32)
    wg = wg.at[:, 0, 0:GLA_GATE_RANK].set(gla_gate_up[:, 0])
    wg = wg.at[:, 1, SMALL_GB:SMALL_GB + GLA_GATE_RANK].set(gla_gate_up[:, 1]).astype(_bf16)
    gbias = gla_gate_bias.reshape(DEPTH, 2, 1, qw)
    conv_w = gdn_conv.reshape(DEPTH, CONV_K, 3, GDN_HEADS * HEAD_DIM).transpose(0, 2, 1, 3)
    conv_w = jnp.pad(conv_w, ((0, 0), (0, 0), (0, 8 - CONV_K), (0, 0)))
    alog_lane = _lane_vector(gdn_a_log.reshape(DEPTH, 2 * GDN_HEADS), SMALL_A)
    dtb_lane = _lane_vector(gdn_dt_bias.reshape(DEPTH, 2 * GDN_HEADS), SMALL_A)
    gla_zero = jnp.zeros((BATCH, 2, GLA_HEADS, GLA_DK, GLA_DV), _f32)
    gdn_zero = jnp.zeros((BATCH, 2, GDN_HEADS, GDN_DK, GDN_DV), _f32)
    lat_blk0 = m_ctx // DEC_SEQ

    na_k_l, na_v_l, gqa_k_l, gqa_v_l, gla_l, gdn_l = [], [], [], [], [], []
    for l in range(DEPTH):
        x = ffn(x, mod[l], norm_g[l, 0], gu[l, 0], down[l, 0], 0)
        z = in_proj(x, mod[l], norm_g[l, 1], w_in_p[l])

        ona_c, ogqa_c, kn, vn, ka, va = ctx_attention(z, na_qk_norm[l], gqa_qk_norm[l])
        ona_l = na_latent(z, cache_na_k, cache_na_v, bias, na_qk_norm[l], l)
        ogqa_l = gqa_latent(z, cache_gqa_k, cache_gqa_v, cos, sin, gqa_qk_norm[l], l)
        ogla_c, sg = gla(z, 0, BATCH, SEQ, wg[l], gbias[l], consts, gla_zero, gla_out_norm[l])
        ogla_l, _ = gla(z, lat_blk0, DEC_BATCH, DEC_SEQ, wg[l], gbias[l], consts, state_gla[:, l],
                        gla_out_norm[l])
        ogdn_c, sd = gdn(z, 0, BATCH, SEQ, conv_w[l], alog_lane[l], dtb_lane[l], consts, gdn_zero,
                         gdn_out_norm[l])
        ogdn_l, _ = gdn(z, lat_blk0, DEC_BATCH, DEC_SEQ, conv_w[l], alog_lane[l], dtb_lane[l], consts,
                        state_gdn[:, l], gdn_out_norm[l])

        x = out_proj(x, mod[l], jnp.concatenate([ona_c, ona_l]), jnp.concatenate([ogla_c, ogla_l]),
                     jnp.concatenate([ogqa_c, ogqa_l]), jnp.concatenate([ogdn_c, ogdn_l]), w_out_b[l])
        x = ffn(x, mod[l], norm_g[l, 2], gu[l, 1], down[l, 1], 6)

        na_k_l.append(kn)
        na_v_l.append(vn)
        gqa_k_l.append(ka)
        gqa_v_l.append(va)
        gla_l.append(sg)
        gdn_l.append(sd)

    y_prompt = x[:m_ctx].reshape(BATCH, SEQ, d)
    y_sample = x[m_ctx:].reshape(DEC_BATCH, DEC_SEQ, d)
    return (y_prompt, y_sample, jnp.stack(na_k_l, axis=1), jnp.stack(na_v_l, axis=1),
            jnp.stack(gqa_k_l, axis=1), jnp.stack(gqa_v_l, axis=1),
            jnp.stack(gla_l, axis=1), jnp.stack(gdn_l, axis=1))
```

```python
import functools
import math

import jax
import jax.numpy as jnp
import numpy as np
from jax import lax
from jax.experimental import pallas as pl
from jax.experimental.pallas import tpu as pltpu

D_MODEL = 2048
BATCH = 16
SEQ = 256
DEPTH = 4
DEC_BATCH = 8
DEC_SEQ = 1024
PAST_LEN = 256

GRID_W = 64
HEAD_DIM = 128
GROUP_WIDTH = D_MODEL // 4
NA_HEADS = GROUP_WIDTH // HEAD_DIM
NA_WIN_ROWS = 8
NA_WIN_COLS = 16
GLA_HEADS = 4
GLA_DV = GROUP_WIDTH // GLA_HEADS
GLA_DK = GLA_DV // 2
GLA_GATE_RANK = 16
GLA_TAU = 16.0
GQA_HEADS = GROUP_WIDTH // HEAD_DIM
GQA_KV_HEADS = GQA_HEADS // 2
GDN_HEADS = GROUP_WIDTH // HEAD_DIM
GDN_DK = HEAD_DIM
GDN_DV = HEAD_DIM
CONV_K = 5
CHUNK = 64
ROPE_THETA = 10000.0
FFN_DIM = ((8 * D_MODEL // 3 + 127) // 128) * 128
N_MOD = 9
EPS = 1e-6
NEG_INF = -1e30

IN_SPLITS = (
    ('na_q', NA_HEADS * HEAD_DIM), ('na_k', NA_HEADS * HEAD_DIM), ('na_v', NA_HEADS * HEAD_DIM),
    ('gla_q', GLA_HEADS * GLA_DK), ('gla_k', GLA_HEADS * GLA_DK), ('gla_v', GLA_HEADS * GLA_DV),
    ('gla_r', GLA_HEADS * GLA_DV), ('gla_gf', GLA_GATE_RANK), ('gla_gb', GLA_GATE_RANK),
    ('gqa_q', GQA_HEADS * HEAD_DIM), ('gqa_k', GQA_KV_HEADS * HEAD_DIM), ('gqa_v', GQA_KV_HEADS * HEAD_DIM),
    ('gdn_qkv', 3 * GDN_HEADS * HEAD_DIM), ('gdn_z', GDN_HEADS * GDN_DV),
    ('gdn_b', 2 * GDN_HEADS), ('gdn_a', 2 * GDN_HEADS),
)

LANES = 128
FFN_TILE = 512
FFN_PAD = ((FFN_DIM + FFN_TILE - 1) // FFN_TILE) * FFN_TILE
TOKEN_TILE = 512
VMEM_LIMIT = 56 * 1024 * 1024

Z_ORDER = ('na_q', 'na_k', 'na_v', 'gla_q', 'gla_k', 'gla_v', 'gla_r', 'gqa_q', 'gqa_k', 'gqa_v',
           'gdn_qkv', 'gdn_z')
SMALL_ORDER = ('gla_gf', 'gla_gb', 'gdn_b', 'gdn_a')
SMALL_GB = GLA_GATE_RANK
SMALL_BETA = 2 * GLA_GATE_RANK
SMALL_A = SMALL_BETA + 2 * GDN_HEADS
N_LEVELS = int(math.log2(CHUNK))


def _layout():
    src, off = {}, 0
    for name, width in IN_SPLITS:
        src[name] = (off, width)
        off += width
    perm, zoff, pos = [], {}, 0
    for name in Z_ORDER:
        o, w = src[name]
        zoff[name] = pos
        perm.extend(range(o, o + w))
        pos += w
    zoff['small'] = pos
    n_small = 0
    for name in SMALL_ORDER:
        o, w = src[name]
        perm.extend(range(o, o + w))
        n_small += w
    perm.extend([-1] * (LANES - n_small))
    pos += LANES
    return np.asarray(perm, np.int32), zoff, pos


Z_PERM, Z_OFF, Z_COLS = _layout()
Z_TILE = (Z_COLS // LANES // 7) * LANES if (Z_COLS // LANES) % 7 == 0 else LANES

_f32 = jnp.float32
_bf16 = jnp.bfloat16


def _dot(a, b):
    return jnp.dot(a, b, preferred_element_type=_f32)


def _dot_nt(a, b):
    return lax.dot_general(a, b, (((1,), (1,)), ((), ())), preferred_element_type=_f32)


def _dot_tn(a, b):
    return lax.dot_general(a, b, (((0,), (0,)), ((), ())), preferred_element_type=_f32)


def _bf(x):
    return x.astype(_bf16)


def _sigmoid(x):
    return 1.0 / (1.0 + jnp.exp(-x))


def _silu(x):
    return x * _sigmoid(x)


def _softplus(x):
    return jnp.maximum(x, 0.0) + jnp.log(1.0 + jnp.exp(-jnp.abs(x)))


def _rms(x, w):
    return x * lax.rsqrt(jnp.mean(x * x, axis=-1, keepdims=True) + EPS) * w


def _split3(x):
    hi = _bf(x)
    r1 = x - hi.astype(_f32)
    mid = _bf(r1)
    lo = _bf(r1 - mid.astype(_f32))
    return hi, mid, lo


def _split2(x):
    hi = _bf(x)
    return hi, _bf(x - hi.astype(_f32))


def _dot_parts(a, b):
    return _dot(jnp.concatenate([a[0], a[0], a[1]], axis=1), jnp.concatenate([b[0], b[1], b[0]], axis=0))


def _cond_row(i, tile):
    n_ctx = (BATCH * SEQ) // tile
    return jnp.where(i < n_ctx, 0, 1 + (i - n_ctx) // (DEC_SEQ // tile))


def _params(sem):
    return pltpu.CompilerParams(dimension_semantics=sem, vmem_limit_bytes=VMEM_LIMIT)


def _adaln_kernel(c_ref, w_ref, b_ref, o_ref):
    a = _bf(_silu(c_ref[...]))
    o_ref[0] = _dot(a, _bf(w_ref[0])) + b_ref[0]


def adaln_all(cond, w_mod, b_mod):
    nc = cond.shape[0]
    tn = 1024
    n = N_MOD * D_MODEL
    return pl.pallas_call(
        _adaln_kernel,
        grid=(DEPTH, n // tn),
        in_specs=[pl.BlockSpec((nc, D_MODEL), lambda l, j: (0, 0)),
                  pl.BlockSpec((1, D_MODEL, tn), lambda l, j: (l, 0, j)),
                  pl.BlockSpec((1, 1, tn), lambda l, j: (l, 0, j))],
        out_specs=pl.BlockSpec((1, nc, tn), lambda l, j: (l, 0, j)),
        out_shape=jax.ShapeDtypeStruct((DEPTH, nc, n), _f32),
        compiler_params=_params(("parallel", "parallel")),
        name="adaln",
    )(cond, w_mod, b_mod.reshape(DEPTH, 1, n))


def _modulated_norm(x, g, m_ref, base):
    sh = m_ref[0, base:base + 1, :]
    sc = m_ref[0, base + 1:base + 2, :]
    return _rms(x, g) * (1.0 + sc) + sh


def _ffn_kernel(x_ref, m_ref, g_ref, wg_ref, wu_ref, wd_ref, o_ref, h_scr, acc_scr, *, base):
    f = pl.program_id(1)

    @pl.when(f == 0)
    def _():
        h_scr[...] = _bf(_modulated_norm(x_ref[...], g_ref[...], m_ref, base))
        acc_scr[...] = jnp.zeros_like(acc_scr)

    h = h_scr[...]
    gate = _dot(h, wg_ref[...])
    up = _dot(h, wu_ref[...])
    acc_scr[...] += _dot(_bf(_silu(gate) * up), wd_ref[...])

    @pl.when(f == pl.num_programs(1) - 1)
    def _():
        o_ref[...] = x_ref[...] + (0.5 * m_ref[0, base + 2:base + 3, :]) * acc_scr[...]


def ffn(x, mod, norm_g, w_gate, w_up, w_down, base):
    m, d = x.shape
    nf = FFN_PAD // FFN_TILE
    tm = TOKEN_TILE
    return pl.pallas_call(
        functools.partial(_ffn_kernel, base=base),
        grid=(m // tm, nf),
        in_specs=[pl.BlockSpec((tm, d), lambda i, f: (i, 0)),
                  pl.BlockSpec((1, N_MOD, d), lambda i, f: (_cond_row(i, tm), 0, 0)),
                  pl.BlockSpec((1, d), lambda i, f: (0, 0)),
                  pl.BlockSpec((d, FFN_TILE), lambda i, f: (0, f)),
                  pl.BlockSpec((d, FFN_TILE), lambda i, f: (0, f)),
                  pl.BlockSpec((FFN_TILE, d), lambda i, f: (f, 0))],
        out_specs=pl.BlockSpec((tm, d), lambda i, f: (i, 0)),
        out_shape=jax.ShapeDtypeStruct((m, d), _f32),
        scratch_shapes=[pltpu.VMEM((tm, d), _bf16), pltpu.VMEM((tm, d), _f32)],
        compiler_params=_params(("parallel", "arbitrary")),
        name="ffn",
    )(x, mod, norm_g.reshape(1, d), w_gate, w_up, w_down)


def _inproj_kernel(x_ref, m_ref, g_ref, w_ref, o_ref, h_scr):
    @pl.when(pl.program_id(1) == 0)
    def _():
        h_scr[...] = _bf(_modulated_norm(x_ref[...], g_ref[...], m_ref, 3))

    o_ref[...] = _dot(h_scr[...], w_ref[...])


def in_proj(x, mod, norm_g, w_in):
    m, d = x.shape
    tm = TOKEN_TILE
    return pl.pallas_call(
        _inproj_kernel,
        grid=(m // tm, Z_COLS // Z_TILE),
        in_specs=[pl.BlockSpec((tm, d), lambda i, j: (i, 0)),
                  pl.BlockSpec((1, N_MOD, d), lambda i, j: (_cond_row(i, tm), 0, 0)),
                  pl.BlockSpec((1, d), lambda i, j: (0, 0)),
                  pl.BlockSpec((d, Z_TILE), lambda i, j: (0, j))],
        out_specs=pl.BlockSpec((tm, Z_TILE), lambda i, j: (i, j)),
        out_shape=jax.ShapeDtypeStruct((m, Z_COLS), _f32),
        scratch_shapes=[pltpu.VMEM((tm, d), _bf16)],
        compiler_params=_params(("parallel", "arbitrary")),
        name="in_proj",
    )(x, mod, norm_g.reshape(1, d), w_in)


def _outproj_kernel(x_ref, m_ref, a_ref, b_ref, c_ref, d_ref, w_ref, o_ref):
    gw = GROUP_WIDTH
    acc = _dot(a_ref[...], w_ref[0:gw, :])
    acc += _dot(b_ref[...], w_ref[gw:2 * gw, :])
    acc += _dot(c_ref[...], w_ref[2 * gw:3 * gw, :])
    acc += _dot(d_ref[...], w_ref[3 * gw:4 * gw, :])
    o_ref[...] = x_ref[...] + m_ref[0, 5:6, :] * acc


def out_proj(x, mod, o_na, o_gla, o_gqa, o_gdn, w_out):
    m, d = x.shape
    tm = TOKEN_TILE
    grp = pl.BlockSpec((tm, GROUP_WIDTH), lambda i: (i, 0))
    return pl.pallas_call(
        _outproj_kernel,
        grid=(m // tm,),
        in_specs=[pl.BlockSpec((tm, d), lambda i: (i, 0)),
                  pl.BlockSpec((1, N_MOD, d), lambda i: (_cond_row(i, tm), 0, 0)),
                  grp, grp, grp, grp,
                  pl.BlockSpec((4 * GROUP_WIDTH, d), lambda i: (0, 0))],
        out_specs=pl.BlockSpec((tm, d), lambda i: (i, 0)),
        out_shape=jax.ShapeDtypeStruct((m, d), _f32),
        compiler_params=_params(("parallel",)),
        name="out_proj",
    )(x, mod, o_na, o_gla, o_gqa, o_gdn, w_out)


def _softmax_pv(scores, values):
    mx = scores[0].max(axis=-1, keepdims=True)
    for s in scores[1:]:
        mx = jnp.maximum(mx, s.max(axis=-1, keepdims=True))
    num, den = None, None
    for s, v in zip(scores, values):
        p = jnp.exp(s - mx)
        d_ = p.sum(axis=-1, keepdims=True)
        n_ = _dot(_bf(p), v)
        num = n_ if num is None else num + n_
        den = d_ if den is None else den + d_
    return num / den


def _rope(x, cos, sin):
    lane = lax.broadcasted_iota(jnp.int32, x.shape, 1)
    quarter = HEAD_DIM // 4
    partner = jnp.where((lane % (2 * quarter)) < quarter,
                        pltpu.roll(x, HEAD_DIM - quarter, 1), pltpu.roll(x, quarter, 1))
    return x * cos + partner * sin


def _ctx_attn_kernel(nq_ref, nk_ref, nv_ref, gq_ref, gk_ref, gv_ref, nw_ref, gw_ref,
                     ona_ref, ogqa_ref, kn_ref, vn_ref, ka_ref, va_ref):
    hd = HEAD_DIM
    scale = hd ** -0.5
    for h in range(NA_HEADS):
        cols = slice(h * hd, (h + 1) * hd)
        k = _rms(nk_ref[:, cols], nw_ref[1:2, :])
        v = nv_ref[:, cols]
        q = _rms(nq_ref[:, cols], nw_ref[0:1, :])
        kn_ref[0, h] = k
        vn_ref[0, h] = v
        s = _dot_nt(_bf(q), _bf(k)) * scale
        ona_ref[:, cols] = _bf(_softmax_pv([s], [_bf(v)]))
    group = GQA_HEADS // GQA_KV_HEADS
    for kv in range(GQA_KV_HEADS):
        cols = slice(kv * hd, (kv + 1) * hd)
        k = _rms(gk_ref[:, cols], gw_ref[1:2, :])
        v = gv_ref[:, cols]
        ka_ref[0, kv] = k
        va_ref[0, kv] = v
        for g in range(group):
            qcols = slice((kv * group + g) * hd, (kv * group + g + 1) * hd)
            q = _rms(gq_ref[:, qcols], gw_ref[0:1, :])
            s = _dot_nt(_bf(q), _bf(k)) * scale
            ogqa_ref[:, qcols] = _bf(_softmax_pv([s], [_bf(v)]))


def ctx_attention(z, na_w, gqa_w):
    t, hd = SEQ, HEAD_DIM

    def zspec(name, width):
        blk = Z_OFF[name] // width
        assert Z_OFF[name] % width == 0
        return pl.BlockSpec((t, width), lambda b: (b, blk))

    gw = GROUP_WIDTH
    kvw = GQA_KV_HEADS * hd
    outs = pl.pallas_call(
        _ctx_attn_kernel,
        grid=(BATCH,),
        in_specs=[zspec('na_q', gw), zspec('na_k', gw), zspec('na_v', gw),
                  zspec('gqa_q', gw), zspec('gqa_k', kvw), zspec('gqa_v', kvw),
                  pl.BlockSpec((2, hd), lambda b: (0, 0)), pl.BlockSpec((2, hd), lambda b: (0, 0))],
        out_specs=[pl.BlockSpec((t, gw), lambda b: (b, 0)), pl.BlockSpec((t, gw), lambda b: (b, 0)),
                   pl.BlockSpec((1, NA_HEADS, t, hd), lambda b: (b, 0, 0, 0)),
                   pl.BlockSpec((1, NA_HEADS, t, hd), lambda b: (b, 0, 0, 0)),
                   pl.BlockSpec((1, GQA_KV_HEADS, t, hd), lambda b: (b, 0, 0, 0)),
                   pl.BlockSpec((1, GQA_KV_HEADS, t, hd), lambda b: (b, 0, 0, 0))],
        out_shape=[jax.ShapeDtypeStruct((z.shape[0], gw), _bf16), jax.ShapeDtypeStruct((z.shape[0], gw), _bf16),
                   jax.ShapeDtypeStruct((BATCH, NA_HEADS, t, hd), _f32),
                   jax.ShapeDtypeStruct((BATCH, NA_HEADS, t, hd), _f32),
                   jax.ShapeDtypeStruct((BATCH, GQA_KV_HEADS, t, hd), _f32),
                   jax.ShapeDtypeStruct((BATCH, GQA_KV_HEADS, t, hd), _f32)],
        compiler_params=_params(("parallel",)),
        name="ctx_attention",
    )(z, z, z, z, z, z, na_w, gqa_w)
    return outs


NA_QROWS = 4
NA_KROWS = NA_QROWS + NA_WIN_ROWS
GRID_ROWS = DEC_SEQ // GRID_W


def _na_key_row0(blk):
    return min(max(blk * NA_QROWS - NA_WIN_ROWS // 2, 0), GRID_ROWS - NA_KROWS)


def _na_bias_kernel(rpb_ref, o_ref):
    l, h = pl.program_id(0), pl.program_id(1)
    n_dr, n_dc = 2 * NA_WIN_ROWS - 1, 2 * NA_WIN_COLS - 1
    base = (l * NA_HEADS + h) * (n_dr * n_dc)
    qc = lax.broadcasted_iota(jnp.int32, (GRID_W, GRID_W), 0)
    kc = lax.broadcasted_iota(jnp.int32, (GRID_W, GRID_W), 1)
    dc = jnp.clip(kc - qc + (NA_WIN_COLS - 1), 0, n_dc - 1)
    c0 = jnp.clip(qc - NA_WIN_COLS // 2, 0, GRID_W - NA_WIN_COLS)
    in_win = (kc >= c0) & (kc < c0 + NA_WIN_COLS)
    masked = jnp.full((GRID_W, GRID_W), NEG_INF, _f32)
    tiles = []
    for dr in range(n_dr):
        t = jnp.zeros((GRID_W, GRID_W), _f32)
        for j in range(n_dc):
            t = jnp.where(dc == j, rpb_ref[base + dr * n_dc + j], t)
        tiles.append(jnp.where(in_win, t, NEG_INF))
    for blk in range(GRID_ROWS // NA_QROWS):
        k0 = _na_key_row0(blk)
        for qi in range(NA_QROWS):
            r = blk * NA_QROWS + qi
            krow0 = min(max(r - NA_WIN_ROWS // 2, 0), GRID_ROWS - NA_WIN_ROWS)
            for kj in range(NA_KROWS):
                kr = k0 + kj
                ok = krow0 <= kr < krow0 + NA_WIN_ROWS
                tile = tiles[kr - r + NA_WIN_ROWS - 1] if ok else masked
                o_ref[0, 0, blk, qi * GRID_W:(qi + 1) * GRID_W, kj * GRID_W:(kj + 1) * GRID_W] = tile


def na_bias_tables(na_rpb):
    nblk = GRID_ROWS // NA_QROWS
    shape = (DEPTH, NA_HEADS, nblk, NA_QROWS * GRID_W, NA_KROWS * GRID_W)
    return pl.pallas_call(
        _na_bias_kernel,
        grid=(DEPTH, NA_HEADS),
        in_specs=[pl.BlockSpec(memory_space=pltpu.SMEM)],
        out_specs=pl.BlockSpec((1, 1) + shape[2:], lambda l, h: (l, h, 0, 0, 0)),
        out_shape=jax.ShapeDtypeStruct(shape, _f32),
        compiler_params=_params(("parallel", "parallel")),
        name="na_bias",
    )(na_rpb.reshape(-1))


def _na_lat_kernel(q_ref, k_ref, v_ref, ck_ref, cv_ref, bias_ref, nw_ref, _prev_ref, o_ref, kn_scr, vb_scr):
    blk = pl.program_id(2)
    scale = HEAD_DIM ** -0.5

    @pl.when(blk == 0)
    def _():
        kn_scr[...] = _bf(_rms(k_ref[...], nw_ref[1:2, :]))
        vb_scr[...] = _bf(v_ref[...])

    nblk = GRID_ROWS // NA_QROWS
    row0 = jnp.int32(_na_key_row0(0))
    for b_ in range(1, nblk):
        row0 = jnp.where(blk == b_, _na_key_row0(b_), row0)
    start = pl.multiple_of(row0 * GRID_W, GRID_W)
    nkeys = NA_KROWS * GRID_W
    q = _bf(_rms(q_ref[...], nw_ref[0:1, :]))
    s_ctx = _dot_nt(q, _bf(ck_ref[0, 0, 0])) * scale
    s_loc = _dot_nt(q, kn_scr[pl.ds(start, nkeys), :]) * scale + bias_ref[0, 0, 0]
    o = _softmax_pv([s_ctx, s_loc], [_bf(cv_ref[0, 0, 0]), vb_scr[pl.ds(start, nkeys), :]])
    o_ref[...] = _bf(o)


def na_latent(z, cache_k, cache_v, bias, na_w, layer, prev):
    t, hd = DEC_SEQ, HEAD_DIM
    tq = NA_QROWS * GRID_W
    nblk = t // tq
    ctx_tiles = (BATCH * SEQ) // tq
    ctx_seqs = (BATCH * SEQ) // t
    qb, kb, vb = Z_OFF['na_q'] // hd, Z_OFF['na_k'] // hd, Z_OFF['na_v'] // hd
    cache_spec = pl.BlockSpec((1, 1, 1, PAST_LEN, hd), lambda b, h, r: (b, layer, h, 0, 0))
    return pl.pallas_call(
        _na_lat_kernel,
        grid=(DEC_BATCH, NA_HEADS, nblk),
        in_specs=[pl.BlockSpec((tq, hd), lambda b, h, r: (ctx_tiles + b * nblk + r, qb + h)),
                  pl.BlockSpec((t, hd), lambda b, h, r: (ctx_seqs + b, kb + h)),
                  pl.BlockSpec((t, hd), lambda b, h, r: (ctx_seqs + b, vb + h)),
                  cache_spec, cache_spec,
                  pl.BlockSpec((1, 1, 1, tq, NA_KROWS * GRID_W), lambda b, h, r: (layer, h, r, 0, 0)),
                  pl.BlockSpec((2, hd), lambda b, h, r: (0, 0)),
                  pl.BlockSpec(memory_space=pl.ANY)],
        out_specs=pl.BlockSpec((tq, hd), lambda b, h, r: (ctx_tiles + b * nblk + r, h)),
        out_shape=jax.ShapeDtypeStruct(prev.shape, prev.dtype),
        input_output_aliases={7: 0},
        scratch_shapes=[pltpu.VMEM((t, hd), _bf16), pltpu.VMEM((t, hd), _bf16)],
        compiler_params=_params(("parallel", "parallel", "arbitrary")),
        name="na_latent",
    )(z, z, z, cache_k, cache_v, bias, na_w, prev)


GQA_TQ = 256


def _gqa_lat_kernel(q_ref, k_ref, v_ref, ck_ref, cv_ref, cq_ref, sq_ref, ck_all_ref, sk_all_ref, gw_ref,
                    _prev_ref, o_ref, kr_scr, vb_scr):
    hd = HEAD_DIM
    scale = hd ** -0.5

    @pl.when(pl.program_id(2) == 0)
    def _():
        kr_scr[...] = _bf(_rope(_rms(k_ref[...], gw_ref[1:2, :]), ck_all_ref[...], sk_all_ref[...]))
        vb_scr[...] = _bf(v_ref[...])

    ck = _bf(ck_ref[0, 0, 0])
    cv = _bf(cv_ref[0, 0, 0])
    for g in range(GQA_HEADS // GQA_KV_HEADS):
        cols = slice(g * hd, (g + 1) * hd)
        q = _bf(_rope(_rms(q_ref[:, cols], gw_ref[0:1, :]), cq_ref[...], sq_ref[...]))
        s_ctx = _dot_nt(q, ck) * scale
        s_loc = _dot_nt(q, kr_scr[...]) * scale
        o_ref[:, cols] = _bf(_softmax_pv([s_ctx, s_loc], [cv, vb_scr[...]]))


def gqa_latent(z, cache_k, cache_v, cos, sin, gqa_w, layer, prev):
    t, hd = DEC_SEQ, HEAD_DIM
    tq = GQA_TQ
    nblk = t // tq
    group = GQA_HEADS // GQA_KV_HEADS
    ctx_tiles = (BATCH * SEQ) // tq
    ctx_seqs = (BATCH * SEQ) // t
    qb = Z_OFF['gqa_q'] // (group * hd)
    kb, vb = Z_OFF['gqa_k'] // hd, Z_OFF['gqa_v'] // hd
    cache_spec = pl.BlockSpec((1, 1, 1, PAST_LEN, hd), lambda b, h, r: (b, layer, h, 0, 0))
    return pl.pallas_call(
        _gqa_lat_kernel,
        grid=(DEC_BATCH, GQA_KV_HEADS, nblk),
        in_specs=[pl.BlockSpec((tq, group * hd), lambda b, h, r: (ctx_tiles + b * nblk + r, qb + h)),
                  pl.BlockSpec((t, hd), lambda b, h, r: (ctx_seqs + b, kb + h)),
                  pl.BlockSpec((t, hd), lambda b, h, r: (ctx_seqs + b, vb + h)),
                  cache_spec, cache_spec,
                  pl.BlockSpec((tq, hd), lambda b, h, r: (r, 0)),
                  pl.BlockSpec((tq, hd), lambda b, h, r: (r, 0)),
                  pl.BlockSpec((t, hd), lambda b, h, r: (0, 0)),
                  pl.BlockSpec((t, hd), lambda b, h, r: (0, 0)),
                  pl.BlockSpec((2, hd), lambda b, h, r: (0, 0)),
                  pl.BlockSpec(memory_space=pl.ANY)],
        out_specs=pl.BlockSpec((tq, group * hd), lambda b, h, r: (ctx_tiles + b * nblk + r, h)),
        out_shape=jax.ShapeDtypeStruct(prev.shape, prev.dtype),
        input_output_aliases={10: 0},
        scratch_shapes=[pltpu.VMEM((t, hd), _bf16), pltpu.VMEM((t, hd), _bf16)],
        compiler_params=_params(("parallel", "parallel", "arbitrary")),
        name="gqa_latent",
    )(z, z, z, cache_k, cache_v, cos, sin, cos, sin, gqa_w, prev)


def _rope_tables():
    t = np.arange(DEC_SEQ)
    row = (t // GRID_W).astype(np.float32)
    col = (t % GRID_W).astype(np.float32)
    half = HEAD_DIM // 2
    inv = jnp.asarray(ROPE_THETA, _f32) ** (-jnp.arange(0, half, 2, dtype=_f32) / half)
    ang_r = jnp.asarray(row)[:, None] * inv[None, :]
    ang_c = jnp.asarray(col)[:, None] * inv[None, :]
    cos = jnp.concatenate([jnp.cos(ang_r), jnp.cos(ang_r), jnp.cos(ang_c), jnp.cos(ang_c)], axis=-1)
    sin = jnp.concatenate([-jnp.sin(ang_r), jnp.sin(ang_r), -jnp.sin(ang_c), jnp.sin(ang_c)], axis=-1)
    return cos, sin


def _chunk_constants():
    c = CHUNK
    t = np.arange(c)
    tri, a_cat, pair, mq, mk, causal, strict = [], [], [], [], [], [], []
    for d in range(2):
        tau = t if d == 0 else c - 1 - t
        incl = (tau[None, :] <= tau[:, None]).astype(np.float32)
        tri.append(incl)
        causal.append(incl)
        strict.append((tau[None, :] < tau[:, None]).astype(np.float32))
        rows, pm, qm, km = [], [], [], []
        for li in range(N_LEVELS):
            s = c >> (li + 1)
            ref_tau = 2 * s * (tau // (2 * s)) + s - 1
            sel = (tau[None, :] == ref_tau[:, None]).astype(np.float32)
            rows.append(incl - sel @ incl)
            odd = ((tau // s) % 2 == 1).astype(np.float32)
            pm.append(((tau[:, None] // (2 * s)) == (tau[None, :] // (2 * s))).astype(np.float32))
            qm.append(np.repeat(odd[:, None], GLA_DK, axis=1))
            km.append(np.repeat((1.0 - odd)[:, None], GLA_DK, axis=1))
        pm.append(np.eye(c, dtype=np.float32))
        rows.append(incl)
        rows.append((tau[None, :] > tau[:, None]).astype(np.float32))
        a = np.concatenate(rows, axis=0)
        a_cat.append(np.concatenate([a, a, a], axis=1))
        pair.append(np.stack(pm))
        mq.append(np.stack(qm))
        mk.append(np.stack(km))
    tri_cat = np.stack([np.concatenate([x, x, x], axis=1) for x in tri])
    sub = np.stack([pair[d][:N_LEVELS] * mq[d][:, :, :1] * np.swapaxes(mk[d][:, :, :1], 1, 2) for d in range(2)])
    return dict(sub=jnp.asarray(sub), sub_b=jnp.asarray(sub, _bf16),
                a_cat=jnp.asarray(np.stack(a_cat), _bf16), pair=jnp.asarray(np.stack(pair)),
                mq=jnp.asarray(np.stack(mq)), mk=jnp.asarray(np.stack(mk)),
                tri_cat=jnp.asarray(tri_cat, _bf16),
                causal=jnp.asarray(np.stack([np.stack([causal[d], strict[d]]) for d in range(2)])))


def _chunk_pos(d, c, n):
    return c if d == 0 else n - 1 - c


def _gla_kernel(q_ref, k_ref, v_ref, r_ref, zs_ref, wg_ref, gb_ref, acat_ref, pair_ref, mq_ref, mk_ref,
                s0_ref, nw_ref, _prev_ref, o_ref, sn_ref, la_scr, oacc_scr, s_scr, *, t):
    c, dk, dv, nh = CHUNK, GLA_DK, GLA_DV, GLA_HEADS
    n = t // c
    zs = _bf(zs_ref[...])
    for d in range(2):
        x = _dot(zs, wg_ref[d]) + gb_ref[d]
        la_scr[d] = (jnp.minimum(x, 0.0) - jnp.log(1.0 + jnp.exp(-jnp.abs(x)))) * (1.0 / GLA_TAU)
        for h in range(nh):
            s_scr[d * nh + h] = s0_ref[0, d, h]
    ones = jnp.ones((3 * c, LANES), _bf16)
    chains = [(d, h) for d in range(2) for h in range(nh)]

    def chunk(ci, carry):
        rows = [pl.ds(pl.multiple_of(_chunk_pos(d, ci, n) * c, c), c) for d in range(2)]
        states = [s_scr[d * nh + h] for d, h in chains]
        dall, blast = [], []
        for d in range(2):
            gcat = jnp.concatenate(_split3(la_scr[d, rows[d], :]), axis=0)
            dall.append(_dot(acat_ref[d], gcat))
            blast.append(_dot_tn(gcat, ones))
        qs, ks, vs = [], [], []
        for d, h in chains:
            kc = slice(h * dk, (h + 1) * dk)
            qs.append(q_ref[rows[d], kc] * (dk ** -0.5))
            ks.append(k_ref[rows[d], kc])
            vs.append(_bf(v_ref[rows[d], h * dv:(h + 1) * dv]))
        atts = [_dot_nt(_bf(q), _bf(k)) * pair_ref[d, N_LEVELS] for (d, h), q, k in zip(chains, qs, ks)]
        for li in range(N_LEVELS):
            for j, (d, h) in enumerate(chains):
                f = jnp.exp(-jnp.abs(dall[d][li * c:(li + 1) * c, h * dk:(h + 1) * dk]))
                atts[j] += _dot_nt(_bf(qs[j] * f * mq_ref[d, li]), _bf(ks[j] * f * mk_ref[d, li])) * pair_ref[d, li]
        outs, new_states = [], []
        for j, (d, h) in enumerate(chains):
            kc = slice(h * dk, (h + 1) * dk)
            eb = jnp.exp(dall[d][N_LEVELS * c:(N_LEVELS + 1) * c, kc])
            el = jnp.exp(dall[d][(N_LEVELS + 1) * c:(N_LEVELS + 2) * c, kc])
            outs.append(_dot(_bf(qs[j] * eb), _bf(states[j])) + _dot(_bf(atts[j]), vs[j]))
            new_states.append(states[j] * jnp.exp(blast[d][kc, :]) + _dot_tn(_bf(ks[j] * el), vs[j]))
        for j, (d, h) in enumerate(chains):
            s_scr[d * nh + h] = new_states[j]
            oacc_scr[d, rows[d], h * dv:(h + 1) * dv] = outs[j]
        return carry

    lax.fori_loop(0, n, chunk, 0)

    for d, h in chains:
        sn_ref[0, d, h] = s_scr[d * nh + h]
    for h in range(nh):
        vc = slice(h * dv, (h + 1) * dv)
        o_ref[:, vc] = _bf(_rms(oacc_scr[0, :, vc] + oacc_scr[1, :, vc], nw_ref[...]) * _silu(r_ref[:, vc]))


def gla(z, row_blk0, nseq, t, wg, gbias, consts, s0, out_norm, prev=None):
    qw = GLA_HEADS * GLA_DK
    vw = GLA_HEADS * GLA_DV
    if prev is None:
        prev = jnp.zeros((z.shape[0], vw), _bf16)

    def zspec(name, width):
        blk = Z_OFF[name] // width
        assert Z_OFF[name] % width == 0
        return pl.BlockSpec((t, width), lambda b: (row_blk0 + b, blk))

    def full(a):
        return pl.BlockSpec(a.shape, lambda b, nd=a.ndim: (0,) * nd)

    state_spec = pl.BlockSpec((1, 2, GLA_HEADS, GLA_DK, GLA_DV), lambda b: (b, 0, 0, 0, 0))
    cs = [consts['a_cat'], consts['pair'], consts['mq'], consts['mk']]
    return pl.pallas_call(
        functools.partial(_gla_kernel, t=t),
        grid=(nseq,),
        in_specs=[zspec('gla_q', qw), zspec('gla_k', qw), zspec('gla_v', vw), zspec('gla_r', vw),
                  zspec('small', LANES), full(wg), full(gbias)] + [full(a) for a in cs]
                 + [state_spec, pl.BlockSpec((1, GLA_DV), lambda b: (0, 0)), pl.BlockSpec(memory_space=pl.ANY)],
        out_specs=[pl.BlockSpec((t, vw), lambda b: (row_blk0 + b, 0)), state_spec],
        out_shape=[jax.ShapeDtypeStruct(prev.shape, prev.dtype),
                   jax.ShapeDtypeStruct((nseq, 2, GLA_HEADS, GLA_DK, GLA_DV), _f32)],
        input_output_aliases={7 + len(cs) + 2: 0},
        scratch_shapes=[pltpu.VMEM((2, t, qw), _f32), pltpu.VMEM((2, t, vw), _f32),
                        pltpu.VMEM((2 * GLA_HEADS, GLA_DK, GLA_DV), _f32)],
        compiler_params=_params(("parallel",)),
        name="gla",
    )(z, z, z, z, z, wg, gbias, *cs, s0, out_norm.reshape(1, GLA_DV), prev)


CONV_PAD = 8
ROW_BLK = 128


def _gdn_kernel(q_ref, k_ref, v_ref, zz_ref, zs_ref, cw_ref, alog_ref, dtb_ref, tri_ref, causal_ref, sub_ref,
                subb_ref, s0_ref, nw_ref, _prev_ref, o_ref, sn_ref,
                xp_scr, qkv_scr, g_scr, beta_scr, b_scr, u_scr, w_scr, att_scr, oacc_scr, s_scr, *, t):
    c, hd, nh = CHUNK, HEAD_DIM, GDN_HEADS
    n = t // c
    w = nh * hd
    half = CONV_K // 2
    nblk = t // ROW_BLK
    eye = (lax.broadcasted_iota(jnp.int32, (c, c), 0) == lax.broadcasted_iota(jnp.int32, (c, c), 1)).astype(_f32)

    xp_scr[0:CONV_PAD, :] = jnp.zeros((CONV_PAD, w), _f32)
    xp_scr[CONV_PAD + t:2 * CONV_PAD + t, :] = jnp.zeros((CONV_PAD, w), _f32)
    for idx, src in enumerate((q_ref, k_ref, v_ref)):
        def copy_in(i, carry, src=src):
            r0 = pl.multiple_of(i * ROW_BLK, ROW_BLK)
            xp_scr[pl.ds(CONV_PAD + r0, ROW_BLK), :] = src[pl.ds(r0, ROW_BLK), :]
            return carry

        lax.fori_loop(0, nblk, copy_in, 0)

        def conv(i, carry, idx=idx):
            r0 = pl.multiple_of(i * ROW_BLK, ROW_BLK)
            win = xp_scr[pl.ds(r0, ROW_BLK + 2 * CONV_PAD), :]
            y = jnp.zeros((ROW_BLK, w), _f32)
            for j in range(CONV_K):
                lo = CONV_PAD + j - half
                y += win[lo:lo + ROW_BLK, :] * cw_ref[idx, j:j + 1, :]
            y = _silu(y)
            for h in range(nh):
                cols = slice(h * hd, (h + 1) * hd)
                yh = y[:, cols]
                if idx == 0:
                    yh = yh * lax.rsqrt(jnp.sum(yh * yh, axis=-1, keepdims=True) + EPS) * (hd ** -0.5)
                elif idx == 1:
                    yh = yh * lax.rsqrt(jnp.sum(yh * yh, axis=-1, keepdims=True) + EPS)
                qkv_scr[idx, pl.ds(r0, ROW_BLK), cols] = yh
            return carry

        lax.fori_loop(0, nblk, conv, 0)

    zs = zs_ref[...]
    g_scr[...] = -jnp.exp(alog_ref[...]) * _softplus(zs + dtb_ref[...])
    beta_scr[...] = _sigmoid(zs)
    for d in range(2):
        for h in range(nh):
            s_scr[d * nh + h] = s0_ref[0, d, h]

    chains = [(d, h) for d in range(2) for h in range(nh)]

    def local(ci, carry):
        rows = pl.ds(pl.multiple_of(ci * c, c), c)
        gcat = jnp.concatenate(_split3(g_scr[rows, :]), axis=0)
        beta_all = beta_scr[rows, :]
        b_all = [_dot(tri_ref[d], gcat) for d in range(2)]
        b_t = [b.T for b in b_all]
        qk, kk, ks, vs = [], [], [], []
        for h in range(nh):
            cols = slice(h * hd, (h + 1) * hd)
            k = qkv_scr[1, rows, cols]
            kbf = _bf(k)
            ks.append(k)
            vs.append(qkv_scr[2, rows, cols])
            kk.append(_dot_nt(kbf, kbf))
            qk.append(_dot_nt(_bf(qkv_scr[0, rows, cols]), kbf))
        lows, xs, rhss = [], [], []
        for d, h in chains:
            ia = SMALL_A + d * nh + h
            ib = SMALL_BETA + d * nh + h
            bcol = b_all[d][:, ia:ia + 1]
            decay = jnp.exp(jnp.minimum(bcol - b_t[d][ia:ia + 1, :], 0.0)) * causal_ref[d, 0]
            beta = beta_all[:, ib:ib + 1]
            att_scr[d * nh + h, rows, :] = _bf(qk[h] * decay)
            low = kk[h] * beta * decay * causal_ref[d, 1]
            lows.append(_split2(low))
            xs.append(eye - low * sub_ref[d, N_LEVELS - 1])
            rhss.append(jnp.concatenate([vs[h] * beta, ks[h] * (beta * jnp.exp(bcol))], axis=1))
        for li in range(N_LEVELS - 2, -1, -1):
            for j, (d, h) in enumerate(chains):
                mask = subb_ref[d, li]
                xp = _split2(xs[j])
                t1 = _dot_parts(xp, (lows[j][0] * mask, lows[j][1] * mask))
                xs[j] = xs[j] - _dot_parts(_split2(t1), xp)
        for j, (d, h) in enumerate(chains):
            cols = slice(h * hd, (h + 1) * hd)
            sol = _dot_parts(_split2(xs[j]), _split2(rhss[j]))
            u_scr[d, rows, cols] = sol[:, :hd]
            w_scr[d, rows, cols] = _bf(sol[:, hd:])
        for d in range(2):
            b_scr[d, rows, :] = b_all[d]
        return carry

    lax.fori_loop(0, n, local, 0)

    def step(ci, carry):
        rows = [pl.ds(pl.multiple_of(_chunk_pos(d, ci, n) * c, c), c) for d in range(2)]
        states = [s_scr[d * nh + h] for d, h in chains]
        b_all = [b_scr[d, rows[d], :] for d in range(2)]
        us, ws, atts, qs, ks = [], [], [], [], []
        for d, h in chains:
            cols = slice(h * hd, (h + 1) * hd)
            us.append(u_scr[d, rows[d], cols])
            ws.append(w_scr[d, rows[d], cols])
            atts.append(att_scr[d * nh + h, rows[d], :])
            qs.append(qkv_scr[0, rows[d], cols])
            ks.append(qkv_scr[1, rows[d], cols])
        sbs = [_bf(s) for s in states]
        v_new = [_bf(u - _dot(w_, sb)) for u, w_, sb in zip(us, ws, sbs)]
        outs, new_states = [], []
        for j, (d, h) in enumerate(chains):
            ia = SMALL_A + d * nh + h
            last = c - 1 if d == 0 else 0
            bcol = b_all[d][:, ia:ia + 1]
            b_last = b_all[d][last:last + 1, ia:ia + 1]
            outs.append(_dot(_bf(qs[j] * jnp.exp(bcol)), sbs[j]) + _dot(atts[j], v_new[j]))
            new_states.append(states[j] * jnp.exp(b_last) + _dot_tn(_bf(ks[j] * jnp.exp(b_last - bcol)), v_new[j]))
        for j, (d, h) in enumerate(chains):
            s_scr[d * nh + h] = new_states[j]
            oacc_scr[d, rows[d], h * hd:(h + 1) * hd] = outs[j]
        return carry

    lax.fori_loop(0, n, step, 0)

    for d, h in chains:
        sn_ref[0, d, h] = s_scr[d * nh + h]

    def epilogue(i, carry):
        r0 = pl.multiple_of(i * ROW_BLK, ROW_BLK)
        rr = pl.ds(r0, ROW_BLK)
        for h in range(nh):
            cols = slice(h * hd, (h + 1) * hd)
            o = oacc_scr[0, rr, cols] + oacc_scr[1, rr, cols]
            o_ref[rr, cols] = _bf(_rms(o, nw_ref[...]) * _silu(zz_ref[rr, cols]))
        return carry

    lax.fori_loop(0, nblk, epilogue, 0)


def gdn(z, row_blk0, nseq, t, conv_w, alog_lane, dtb_lane, consts, s0, out_norm, prev=None):
    w = GDN_HEADS * HEAD_DIM
    if prev is None:
        prev = jnp.zeros((z.shape[0], w), _bf16)
    qblk = Z_OFF['gdn_qkv'] // w
    assert Z_OFF['gdn_qkv'] % w == 0 and Z_OFF['gdn_z'] % w == 0 and Z_OFF['small'] % LANES == 0

    def zspec(blk, width):
        return pl.BlockSpec((t, width), lambda b: (row_blk0 + b, blk))

    def full(a):
        return pl.BlockSpec(a.shape, lambda b, nd=a.ndim: (0,) * nd)

    state_spec = pl.BlockSpec((1, 2, GDN_HEADS, GDN_DK, GDN_DV), lambda b: (b, 0, 0, 0, 0))
    return pl.pallas_call(
        functools.partial(_gdn_kernel, t=t),
        grid=(nseq,),
        in_specs=[zspec(qblk, w), zspec(qblk + 1, w), zspec(qblk + 2, w), zspec(Z_OFF['gdn_z'] // w, w),
                  zspec(Z_OFF['small'] // LANES, LANES), full(conv_w), full(alog_lane), full(dtb_lane),
                  full(consts['tri_cat']), full(consts['causal']), full(consts['sub']), full(consts['sub_b']),
                  state_spec, pl.BlockSpec((1, GDN_DV), lambda b: (0, 0)), pl.BlockSpec(memory_space=pl.ANY)],
        out_specs=[pl.BlockSpec((t, w), lambda b: (row_blk0 + b, 0)), state_spec],
        out_shape=[jax.ShapeDtypeStruct(prev.shape, prev.dtype),
                   jax.ShapeDtypeStruct((nseq, 2, GDN_HEADS, GDN_DK, GDN_DV), _f32)],
        input_output_aliases={14: 0},
        scratch_shapes=[pltpu.VMEM((t + 2 * CONV_PAD, w), _f32),
                        pltpu.VMEM((3, t, w), _f32),
                        pltpu.VMEM((t, LANES), _f32),
                        pltpu.VMEM((t, LANES), _f32),
                        pltpu.VMEM((2, t, LANES), _f32),
                        pltpu.VMEM((2, t, w), _f32),
                        pltpu.VMEM((2, t, w), _bf16),
                        pltpu.VMEM((2 * GDN_HEADS, t, CHUNK), _bf16),
                        pltpu.VMEM((2, t, w), _f32),
                        pltpu.VMEM((2 * GDN_HEADS, GDN_DK, GDN_DV), _f32)],
        compiler_params=_params(("parallel",)),
        name="gdn",
    )(z, z, z, z, z, conv_w, alog_lane, dtb_lane, consts['tri_cat'], consts['causal'], consts['sub'],
      consts['sub_b'], s0, out_norm.reshape(1, GDN_DV), prev)


def _permute_columns(w_in):
    parts, start = [], 0
    n = len(Z_PERM)
    while start < n:
        stop = start + 1
        if Z_PERM[start] < 0:
            while stop < n and Z_PERM[stop] < 0:
                stop += 1
            parts.append(jnp.zeros(w_in.shape[:2] + (stop - start,), _bf16))
        else:
            while stop < n and Z_PERM[stop] == Z_PERM[stop - 1] + 1:
                stop += 1
            parts.append(w_in[..., int(Z_PERM[start]):int(Z_PERM[stop - 1]) + 1].astype(_bf16))
        start = stop
    return jnp.concatenate(parts, axis=-1)


def _lane_vector(values, offset):
    k = values.shape[-1]
    return jnp.pad(values.astype(_f32), ((0, 0), (offset, LANES - offset - k)))[:, None, :]


def kernel(x_prompt, x_sample, cache_na_k, cache_na_v, cache_gqa_k, cache_gqa_v, state_gla, state_gdn, c, c_ctx, norm_g, w_mod, b_mod, ffn_gu, ffn_down, w_in, w_out, na_qk_norm, na_rpb, gla_gate_up, gla_gate_bias, gla_out_norm, gqa_qk_norm, gdn_conv, gdn_a_log, gdn_dt_bias, gdn_out_norm):
    assert GRID_ROWS % NA_QROWS == 0 and GRID_ROWS >= NA_KROWS and PAST_LEN == SEQ
    d = D_MODEL
    m_ctx = BATCH * SEQ
    assert m_ctx % DEC_SEQ == 0 and DEC_SEQ % TOKEN_TILE == 0
    x = jnp.concatenate([x_prompt.reshape(m_ctx, d), x_sample.reshape(DEC_BATCH * DEC_SEQ, d)], axis=0)

    n_cond = 1 + DEC_BATCH
    cond = jnp.concatenate([c_ctx[None, :], c], axis=0)
    cond = jnp.pad(cond, ((0, (-n_cond) % 8), (0, 0)))
    mod = adaln_all(cond, w_mod, b_mod).reshape(DEPTH, cond.shape[0], N_MOD, d)

    fpad = FFN_PAD - FFN_DIM
    w_gate = jnp.pad(ffn_gu[..., :FFN_DIM].astype(_bf16), ((0, 0),) * 3 + ((0, fpad),))
    w_up = jnp.pad(ffn_gu[..., FFN_DIM:].astype(_bf16), ((0, 0),) * 3 + ((0, fpad),))
    down = jnp.pad(ffn_down.astype(_bf16), ((0, 0), (0, 0), (0, fpad), (0, 0)))
    w_in_p = _permute_columns(w_in)
    w_out_b = w_out.astype(_bf16)

    consts = _chunk_constants()
    cos, sin = _rope_tables()
    bias = na_bias_tables(na_rpb)

    qw = GLA_HEADS * GLA_DK
    wg = jnp.zeros((DEPTH, 2, LANES, qw), _f32)
    wg = wg.at[:, 0, 0:GLA_GATE_RANK].set(gla_gate_up[:, 0])
    wg = wg.at[:, 1, SMALL_GB:SMALL_GB + GLA_GATE_RANK].set(gla_gate_up[:, 1]).astype(_bf16)
    gbias = gla_gate_bias.reshape(DEPTH, 2, 1, qw)
    conv_w = gdn_conv.reshape(DEPTH, CONV_K, 3, GDN_HEADS * HEAD_DIM).transpose(0, 2, 1, 3)
    conv_w = jnp.pad(conv_w, ((0, 0), (0, 0), (0, 8 - CONV_K), (0, 0)))
    alog_lane = _lane_vector(gdn_a_log.reshape(DEPTH, 2 * GDN_HEADS), SMALL_A)
    dtb_lane = _lane_vector(gdn_dt_bias.reshape(DEPTH, 2 * GDN_HEADS), SMALL_A)
    gla_zero = jnp.zeros((BATCH, 2, GLA_HEADS, GLA_DK, GLA_DV), _f32)
    gdn_zero = jnp.zeros((BATCH, 2, GDN_HEADS, GDN_DK, GDN_DV), _f32)
    lat_blk0 = m_ctx // DEC_SEQ

    na_k_l, na_v_l, gqa_k_l, gqa_v_l, gla_l, gdn_l = [], [], [], [], [], []
    for l in range(DEPTH):
        x = ffn(x, mod[l], norm_g[l, 0], w_gate[l, 0], w_up[l, 0], down[l, 0], 0)
        z = in_proj(x, mod[l], norm_g[l, 1], w_in_p[l])

        o_na, o_gqa, kn, vn, ka, va = ctx_attention(z, na_qk_norm[l], gqa_qk_norm[l])
        o_na = na_latent(z, cache_na_k, cache_na_v, bias, na_qk_norm[l], l, o_na)
        o_gqa = gqa_latent(z, cache_gqa_k, cache_gqa_v, cos, sin, gqa_qk_norm[l], l, o_gqa)
        o_gla, sg = gla(z, 0, BATCH, SEQ, wg[l], gbias[l], consts, gla_zero, gla_out_norm[l])
        o_gla, _ = gla(z, lat_blk0, DEC_BATCH, DEC_SEQ, wg[l], gbias[l], consts, state_gla[:, l],
                       gla_out_norm[l], o_gla)
        o_gdn, sd = gdn(z, 0, BATCH, SEQ, conv_w[l], alog_lane[l], dtb_lane[l], consts, gdn_zero,
                        gdn_out_norm[l])
        o_gdn, _ = gdn(z, lat_blk0, DEC_BATCH, DEC_SEQ, conv_w[l], alog_lane[l], dtb_lane[l], consts,
                       state_gdn[:, l], gdn_out_norm[l], o_gdn)

        x = out_proj(x, mod[l], o_na, o_gla, o_gqa, o_gdn, w_out_b[l])
        x = ffn(x, mod[l], norm_g[l, 2], w_gate[l, 1], w_up[l, 1], down[l, 1], 6)

        na_k_l.append(kn)
        na_v_l.append(vn)
        gqa_k_l.append(ka)
        gqa_v_l.append(va)
        gla_l.append(sg)
        gdn_l.append(sd)

    y_prompt = x[:m_ctx].reshape(BATCH, SEQ, d)
    y_sample = x[m_ctx:].reshape(DEC_BATCH, DEC_SEQ, d)
    return (y_prompt, y_sample, jnp.stack(na_k_l, axis=1), jnp.stack(na_v_l, axis=1),
            jnp.stack(gqa_k_l, axis=1), jnp.stack(gqa_v_l, axis=1),
            jnp.stack(gla_l, axis=1), jnp.stack(gdn_l, axis=1))
```

```python
import functools
import math

import jax
import jax.numpy as jnp
import numpy as np
from jax import lax
from jax.experimental import pallas as pl
from jax.experimental.pallas import tpu as pltpu

D_MODEL = 2048
BATCH = 16
SEQ = 256
DEPTH = 4
DEC_BATCH = 8
DEC_SEQ = 1024
PAST_LEN = 256

GRID_W = 64
HEAD_DIM = 128
GROUP_WIDTH = D_MODEL // 4
NA_HEADS = GROUP_WIDTH // HEAD_DIM
NA_WIN_ROWS = 8
NA_WIN_COLS = 16
GLA_HEADS = 4
GLA_DV = GROUP_WIDTH // GLA_HEADS
GLA_DK = GLA_DV // 2
GLA_GATE_RANK = 16
GLA_TAU = 16.0
GQA_HEADS = GROUP_WIDTH // HEAD_DIM
GQA_KV_HEADS = GQA_HEADS // 2
GDN_HEADS = GROUP_WIDTH // HEAD_DIM
GDN_DK = HEAD_DIM
GDN_DV = HEAD_DIM
CONV_K = 5
CHUNK = 64
ROPE_THETA = 10000.0
FFN_DIM = ((8 * D_MODEL // 3 + 127) // 128) * 128
N_MOD = 9
EPS = 1e-6
NEG_INF = -1e30

IN_SPLITS = (
    ('na_q', NA_HEADS * HEAD_DIM), ('na_k', NA_HEADS * HEAD_DIM), ('na_v', NA_HEADS * HEAD_DIM),
    ('gla_q', GLA_HEADS * GLA_DK), ('gla_k', GLA_HEADS * GLA_DK), ('gla_v', GLA_HEADS * GLA_DV),
    ('gla_r', GLA_HEADS * GLA_DV), ('gla_gf', GLA_GATE_RANK), ('gla_gb', GLA_GATE_RANK),
    ('gqa_q', GQA_HEADS * HEAD_DIM), ('gqa_k', GQA_KV_HEADS * HEAD_DIM), ('gqa_v', GQA_KV_HEADS * HEAD_DIM),
    ('gdn_qkv', 3 * GDN_HEADS * HEAD_DIM), ('gdn_z', GDN_HEADS * GDN_DV),
    ('gdn_b', 2 * GDN_HEADS), ('gdn_a', 2 * GDN_HEADS),
)

LANES = 128
FFN_TILE = 512
FFN_PAD = ((FFN_DIM + FFN_TILE - 1) // FFN_TILE) * FFN_TILE
TOKEN_TILE = 512
VMEM_LIMIT = 56 * 1024 * 1024

Z_ORDER = ('na_q', 'na_k', 'na_v', 'gla_q', 'gla_k', 'gla_v', 'gla_r', 'gqa_q', 'gqa_k', 'gqa_v',
           'gdn_qkv', 'gdn_z')
SMALL_ORDER = ('gla_gf', 'gla_gb', 'gdn_b', 'gdn_a')
SMALL_GB = GLA_GATE_RANK
SMALL_BETA = 2 * GLA_GATE_RANK
SMALL_A = SMALL_BETA + 2 * GDN_HEADS
N_LEVELS = int(math.log2(CHUNK))


def _layout():
    src, off = {}, 0
    for name, width in IN_SPLITS:
        src[name] = (off, width)
        off += width
    perm, zoff, pos = [], {}, 0
    for name in Z_ORDER:
        o, w = src[name]
        zoff[name] = pos
        perm.extend(range(o, o + w))
        pos += w
    zoff['small'] = pos
    n_small = 0
    for name in SMALL_ORDER:
        o, w = src[name]
        perm.extend(range(o, o + w))
        n_small += w
    perm.extend([-1] * (LANES - n_small))
    pos += LANES
    perm.extend([-1] * ((-pos) % Z_TILE))
    return np.asarray(perm, np.int32), zoff, len(perm)


Z_TILE = 5 * 256
Z_PERM, Z_OFF, Z_COLS = _layout()

_f32 = jnp.float32
_bf16 = jnp.bfloat16


def _dot(a, b):
    return jnp.dot(a, b, preferred_element_type=_f32)


def _dot_nt(a, b):
    return lax.dot_general(a, b, (((1,), (1,)), ((), ())), preferred_element_type=_f32)


def _dot_tn(a, b):
    return lax.dot_general(a, b, (((0,), (0,)), ((), ())), preferred_element_type=_f32)


def _bf(x):
    return x.astype(_bf16)


def _sigmoid(x):
    return 1.0 / (1.0 + jnp.exp(-x))


def _silu(x):
    return x * _sigmoid(x)


def _softplus(x):
    return jnp.maximum(x, 0.0) + jnp.log(1.0 + jnp.exp(-jnp.abs(x)))


def _rms(x, w):
    return x * lax.rsqrt(jnp.mean(x * x, axis=-1, keepdims=True) + EPS) * w


def _split3(x):
    hi = _bf(x)
    r1 = x - hi.astype(_f32)
    mid = _bf(r1)
    lo = _bf(r1 - mid.astype(_f32))
    return hi, mid, lo


def _split2(x):
    hi = _bf(x)
    return hi, _bf(x - hi.astype(_f32))


def _dot_parts(a, b):
    return _dot(jnp.concatenate([a[0], a[0], a[1]], axis=1), jnp.concatenate([b[0], b[1], b[0]], axis=0))


def _cond_row(i, tile):
    n_ctx = (BATCH * SEQ) // tile
    return jnp.where(i < n_ctx, 0, 1 + (i - n_ctx) // (DEC_SEQ // tile))


def _params(sem):
    return pltpu.CompilerParams(dimension_semantics=sem, vmem_limit_bytes=VMEM_LIMIT)


def _adaln_kernel(c_ref, w_ref, b_ref, o_ref):
    a = _bf(_silu(c_ref[...]))
    o_ref[0] = _dot(a, _bf(w_ref[0])) + b_ref[0]


def adaln_all(cond, w_mod, b_mod):
    nc = cond.shape[0]
    tn = 1024
    n = N_MOD * D_MODEL
    return pl.pallas_call(
        _adaln_kernel,
        grid=(DEPTH, n // tn),
        in_specs=[pl.BlockSpec((nc, D_MODEL), lambda l, j: (0, 0)),
                  pl.BlockSpec((1, D_MODEL, tn), lambda l, j: (l, 0, j)),
                  pl.BlockSpec((1, 1, tn), lambda l, j: (l, 0, j))],
        out_specs=pl.BlockSpec((1, nc, tn), lambda l, j: (l, 0, j)),
        out_shape=jax.ShapeDtypeStruct((DEPTH, nc, n), _f32),
        compiler_params=_params(("parallel", "parallel")),
        name="adaln",
    )(cond, w_mod, b_mod.reshape(DEPTH, 1, n))


CAST_BLK = 256
CAST_CHUNK = 1024


def _cast_gu_kernel(w_ref, g_ref, u_ref):
    f = FFN_DIM
    for c0 in range(0, f, CAST_CHUNK):
        c1 = min(c0 + CAST_CHUNK, f)
        g_ref[0, :, c0:c1] = _bf(w_ref[0, :, c0:c1])
        u_ref[0, :, c0:c1] = _bf(w_ref[0, :, f + c0:f + c1])
    if FFN_PAD > f:
        g_ref[0, :, f:] = jnp.zeros((CAST_BLK, FFN_PAD - f), _bf16)
        u_ref[0, :, f:] = jnp.zeros((CAST_BLK, FFN_PAD - f), _bf16)


def _cast_down_kernel(w_ref, o_ref):
    f = FFN_DIM
    for r0 in range(0, f, CAST_CHUNK):
        r1 = min(r0 + CAST_CHUNK, f)
        o_ref[0, r0:r1, :] = _bf(w_ref[0, r0:r1, :])
    if FFN_PAD > f:
        o_ref[0, f:, :] = jnp.zeros((FFN_PAD - f, CAST_BLK), _bf16)


def cast_ffn_weights(ffn_gu, ffn_down):
    d, f = D_MODEL, FFN_DIM
    n = ffn_gu.shape[0] * ffn_gu.shape[1]
    gate, up = pl.pallas_call(
        _cast_gu_kernel,
        grid=(n, d // CAST_BLK),
        in_specs=[pl.BlockSpec((1, CAST_BLK, 2 * f), lambda i, r: (i, r, 0))],
        out_specs=[pl.BlockSpec((1, CAST_BLK, FFN_PAD), lambda i, r: (i, r, 0))] * 2,
        out_shape=[jax.ShapeDtypeStruct((n, d, FFN_PAD), _bf16)] * 2,
        compiler_params=_params(("parallel", "parallel")),
        name="cast_gate_up",
    )(ffn_gu.reshape(n, d, 2 * f))
    down = pl.pallas_call(
        _cast_down_kernel,
        grid=(n, d // CAST_BLK),
        in_specs=[pl.BlockSpec((1, f, CAST_BLK), lambda i, r: (i, 0, r))],
        out_specs=pl.BlockSpec((1, FFN_PAD, CAST_BLK), lambda i, r: (i, 0, r)),
        out_shape=jax.ShapeDtypeStruct((n, FFN_PAD, d), _bf16),
        compiler_params=_params(("parallel", "parallel")),
        name="cast_down",
    )(ffn_down.reshape(n, f, d))
    return gate, up, down


def _modulated_norm(x, g, m_ref, base):
    sh = m_ref[0, base:base + 1, :]
    sc = m_ref[0, base + 1:base + 2, :]
    return _rms(x, g) * (1.0 + sc) + sh


def _ffn_kernel(x_ref, m_ref, g_ref, wg_ref, wu_ref, wd_ref, o_ref, h_scr, acc_scr, *, base):
    f = pl.program_id(1)

    @pl.when(f == 0)
    def _():
        h_scr[...] = _bf(_modulated_norm(x_ref[...], g_ref[...], m_ref, base))
        acc_scr[...] = jnp.zeros_like(acc_scr)

    h = h_scr[...]
    gate = _dot(h, wg_ref[...])
    up = _dot(h, wu_ref[...])
    acc_scr[...] += _dot(_bf(_silu(gate) * up), wd_ref[...])

    @pl.when(f == pl.num_programs(1) - 1)
    def _():
        o_ref[...] = x_ref[...] + (0.5 * m_ref[0, base + 2:base + 3, :]) * acc_scr[...]


def ffn(x, mod, norm_g, w_gate, w_up, w_down, widx, base):
    m, d = x.shape
    nf = FFN_PAD // FFN_TILE
    tm = TOKEN_TILE
    return pl.pallas_call(
        functools.partial(_ffn_kernel, base=base),
        grid=(m // tm, nf),
        in_specs=[pl.BlockSpec((tm, d), lambda i, f: (i, 0)),
                  pl.BlockSpec((1, N_MOD, d), lambda i, f: (_cond_row(i, tm), 0, 0)),
                  pl.BlockSpec((1, d), lambda i, f: (0, 0)),
                  pl.BlockSpec((None, d, FFN_TILE), lambda i, f: (widx, 0, f)),
                  pl.BlockSpec((None, d, FFN_TILE), lambda i, f: (widx, 0, f)),
                  pl.BlockSpec((None, FFN_TILE, d), lambda i, f: (widx, f, 0))],
        out_specs=pl.BlockSpec((tm, d), lambda i, f: (i, 0)),
        out_shape=jax.ShapeDtypeStruct((m, d), _f32),
        scratch_shapes=[pltpu.VMEM((tm, d), _bf16), pltpu.VMEM((tm, d), _f32)],
        compiler_params=_params(("parallel", "arbitrary")),
        name="ffn",
    )(x, mod, norm_g.reshape(1, d), w_gate, w_up, w_down)


def _inproj_kernel(x_ref, m_ref, g_ref, w_ref, o_ref, h_scr):
    @pl.when(pl.program_id(1) == 0)
    def _():
        h_scr[...] = _bf(_modulated_norm(x_ref[...], g_ref[...], m_ref, 3))

    o_ref[...] = _dot(h_scr[...], w_ref[...])


def in_proj(x, mod, norm_g, w_in, layer):
    m, d = x.shape
    tm = TOKEN_TILE
    return pl.pallas_call(
        _inproj_kernel,
        grid=(m // tm, Z_COLS // Z_TILE),
        in_specs=[pl.BlockSpec((tm, d), lambda i, j: (i, 0)),
                  pl.BlockSpec((1, N_MOD, d), lambda i, j: (_cond_row(i, tm), 0, 0)),
                  pl.BlockSpec((1, d), lambda i, j: (0, 0)),
                  pl.BlockSpec((None, d, Z_TILE), lambda i, j: (layer, 0, j))],
        out_specs=pl.BlockSpec((tm, Z_TILE), lambda i, j: (i, j)),
        out_shape=jax.ShapeDtypeStruct((m, Z_COLS), _f32),
        scratch_shapes=[pltpu.VMEM((tm, d), _bf16)],
        compiler_params=_params(("parallel", "arbitrary")),
        name="in_proj",
    )(x, mod, norm_g.reshape(1, d), w_in)


def _outproj_kernel(x_ref, m_ref, a_ref, b_ref, c_ref, d_ref, w_ref, o_ref):
    gw = GROUP_WIDTH
    acc = _dot(a_ref[...], w_ref[0:gw, :])
    acc += _dot(b_ref[...], w_ref[gw:2 * gw, :])
    acc += _dot(c_ref[...], w_ref[2 * gw:3 * gw, :])
    acc += _dot(d_ref[...], w_ref[3 * gw:4 * gw, :])
    o_ref[...] = x_ref[...] + m_ref[0, 5:6, :] * acc


def out_proj(x, mod, o_na, o_gla, o_gqa, o_gdn, w_out, layer):
    m, d = x.shape
    tm = TOKEN_TILE
    grp = pl.BlockSpec((tm, GROUP_WIDTH), lambda i: (i, 0))
    return pl.pallas_call(
        _outproj_kernel,
        grid=(m // tm,),
        in_specs=[pl.BlockSpec((tm, d), lambda i: (i, 0)),
                  pl.BlockSpec((1, N_MOD, d), lambda i: (_cond_row(i, tm), 0, 0)),
                  grp, grp, grp, grp,
                  pl.BlockSpec((None, 4 * GROUP_WIDTH, d), lambda i: (layer, 0, 0))],
        out_specs=pl.BlockSpec((tm, d), lambda i: (i, 0)),
        out_shape=jax.ShapeDtypeStruct((m, d), _f32),
        compiler_params=_params(("parallel",)),
        name="out_proj",
    )(x, mod, o_na, o_gla, o_gqa, o_gdn, w_out)


def _softmax_pv(scores, values):
    mx = scores[0].max(axis=-1, keepdims=True)
    for s in scores[1:]:
        mx = jnp.maximum(mx, s.max(axis=-1, keepdims=True))
    num, den = None, None
    for s, v in zip(scores, values):
        p = jnp.exp(s - mx)
        d_ = p.sum(axis=-1, keepdims=True)
        n_ = _dot(_bf(p), v)
        num = n_ if num is None else num + n_
        den = d_ if den is None else den + d_
    return num / den


def _rope(x, cos, sin):
    lane = lax.broadcasted_iota(jnp.int32, x.shape, 1)
    quarter = HEAD_DIM // 4
    partner = jnp.where((lane % (2 * quarter)) < quarter,
                        pltpu.roll(x, HEAD_DIM - quarter, 1), pltpu.roll(x, quarter, 1))
    return x * cos + partner * sin


def _ctx_attn_kernel(nq_ref, nk_ref, nv_ref, gq_ref, gk_ref, gv_ref, nw_ref, gw_ref,
                     ona_ref, ogqa_ref, kn_ref, vn_ref, ka_ref, va_ref):
    hd = HEAD_DIM
    scale = hd ** -0.5
    for h in range(NA_HEADS):
        cols = slice(h * hd, (h + 1) * hd)
        k = _rms(nk_ref[:, cols], nw_ref[1:2, :])
        v = nv_ref[:, cols]
        q = _rms(nq_ref[:, cols], nw_ref[0:1, :])
        kn_ref[0, h] = k
        vn_ref[0, h] = v
        s = _dot_nt(_bf(q), _bf(k)) * scale
        ona_ref[:, cols] = _bf(_softmax_pv([s], [_bf(v)]))
    group = GQA_HEADS // GQA_KV_HEADS
    for kv in range(GQA_KV_HEADS):
        cols = slice(kv * hd, (kv + 1) * hd)
        k = _rms(gk_ref[:, cols], gw_ref[1:2, :])
        v = gv_ref[:, cols]
        ka_ref[0, kv] = k
        va_ref[0, kv] = v
        for g in range(group):
            qcols = slice((kv * group + g) * hd, (kv * group + g + 1) * hd)
            q = _rms(gq_ref[:, qcols], gw_ref[0:1, :])
            s = _dot_nt(_bf(q), _bf(k)) * scale
            ogqa_ref[:, qcols] = _bf(_softmax_pv([s], [_bf(v)]))


def ctx_attention(z, na_w, gqa_w):
    t, hd = SEQ, HEAD_DIM

    def zspec(name, width):
        blk = Z_OFF[name] // width
        assert Z_OFF[name] % width == 0
        return pl.BlockSpec((t, width), lambda b: (b, blk))

    gw = GROUP_WIDTH
    kvw = GQA_KV_HEADS * hd
    outs = pl.pallas_call(
        _ctx_attn_kernel,
        grid=(BATCH,),
        in_specs=[zspec('na_q', gw), zspec('na_k', gw), zspec('na_v', gw),
                  zspec('gqa_q', gw), zspec('gqa_k', kvw), zspec('gqa_v', kvw),
                  pl.BlockSpec((2, hd), lambda b: (0, 0)), pl.BlockSpec((2, hd), lambda b: (0, 0))],
        out_specs=[pl.BlockSpec((t, gw), lambda b: (b, 0)), pl.BlockSpec((t, gw), lambda b: (b, 0)),
                   pl.BlockSpec((1, NA_HEADS, t, hd), lambda b: (b, 0, 0, 0)),
                   pl.BlockSpec((1, NA_HEADS, t, hd), lambda b: (b, 0, 0, 0)),
                   pl.BlockSpec((1, GQA_KV_HEADS, t, hd), lambda b: (b, 0, 0, 0)),
                   pl.BlockSpec((1, GQA_KV_HEADS, t, hd), lambda b: (b, 0, 0, 0))],
        out_shape=[jax.ShapeDtypeStruct((z.shape[0], gw), _bf16), jax.ShapeDtypeStruct((z.shape[0], gw), _bf16),
                   jax.ShapeDtypeStruct((BATCH, NA_HEADS, t, hd), _f32),
                   jax.ShapeDtypeStruct((BATCH, NA_HEADS, t, hd), _f32),
                   jax.ShapeDtypeStruct((BATCH, GQA_KV_HEADS, t, hd), _f32),
                   jax.ShapeDtypeStruct((BATCH, GQA_KV_HEADS, t, hd), _f32)],
        compiler_params=_params(("parallel",)),
        name="ctx_attention",
    )(z, z, z, z, z, z, na_w, gqa_w)
    return outs


NA_QROWS = 4
NA_KROWS = NA_QROWS + NA_WIN_ROWS
GRID_ROWS = DEC_SEQ // GRID_W


def _na_key_row0(blk):
    return min(max(blk * NA_QROWS - NA_WIN_ROWS // 2, 0), GRID_ROWS - NA_KROWS)


def _na_bias_kernel(rpb_ref, o_ref):
    l, h = pl.program_id(0), pl.program_id(1)
    n_dr, n_dc = 2 * NA_WIN_ROWS - 1, 2 * NA_WIN_COLS - 1
    base = (l * NA_HEADS + h) * (n_dr * n_dc)
    qc = lax.broadcasted_iota(jnp.int32, (GRID_W, GRID_W), 0)
    kc = lax.broadcasted_iota(jnp.int32, (GRID_W, GRID_W), 1)
    dc = jnp.clip(kc - qc + (NA_WIN_COLS - 1), 0, n_dc - 1)
    c0 = jnp.clip(qc - NA_WIN_COLS // 2, 0, GRID_W - NA_WIN_COLS)
    in_win = (kc >= c0) & (kc < c0 + NA_WIN_COLS)
    masked = jnp.full((GRID_W, GRID_W), NEG_INF, _f32)
    tiles = []
    for dr in range(n_dr):
        t = jnp.zeros((GRID_W, GRID_W), _f32)
        for j in range(n_dc):
            t = jnp.where(dc == j, rpb_ref[base + dr * n_dc + j], t)
        tiles.append(jnp.where(in_win, t, NEG_INF))
    for blk in range(GRID_ROWS // NA_QROWS):
        k0 = _na_key_row0(blk)
        for qi in range(NA_QROWS):
            r = blk * NA_QROWS + qi
            krow0 = min(max(r - NA_WIN_ROWS // 2, 0), GRID_ROWS - NA_WIN_ROWS)
            for kj in range(NA_KROWS):
                kr = k0 + kj
                ok = krow0 <= kr < krow0 + NA_WIN_ROWS
                tile = tiles[kr - r + NA_WIN_ROWS - 1] if ok else masked
                o_ref[0, 0, blk, qi * GRID_W:(qi + 1) * GRID_W, kj * GRID_W:(kj + 1) * GRID_W] = tile


def na_bias_tables(na_rpb):
    nblk = GRID_ROWS // NA_QROWS
    shape = (DEPTH, NA_HEADS, nblk, NA_QROWS * GRID_W, NA_KROWS * GRID_W)
    return pl.pallas_call(
        _na_bias_kernel,
        grid=(DEPTH, NA_HEADS),
        in_specs=[pl.BlockSpec(memory_space=pltpu.SMEM)],
        out_specs=pl.BlockSpec((1, 1) + shape[2:], lambda l, h: (l, h, 0, 0, 0)),
        out_shape=jax.ShapeDtypeStruct(shape, _f32),
        compiler_params=_params(("parallel", "parallel")),
        name="na_bias",
    )(na_rpb.reshape(-1))


def _na_lat_kernel(q_ref, k_ref, v_ref, ck_ref, cv_ref, bias_ref, nw_ref, _prev_ref, o_ref, kn_scr, vb_scr):
    blk = pl.program_id(2)
    scale = HEAD_DIM ** -0.5

    @pl.when(blk == 0)
    def _():
        kn_scr[...] = _bf(_rms(k_ref[...], nw_ref[1:2, :]))
        vb_scr[...] = _bf(v_ref[...])

    nblk = GRID_ROWS // NA_QROWS
    row0 = jnp.int32(_na_key_row0(0))
    for b_ in range(1, nblk):
        row0 = jnp.where(blk == b_, _na_key_row0(b_), row0)
    start = pl.multiple_of(row0 * GRID_W, GRID_W)
    nkeys = NA_KROWS * GRID_W
    q = _bf(_rms(q_ref[...], nw_ref[0:1, :]))
    s_ctx = _dot_nt(q, _bf(ck_ref[0, 0, 0])) * scale
    s_loc = _dot_nt(q, kn_scr[pl.ds(start, nkeys), :]) * scale + bias_ref[0, 0, 0]
    o = _softmax_pv([s_ctx, s_loc], [_bf(cv_ref[0, 0, 0]), vb_scr[pl.ds(start, nkeys), :]])
    o_ref[...] = _bf(o)


def na_latent(z, cache_k, cache_v, bias, na_w, layer, prev):
    t, hd = DEC_SEQ, HEAD_DIM
    tq = NA_QROWS * GRID_W
    nblk = t // tq
    ctx_tiles = (BATCH * SEQ) // tq
    ctx_seqs = (BATCH * SEQ) // t
    qb, kb, vb = Z_OFF['na_q'] // hd, Z_OFF['na_k'] // hd, Z_OFF['na_v'] // hd
    cache_spec = pl.BlockSpec((1, 1, 1, PAST_LEN, hd), lambda b, h, r: (b, layer, h, 0, 0))
    return pl.pallas_call(
        _na_lat_kernel,
        grid=(DEC_BATCH, NA_HEADS, nblk),
        in_specs=[pl.BlockSpec((tq, hd), lambda b, h, r: (ctx_tiles + b * nblk + r, qb + h)),
                  pl.BlockSpec((t, hd), lambda b, h, r: (ctx_seqs + b, kb + h)),
                  pl.BlockSpec((t, hd), lambda b, h, r: (ctx_seqs + b, vb + h)),
                  cache_spec, cache_spec,
                  pl.BlockSpec((1, 1, 1, tq, NA_KROWS * GRID_W), lambda b, h, r: (layer, h, r, 0, 0)),
                  pl.BlockSpec((2, hd), lambda b, h, r: (0, 0)),
                  pl.BlockSpec(memory_space=pl.ANY)],
        out_specs=pl.BlockSpec((tq, hd), lambda b, h, r: (ctx_tiles + b * nblk + r, h)),
        out_shape=jax.ShapeDtypeStruct(prev.shape, prev.dtype),
        input_output_aliases={7: 0},
        scratch_shapes=[pltpu.VMEM((t, hd), _bf16), pltpu.VMEM((t, hd), _bf16)],
        compiler_params=_params(("parallel", "parallel", "arbitrary")),
        name="na_latent",
    )(z, z, z, cache_k, cache_v, bias, na_w, prev)


GQA_TQ = 256


def _gqa_lat_kernel(q_ref, k_ref, v_ref, ck_ref, cv_ref, cq_ref, sq_ref, ck_all_ref, sk_all_ref, gw_ref,
                    _prev_ref, o_ref, kr_scr, vb_scr):
    hd = HEAD_DIM
    scale = hd ** -0.5

    @pl.when(pl.program_id(2) == 0)
    def _():
        kr_scr[...] = _bf(_rope(_rms(k_ref[...], gw_ref[1:2, :]), ck_all_ref[...], sk_all_ref[...]))
        vb_scr[...] = _bf(v_ref[...])

    ck = _bf(ck_ref[0, 0, 0])
    cv = _bf(cv_ref[0, 0, 0])
    for g in range(GQA_HEADS // GQA_KV_HEADS):
        cols = slice(g * hd, (g + 1) * hd)
        q = _bf(_rope(_rms(q_ref[:, cols], gw_ref[0:1, :]), cq_ref[...], sq_ref[...]))
        s_ctx = _dot_nt(q, ck) * scale
        s_loc = _dot_nt(q, kr_scr[...]) * scale
        o_ref[:, cols] = _bf(_softmax_pv([s_ctx, s_loc], [cv, vb_scr[...]]))


def gqa_latent(z, cache_k, cache_v, cos, sin, gqa_w, layer, prev):
    t, hd = DEC_SEQ, HEAD_DIM
    tq = GQA_TQ
    nblk = t // tq
    group = GQA_HEADS // GQA_KV_HEADS
    ctx_tiles = (BATCH * SEQ) // tq
    ctx_seqs = (BATCH * SEQ) // t
    qb = Z_OFF['gqa_q'] // (group * hd)
    kb, vb = Z_OFF['gqa_k'] // hd, Z_OFF['gqa_v'] // hd
    cache_spec = pl.BlockSpec((1, 1, 1, PAST_LEN, hd), lambda b, h, r: (b, layer, h, 0, 0))
    return pl.pallas_call(
        _gqa_lat_kernel,
        grid=(DEC_BATCH, GQA_KV_HEADS, nblk),
        in_specs=[pl.BlockSpec((tq, group * hd), lambda b, h, r: (ctx_tiles + b * nblk + r, qb + h)),
                  pl.BlockSpec((t, hd), lambda b, h, r: (ctx_seqs + b, kb + h)),
                  pl.BlockSpec((t, hd), lambda b, h, r: (ctx_seqs + b, vb + h)),
                  cache_spec, cache_spec,
                  pl.BlockSpec((tq, hd), lambda b, h, r: (r, 0)),
                  pl.BlockSpec((tq, hd), lambda b, h, r: (r, 0)),
                  pl.BlockSpec((t, hd), lambda b, h, r: (0, 0)),
                  pl.BlockSpec((t, hd), lambda b, h, r: (0, 0)),
                  pl.BlockSpec((2, hd), lambda b, h, r: (0, 0)),
                  pl.BlockSpec(memory_space=pl.ANY)],
        out_specs=pl.BlockSpec((tq, group * hd), lambda b, h, r: (ctx_tiles + b * nblk + r, h)),
        out_shape=jax.ShapeDtypeStruct(prev.shape, prev.dtype),
        input_output_aliases={10: 0},
        scratch_shapes=[pltpu.VMEM((t, hd), _bf16), pltpu.VMEM((t, hd), _bf16)],
        compiler_params=_params(("parallel", "parallel", "arbitrary")),
        name="gqa_latent",
    )(z, z, z, cache_k, cache_v, cos, sin, cos, sin, gqa_w, prev)


def _rope_tables():
    t = np.arange(DEC_SEQ)
    row = (t // GRID_W).astype(np.float32)
    col = (t % GRID_W).astype(np.float32)
    half = HEAD_DIM // 2
    inv = jnp.asarray(ROPE_THETA, _f32) ** (-jnp.arange(0, half, 2, dtype=_f32) / half)
    ang_r = jnp.asarray(row)[:, None] * inv[None, :]
    ang_c = jnp.asarray(col)[:, None] * inv[None, :]
    cos = jnp.concatenate([jnp.cos(ang_r), jnp.cos(ang_r), jnp.cos(ang_c), jnp.cos(ang_c)], axis=-1)
    sin = jnp.concatenate([-jnp.sin(ang_r), jnp.sin(ang_r), -jnp.sin(ang_c), jnp.sin(ang_c)], axis=-1)
    return cos, sin


def _chunk_constants():
    c = CHUNK
    t = np.arange(c)
    tri, a_cat, pair, mq, mk, causal, strict = [], [], [], [], [], [], []
    for d in range(2):
        tau = t if d == 0 else c - 1 - t
        incl = (tau[None, :] <= tau[:, None]).astype(np.float32)
        tri.append(incl)
        causal.append(incl)
        strict.append((tau[None, :] < tau[:, None]).astype(np.float32))
        rows, pm, qm, km = [], [], [], []
        for li in range(N_LEVELS):
            s = c >> (li + 1)
            ref_tau = 2 * s * (tau // (2 * s)) + s - 1
            sel = (tau[None, :] == ref_tau[:, None]).astype(np.float32)
            rows.append(incl - sel @ incl)
            odd = ((tau // s) % 2 == 1).astype(np.float32)
            pm.append(((tau[:, None] // (2 * s)) == (tau[None, :] // (2 * s))).astype(np.float32))
            qm.append(np.repeat(odd[:, None], GLA_DK, axis=1))
            km.append(np.repeat((1.0 - odd)[:, None], GLA_DK, axis=1))
        pm.append(np.eye(c, dtype=np.float32))
        rows.append(incl)
        rows.append((tau[None, :] > tau[:, None]).astype(np.float32))
        a = np.concatenate(rows, axis=0)
        a_cat.append(np.concatenate([a, a, a], axis=1))
        pair.append(np.stack(pm))
        mq.append(np.stack(qm))
        mk.append(np.stack(km))
    tri_cat = np.stack([np.concatenate([x, x, x], axis=1) for x in tri])
    sub = np.stack([pair[d][:N_LEVELS] * mq[d][:, :, :1] * np.swapaxes(mk[d][:, :, :1], 1, 2) for d in range(2)])
    return dict(sub=jnp.asarray(sub), sub_b=jnp.asarray(sub, _bf16),
                a_cat=jnp.asarray(np.stack(a_cat), _bf16), pair=jnp.asarray(np.stack(pair)),
                mq=jnp.asarray(np.stack(mq)), mk=jnp.asarray(np.stack(mk)),
                tri_cat=jnp.asarray(tri_cat, _bf16),
                causal=jnp.asarray(np.stack([np.stack([causal[d], strict[d]]) for d in range(2)])))


def _chunk_pos(d, c, n):
    return c if d == 0 else n - 1 - c


def _gla_kernel(q_ref, k_ref, v_ref, r_ref, zs_ref, wg_ref, gb_ref, acat_ref, pair_ref, mq_ref, mk_ref,
                s0_ref, nw_ref, _prev_ref, o_ref, sn_ref, la_scr, oacc_scr, s_scr, *, t):
    c, dk, dv, nh = CHUNK, GLA_DK, GLA_DV, GLA_HEADS
    n = t // c
    zs = _bf(zs_ref[...])
    for d in range(2):
        x = _dot(zs, wg_ref[d]) + gb_ref[d]
        la_scr[d] = (jnp.minimum(x, 0.0) - jnp.log(1.0 + jnp.exp(-jnp.abs(x)))) * (1.0 / GLA_TAU)
        for h in range(nh):
            s_scr[d * nh + h] = s0_ref[0, d, h]
    ones = jnp.ones((3 * c, LANES), _bf16)
    chains = [(d, h) for d in range(2) for h in range(nh)]

    def chunk(ci, carry):
        rows = [pl.ds(pl.multiple_of(_chunk_pos(d, ci, n) * c, c), c) for d in range(2)]
        states = [s_scr[d * nh + h] for d, h in chains]
        dall, blast = [], []
        for d in range(2):
            gcat = jnp.concatenate(_split3(la_scr[d, rows[d], :]), axis=0)
            dall.append(_dot(acat_ref[d], gcat))
            blast.append(_dot_tn(gcat, ones))
        qs, ks, vs = [], [], []
        for d, h in chains:
            kc = slice(h * dk, (h + 1) * dk)
            qs.append(q_ref[rows[d], kc] * (dk ** -0.5))
            ks.append(k_ref[rows[d], kc])
            vs.append(_bf(v_ref[rows[d], h * dv:(h + 1) * dv]))
        atts = [_dot_nt(_bf(q), _bf(k)) * pair_ref[d, N_LEVELS] for (d, h), q, k in zip(chains, qs, ks)]
        for li in range(N_LEVELS):
            for j, (d, h) in enumerate(chains):
                f = jnp.exp(-jnp.abs(dall[d][li * c:(li + 1) * c, h * dk:(h + 1) * dk]))
                atts[j] += _dot_nt(_bf(qs[j] * f * mq_ref[d, li]), _bf(ks[j] * f * mk_ref[d, li])) * pair_ref[d, li]
        outs, new_states = [], []
        for j, (d, h) in enumerate(chains):
            kc = slice(h * dk, (h + 1) * dk)
            eb = jnp.exp(dall[d][N_LEVELS * c:(N_LEVELS + 1) * c, kc])
            el = jnp.exp(dall[d][(N_LEVELS + 1) * c:(N_LEVELS + 2) * c, kc])
            outs.append(_dot(_bf(qs[j] * eb), _bf(states[j])) + _dot(_bf(atts[j]), vs[j]))
            new_states.append(states[j] * jnp.exp(blast[d][kc, :]) + _dot_tn(_bf(ks[j] * el), vs[j]))
        for j, (d, h) in enumerate(chains):
            s_scr[d * nh + h] = new_states[j]
            oacc_scr[d, rows[d], h * dv:(h + 1) * dv] = outs[j]
        return carry

    lax.fori_loop(0, n, chunk, 0)

    for d, h in chains:
        sn_ref[0, d, h] = s_scr[d * nh + h]
    for h in range(nh):
        vc = slice(h * dv, (h + 1) * dv)
        o_ref[:, vc] = _bf(_rms(oacc_scr[0, :, vc] + oacc_scr[1, :, vc], nw_ref[...]) * _silu(r_ref[:, vc]))


def gla(z, row_blk0, nseq, t, wg, gbias, consts, s0, out_norm, prev=None):
    qw = GLA_HEADS * GLA_DK
    vw = GLA_HEADS * GLA_DV
    if prev is None:
        prev = jnp.zeros((z.shape[0], vw), _bf16)

    def zspec(name, width):
        blk = Z_OFF[name] // width
        assert Z_OFF[name] % width == 0
        return pl.BlockSpec((t, width), lambda b: (row_blk0 + b, blk))

    def full(a):
        return pl.BlockSpec(a.shape, lambda b, nd=a.ndim: (0,) * nd)

    state_spec = pl.BlockSpec((1, 2, GLA_HEADS, GLA_DK, GLA_DV), lambda b: (b, 0, 0, 0, 0))
    cs = [consts['a_cat'], consts['pair'], consts['mq'], consts['mk']]
    return pl.pallas_call(
        functools.partial(_gla_kernel, t=t),
        grid=(nseq,),
        in_specs=[zspec('gla_q', qw), zspec('gla_k', qw), zspec('gla_v', vw), zspec('gla_r', vw),
                  zspec('small', LANES), full(wg), full(gbias)] + [full(a) for a in cs]
                 + [state_spec, pl.BlockSpec((1, GLA_DV), lambda b: (0, 0)), pl.BlockSpec(memory_space=pl.ANY)],
        out_specs=[pl.BlockSpec((t, vw), lambda b: (row_blk0 + b, 0)), state_spec],
        out_shape=[jax.ShapeDtypeStruct(prev.shape, prev.dtype),
                   jax.ShapeDtypeStruct((nseq, 2, GLA_HEADS, GLA_DK, GLA_DV), _f32)],
        input_output_aliases={7 + len(cs) + 2: 0},
        scratch_shapes=[pltpu.VMEM((2, t, qw), _f32), pltpu.VMEM((2, t, vw), _f32),
                        pltpu.VMEM((2 * GLA_HEADS, GLA_DK, GLA_DV), _f32)],
        compiler_params=_params(("parallel",)),
        name="gla",
    )(z, z, z, z, z, wg, gbias, *cs, s0, out_norm.reshape(1, GLA_DV), prev)


CONV_PAD = 8
ROW_BLK = 128
GDN_LOCAL_CHUNKS = 2


def _gdn_kernel(q_ref, k_ref, v_ref, zz_ref, zs_ref, cw_ref, alog_ref, dtb_ref, tri_ref, causal_ref, sub_ref,
                subb_ref, s0_ref, nw_ref, _prev_ref, o_ref, sn_ref,
                xp_scr, qkv_scr, g_scr, beta_scr, b_scr, u_scr, w_scr, att_scr, oacc_scr, s_scr, *, t):
    c, hd, nh = CHUNK, HEAD_DIM, GDN_HEADS
    n = t // c
    w = nh * hd
    half = CONV_K // 2
    nblk = t // ROW_BLK
    eye = (lax.broadcasted_iota(jnp.int32, (c, c), 0) == lax.broadcasted_iota(jnp.int32, (c, c), 1)).astype(_f32)

    xp_scr[0:CONV_PAD, :] = jnp.zeros((CONV_PAD, w), _f32)
    xp_scr[CONV_PAD + t:2 * CONV_PAD + t, :] = jnp.zeros((CONV_PAD, w), _f32)
    for idx, src in enumerate((q_ref, k_ref, v_ref)):
        def copy_in(i, carry, src=src):
            r0 = pl.multiple_of(i * ROW_BLK, ROW_BLK)
            xp_scr[pl.ds(CONV_PAD + r0, ROW_BLK), :] = src[pl.ds(r0, ROW_BLK), :]
            return carry

        lax.fori_loop(0, nblk, copy_in, 0)

        def conv(i, carry, idx=idx):
            r0 = pl.multiple_of(i * ROW_BLK, ROW_BLK)
            win = xp_scr[pl.ds(r0, ROW_BLK + 2 * CONV_PAD), :]
            y = jnp.zeros((ROW_BLK, w), _f32)
            for j in range(CONV_K):
                lo = CONV_PAD + j - half
                y += win[lo:lo + ROW_BLK, :] * cw_ref[idx, j:j + 1, :]
            y = _silu(y)
            for h in range(nh):
                cols = slice(h * hd, (h + 1) * hd)
                yh = y[:, cols]
                if idx == 0:
                    yh = yh * lax.rsqrt(jnp.sum(yh * yh, axis=-1, keepdims=True) + EPS) * (hd ** -0.5)
                elif idx == 1:
                    yh = yh * lax.rsqrt(jnp.sum(yh * yh, axis=-1, keepdims=True) + EPS)
                qkv_scr[idx, pl.ds(r0, ROW_BLK), cols] = yh
            return carry

        lax.fori_loop(0, nblk, conv, 0)

    zs = zs_ref[...]
    g_scr[...] = -jnp.exp(alog_ref[...]) * _softplus(zs + dtb_ref[...])
    beta_scr[...] = _sigmoid(zs)
    for d in range(2):
        for h in range(nh):
            s_scr[d * nh + h] = s0_ref[0, d, h]

    chains = [(d, h) for d in range(2) for h in range(nh)]

    def local(ci, carry):
        jobs, rows, b_all, lows, xs, rhss = [], [], [], [], [], []
        for cc in range(GDN_LOCAL_CHUNKS):
            rr = pl.ds(pl.multiple_of((ci * GDN_LOCAL_CHUNKS + cc) * c, c), c)
            rows.append(rr)
            gcat = jnp.concatenate(_split3(g_scr[rr, :]), axis=0)
            beta_all = beta_scr[rr, :]
            b_cc = [_dot(tri_ref[d], gcat) for d in range(2)]
            b_all.append(b_cc)
            b_t = [b.T for b in b_cc]
            qk, kk, ks, vs = [], [], [], []
            for h in range(nh):
                cols = slice(h * hd, (h + 1) * hd)
                k = qkv_scr[1, rr, cols]
                kbf = _bf(k)
                ks.append(k)
                vs.append(qkv_scr[2, rr, cols])
                kk.append(_dot_nt(kbf, kbf))
                qk.append(_dot_nt(_bf(qkv_scr[0, rr, cols]), kbf))
            for d, h in chains:
                ia = SMALL_A + d * nh + h
                ib = SMALL_BETA + d * nh + h
                bcol = b_cc[d][:, ia:ia + 1]
                decay = jnp.exp(jnp.minimum(bcol - b_t[d][ia:ia + 1, :], 0.0)) * causal_ref[d, 0]
                beta = beta_all[:, ib:ib + 1]
                att_scr[d * nh + h, rr, :] = _bf(qk[h] * decay)
                low = kk[h] * beta * decay * causal_ref[d, 1]
                jobs.append((cc, d, h))
                lows.append(_split2(low))
                xs.append(eye - low * sub_ref[d, N_LEVELS - 1])
                rhss.append(jnp.concatenate([vs[h] * beta, ks[h] * (beta * jnp.exp(bcol))], axis=1))
        for li in range(N_LEVELS - 2, -1, -1):
            for j, (cc, d, h) in enumerate(jobs):
                mask = subb_ref[d, li]
                xp = _split2(xs[j])
                t1 = _dot_parts(xp, (lows[j][0] * mask, lows[j][1] * mask))
                xs[j] = xs[j] - _dot_parts(_split2(t1), xp)
        for j, (cc, d, h) in enumerate(jobs):
            cols = slice(h * hd, (h + 1) * hd)
            sol = _dot_parts(_split2(xs[j]), _split2(rhss[j]))
            u_scr[d, rows[cc], cols] = sol[:, :hd]
            w_scr[d, rows[cc], cols] = _bf(sol[:, hd:])
        for cc in range(GDN_LOCAL_CHUNKS):
            for d in range(2):
                b_scr[d, rows[cc], :] = b_all[cc][d]
        return carry

    lax.fori_loop(0, n // GDN_LOCAL_CHUNKS, local, 0)

    def step(ci, carry):
        rows = [pl.ds(pl.multiple_of(_chunk_pos(d, ci, n) * c, c), c) for d in range(2)]
        states = [s_scr[d * nh + h] for d, h in chains]
        b_all = [b_scr[d, rows[d], :] for d in range(2)]
        us, ws, atts, qs, ks = [], [], [], [], []
        for d, h in chains:
            cols = slice(h * hd, (h + 1) * hd)
            us.append(u_scr[d, rows[d], cols])
            ws.append(w_scr[d, rows[d], cols])
            atts.append(att_scr[d * nh + h, rows[d], :])
            qs.append(qkv_scr[0, rows[d], cols])
            ks.append(qkv_scr[1, rows[d], cols])
        sbs = [_bf(s) for s in states]
        v_new = [_bf(u - _dot(w_, sb)) for u, w_, sb in zip(us, ws, sbs)]
        outs, new_states = [], []
        for j, (d, h) in enumerate(chains):
            ia = SMALL_A + d * nh + h
            last = c - 1 if d == 0 else 0
            bcol = b_all[d][:, ia:ia + 1]
            b_last = b_all[d][last:last + 1, ia:ia + 1]
            outs.append(_dot(_bf(qs[j] * jnp.exp(bcol)), sbs[j]) + _dot(atts[j], v_new[j]))
            new_states.append(states[j] * jnp.exp(b_last) + _dot_tn(_bf(ks[j] * jnp.exp(b_last - bcol)), v_new[j]))
        for j, (d, h) in enumerate(chains):
            s_scr[d * nh + h] = new_states[j]
            oacc_scr[d, rows[d], h * hd:(h + 1) * hd] = outs[j]
        return carry

    lax.fori_loop(0, n, step, 0)

    for d, h in chains:
        sn_ref[0, d, h] = s_scr[d * nh + h]

    def epilogue(i, carry):
        r0 = pl.multiple_of(i * ROW_BLK, ROW_BLK)
        rr = pl.ds(r0, ROW_BLK)
        for h in range(nh):
            cols = slice(h * hd, (h + 1) * hd)
            o = oacc_scr[0, rr, cols] + oacc_scr[1, rr, cols]
            o_ref[rr, cols] = _bf(_rms(o, nw_ref[...]) * _silu(zz_ref[rr, cols]))
        return carry

    lax.fori_loop(0, nblk, epilogue, 0)


def gdn(z, row_blk0, nseq, t, conv_w, alog_lane, dtb_lane, consts, s0, out_norm, prev=None):
    w = GDN_HEADS * HEAD_DIM
    if prev is None:
        prev = jnp.zeros((z.shape[0], w), _bf16)
    qblk = Z_OFF['gdn_qkv'] // w
    assert Z_OFF['gdn_qkv'] % w == 0 and Z_OFF['gdn_z'] % w == 0 and Z_OFF['small'] % LANES == 0

    def zspec(blk, width):
        return pl.BlockSpec((t, width), lambda b: (row_blk0 + b, blk))

    def full(a):
        return pl.BlockSpec(a.shape, lambda b, nd=a.ndim: (0,) * nd)

    state_spec = pl.BlockSpec((1, 2, GDN_HEADS, GDN_DK, GDN_DV), lambda b: (b, 0, 0, 0, 0))
    return pl.pallas_call(
        functools.partial(_gdn_kernel, t=t),
        grid=(nseq,),
        in_specs=[zspec(qblk, w), zspec(qblk + 1, w), zspec(qblk + 2, w), zspec(Z_OFF['gdn_z'] // w, w),
                  zspec(Z_OFF['small'] // LANES, LANES), full(conv_w), full(alog_lane), full(dtb_lane),
                  full(consts['tri_cat']), full(consts['causal']), full(consts['sub']), full(consts['sub_b']),
                  state_spec, pl.BlockSpec((1, GDN_DV), lambda b: (0, 0)), pl.BlockSpec(memory_space=pl.ANY)],
        out_specs=[pl.BlockSpec((t, w), lambda b: (row_blk0 + b, 0)), state_spec],
        out_shape=[jax.ShapeDtypeStruct(prev.shape, prev.dtype),
                   jax.ShapeDtypeStruct((nseq, 2, GDN_HEADS, GDN_DK, GDN_DV), _f32)],
        input_output_aliases={14: 0},
        scratch_shapes=[pltpu.VMEM((t + 2 * CONV_PAD, w), _f32),
                        pltpu.VMEM((3, t, w), _f32),
                        pltpu.VMEM((t, LANES), _f32),
                        pltpu.VMEM((t, LANES), _f32),
                        pltpu.VMEM((2, t, LANES), _f32),
                        pltpu.VMEM((2, t, w), _f32),
                        pltpu.VMEM((2, t, w), _bf16),
                        pltpu.VMEM((2 * GDN_HEADS, t, CHUNK), _bf16),
                        pltpu.VMEM((2, t, w), _f32),
                        pltpu.VMEM((2 * GDN_HEADS, GDN_DK, GDN_DV), _f32)],
        compiler_params=_params(("parallel",)),
        name="gdn",
    )(z, z, z, z, z, conv_w, alog_lane, dtb_lane, consts['tri_cat'], consts['causal'], consts['sub'],
      consts['sub_b'], s0, out_norm.reshape(1, GDN_DV), prev)


def _permute_columns(w_in):
    parts, start = [], 0
    n = len(Z_PERM)
    while start < n:
        stop = start + 1
        if Z_PERM[start] < 0:
            while stop < n and Z_PERM[stop] < 0:
                stop += 1
            parts.append(jnp.zeros(w_in.shape[:2] + (stop - start,), _bf16))
        else:
            while stop < n and Z_PERM[stop] == Z_PERM[stop - 1] + 1:
                stop += 1
            parts.append(w_in[..., int(Z_PERM[start]):int(Z_PERM[stop - 1]) + 1].astype(_bf16))
        start = stop
    return jnp.concatenate(parts, axis=-1)


def _lane_vector(values, offset):
    k = values.shape[-1]
    return jnp.pad(values.astype(_f32), ((0, 0), (offset, LANES - offset - k)))[:, None, :]


def kernel(x_prompt, x_sample, cache_na_k, cache_na_v, cache_gqa_k, cache_gqa_v, state_gla, state_gdn, c, c_ctx, norm_g, w_mod, b_mod, ffn_gu, ffn_down, w_in, w_out, na_qk_norm, na_rpb, gla_gate_up, gla_gate_bias, gla_out_norm, gqa_qk_norm, gdn_conv, gdn_a_log, gdn_dt_bias, gdn_out_norm):
    assert GRID_ROWS % NA_QROWS == 0 and GRID_ROWS >= NA_KROWS and PAST_LEN == SEQ
    d = D_MODEL
    m_ctx = BATCH * SEQ
    assert m_ctx % DEC_SEQ == 0 and DEC_SEQ % TOKEN_TILE == 0
    x = jnp.concatenate([x_prompt.reshape(m_ctx, d), x_sample.reshape(DEC_BATCH * DEC_SEQ, d)], axis=0)

    n_cond = 1 + DEC_BATCH
    cond = jnp.concatenate([c_ctx[None, :], c], axis=0)
    cond = jnp.pad(cond, ((0, (-n_cond) % 8), (0, 0)))
    mod = adaln_all(cond, w_mod, b_mod).reshape(DEPTH, cond.shape[0], N_MOD, d)

    w_gate, w_up, down = cast_ffn_weights(ffn_gu, ffn_down)
    w_in_p = _permute_columns(w_in)
    w_out_b = w_out.astype(_bf16)

    consts = _chunk_constants()
    cos, sin = _rope_tables()
    bias = na_bias_tables(na_rpb)

    qw = GLA_HEADS * GLA_DK
    wg = jnp.zeros((DEPTH, 2, LANES, qw), _f32)
    wg = wg.at[:, 0, 0:GLA_GATE_RANK].set(gla_gate_up[:, 0])
    wg = wg.at[:, 1, SMALL_GB:SMALL_GB + GLA_GATE_RANK].set(gla_gate_up[:, 1]).astype(_bf16)
    gbias = gla_gate_bias.reshape(DEPTH, 2, 1, qw)
    conv_w = gdn_conv.reshape(DEPTH, CONV_K, 3, GDN_HEADS * HEAD_DIM).transpose(0, 2, 1, 3)
    conv_w = jnp.pad(conv_w, ((0, 0), (0, 0), (0, 8 - CONV_K), (0, 0)))
    alog_lane = _lane_vector(gdn_a_log.reshape(DEPTH, 2 * GDN_HEADS), SMALL_A)
    dtb_lane = _lane_vector(gdn_dt_bias.reshape(DEPTH, 2 * GDN_HEADS), SMALL_A)
    gla_zero = jnp.zeros((BATCH, 2, GLA_HEADS, GLA_DK, GLA_DV), _f32)
    gdn_zero = jnp.zeros((BATCH, 2, GDN_HEADS, GDN_DK, GDN_DV), _f32)
    lat_blk0 = m_ctx // DEC_SEQ

    na_k_l, na_v_l, gqa_k_l, gqa_v_l, gla_l, gdn_l = [], [], [], [], [], []
    for l in range(DEPTH):
        x = ffn(x, mod[l], norm_g[l, 0], w_gate, w_up, down, 2 * l, 0)
        z = in_proj(x, mod[l], norm_g[l, 1], w_in_p, l)

        o_na, o_gqa, kn, vn, ka, va = ctx_attention(z, na_qk_norm[l], gqa_qk_norm[l])
        o_na = na_latent(z, cache_na_k, cache_na_v, bias, na_qk_norm[l], l, o_na)
        o_gqa = gqa_latent(z, cache_gqa_k, cache_gqa_v, cos, sin, gqa_qk_norm[l], l, o_gqa)
        o_gla, sg = gla(z, 0, BATCH, SEQ, wg[l], gbias[l], consts, gla_zero, gla_out_norm[l])
        o_gla, _ = gla(z, lat_blk0, DEC_BATCH, DEC_SEQ, wg[l], gbias[l], consts, state_gla[:, l],
                       gla_out_norm[l], o_gla)
        o_gdn, sd = gdn(z, 0, BATCH, SEQ, conv_w[l], alog_lane[l], dtb_lane[l], consts, gdn_zero,
                        gdn_out_norm[l])
        o_gdn, _ = gdn(z, lat_blk0, DEC_BATCH, DEC_SEQ, conv_w[l], alog_lane[l], dtb_lane[l], consts,
                       state_gdn[:, l], gdn_out_norm[l], o_gdn)

        x = out_proj(x, mod[l], o_na, o_gla, o_gqa, o_gdn, w_out_b, l)
        x = ffn(x, mod[l], norm_g[l, 2], w_gate, w_up, down, 2 * l + 1, 6)

        na_k_l.append(kn)
        na_v_l.append(vn)
        gqa_k_l.append(ka)
        gqa_v_l.append(va)
        gla_l.append(sg)
        gdn_l.append(sd)

    y_prompt = x[:m_ctx].reshape(BATCH, SEQ, d)
    y_sample = x[m_ctx:].reshape(DEC_BATCH, DEC_SEQ, d)
    return (y_prompt, y_sample, jnp.stack(na_k_l, axis=1), jnp.stack(na_v_l, axis=1),
            jnp.stack(gqa_k_l, axis=1), jnp.stack(gqa_v_l, axis=1),
            jnp.stack(gla_l, axis=1), jnp.stack(gdn_l, axis=1))
```

```python
import functools
import math

import jax
import jax.numpy as jnp
import numpy as np
from jax import lax
from jax.experimental import pallas as pl
from jax.experimental.pallas import tpu as pltpu

D_MODEL = 2048
BATCH = 16
SEQ = 256
DEPTH = 4
DEC_BATCH = 8
DEC_SEQ = 1024
PAST_LEN = 256

GRID_W = 64
HEAD_DIM = 128
GROUP_WIDTH = D_MODEL // 4
NA_HEADS = GROUP_WIDTH // HEAD_DIM
NA_WIN_ROWS = 8
NA_WIN_COLS = 16
GLA_HEADS = 4
GLA_DV = GROUP_WIDTH // GLA_HEADS
GLA_DK = GLA_DV // 2
GLA_GATE_RANK = 16
GLA_TAU = 16.0
GQA_HEADS = GROUP_WIDTH // HEAD_DIM
GQA_KV_HEADS = GQA_HEADS // 2
GDN_HEADS = GROUP_WIDTH // HEAD_DIM
GDN_DK = HEAD_DIM
GDN_DV = HEAD_DIM
CONV_K = 5
CHUNK = 64
ROPE_THETA = 10000.0
FFN_DIM = ((8 * D_MODEL // 3 + 127) // 128) * 128
N_MOD = 9
EPS = 1e-6
NEG_INF = -1e30

IN_SPLITS = (
    ('na_q', NA_HEADS * HEAD_DIM), ('na_k', NA_HEADS * HEAD_DIM), ('na_v', NA_HEADS * HEAD_DIM),
    ('gla_q', GLA_HEADS * GLA_DK), ('gla_k', GLA_HEADS * GLA_DK), ('gla_v', GLA_HEADS * GLA_DV),
    ('gla_r', GLA_HEADS * GLA_DV), ('gla_gf', GLA_GATE_RANK), ('gla_gb', GLA_GATE_RANK),
    ('gqa_q', GQA_HEADS * HEAD_DIM), ('gqa_k', GQA_KV_HEADS * HEAD_DIM), ('gqa_v', GQA_KV_HEADS * HEAD_DIM),
    ('gdn_qkv', 3 * GDN_HEADS * HEAD_DIM), ('gdn_z', GDN_HEADS * GDN_DV),
    ('gdn_b', 2 * GDN_HEADS), ('gdn_a', 2 * GDN_HEADS),
)

LANES = 128
FFN_TILE = 512
FFN_PAD = ((FFN_DIM + FFN_TILE - 1) // FFN_TILE) * FFN_TILE
TOKEN_TILE = 512
INPROJ_TOKEN_TILE = 1024
FFN_TOKEN_TILE = 512
VMEM_LIMIT = 56 * 1024 * 1024

Z_ORDER = ('na_q', 'na_k', 'na_v', 'gla_q', 'gla_k', 'gla_v', 'gla_r', 'gqa_q', 'gqa_k', 'gqa_v',
           'gdn_qkv', 'gdn_z')
SMALL_ORDER = ('gla_gf', 'gla_gb', 'gdn_b', 'gdn_a')
SMALL_GB = GLA_GATE_RANK
SMALL_BETA = 2 * GLA_GATE_RANK
SMALL_A = SMALL_BETA + 2 * GDN_HEADS
N_LEVELS = int(math.log2(CHUNK))


def _layout():
    src, off = {}, 0
    for name, width in IN_SPLITS:
        src[name] = (off, width)
        off += width
    perm, zoff, pos = [], {}, 0
    for name in Z_ORDER:
        o, w = src[name]
        zoff[name] = pos
        perm.extend(range(o, o + w))
        pos += w
    zoff['small'] = pos
    n_small = 0
    for name in SMALL_ORDER:
        o, w = src[name]
        perm.extend(range(o, o + w))
        n_small += w
    perm.extend([-1] * (LANES - n_small))
    pos += LANES
    perm.extend([-1] * ((-pos) % Z_TILE))
    return np.asarray(perm, np.int32), zoff, len(perm)


Z_TILE = 5 * 256
Z_PERM, Z_OFF, Z_COLS = _layout()

_f32 = jnp.float32
_bf16 = jnp.bfloat16


def _dot(a, b):
    return jnp.dot(a, b, preferred_element_type=_f32)


def _dot_nt(a, b):
    return lax.dot_general(a, b, (((1,), (1,)), ((), ())), preferred_element_type=_f32)


def _dot_tn(a, b):
    return lax.dot_general(a, b, (((0,), (0,)), ((), ())), preferred_element_type=_f32)


def _bf(x):
    return x.astype(_bf16)


def _sigmoid(x):
    return 1.0 / (1.0 + jnp.exp(-x))


def _silu(x):
    return x * _sigmoid(x)


def _softplus(x):
    return jnp.maximum(x, 0.0) + jnp.log(1.0 + jnp.exp(-jnp.abs(x)))


def _rms(x, w):
    return x * lax.rsqrt(jnp.mean(x * x, axis=-1, keepdims=True) + EPS) * w


def _split3(x):
    hi = _bf(x)
    r1 = x - hi.astype(_f32)
    mid = _bf(r1)
    lo = _bf(r1 - mid.astype(_f32))
    return hi, mid, lo


def _split2(x):
    hi = _bf(x)
    return hi, _bf(x - hi.astype(_f32))


def _dot_parts(a, b):
    return _dot(jnp.concatenate([a[0], a[0], a[1]], axis=1), jnp.concatenate([b[0], b[1], b[0]], axis=0))


def _cond_row(i, tile):
    n_ctx = (BATCH * SEQ) // tile
    return jnp.where(i < n_ctx, 0, 1 + (i - n_ctx) // (DEC_SEQ // tile))


def _params(sem):
    return pltpu.CompilerParams(dimension_semantics=sem, vmem_limit_bytes=VMEM_LIMIT)


def _adaln_kernel(c_ref, w_ref, b_ref, o_ref):
    a = _bf(_silu(c_ref[...]))
    o_ref[0] = _dot(a, _bf(w_ref[0])) + b_ref[0]


def adaln_all(cond, w_mod, b_mod):
    nc = cond.shape[0]
    tn = 1024
    n = N_MOD * D_MODEL
    return pl.pallas_call(
        _adaln_kernel,
        grid=(DEPTH, n // tn),
        in_specs=[pl.BlockSpec((nc, D_MODEL), lambda l, j: (0, 0)),
                  pl.BlockSpec((1, D_MODEL, tn), lambda l, j: (l, 0, j)),
                  pl.BlockSpec((1, 1, tn), lambda l, j: (l, 0, j))],
        out_specs=pl.BlockSpec((1, nc, tn), lambda l, j: (l, 0, j)),
        out_shape=jax.ShapeDtypeStruct((DEPTH, nc, n), _f32),
        compiler_params=_params(("parallel", "parallel")),
        name="adaln",
    )(cond, w_mod, b_mod.reshape(DEPTH, 1, n))


CAST_BLK = 256
CAST_CHUNK = 1024


def _cast_gu_kernel(w_ref, g_ref, u_ref):
    f = FFN_DIM
    for c0 in range(0, f, CAST_CHUNK):
        c1 = min(c0 + CAST_CHUNK, f)
        g_ref[0, :, c0:c1] = _bf(w_ref[0, :, c0:c1])
        u_ref[0, :, c0:c1] = _bf(w_ref[0, :, f + c0:f + c1])
    if FFN_PAD > f:
        g_ref[0, :, f:] = jnp.zeros((CAST_BLK, FFN_PAD - f), _bf16)
        u_ref[0, :, f:] = jnp.zeros((CAST_BLK, FFN_PAD - f), _bf16)


def _cast_down_kernel(w_ref, o_ref):
    f = FFN_DIM
    for r0 in range(0, f, CAST_CHUNK):
        r1 = min(r0 + CAST_CHUNK, f)
        o_ref[0, r0:r1, :] = _bf(w_ref[0, r0:r1, :])
    if FFN_PAD > f:
        o_ref[0, f:, :] = jnp.zeros((FFN_PAD - f, CAST_BLK), _bf16)


def cast_ffn_weights(ffn_gu, ffn_down):
    d, f = D_MODEL, FFN_DIM
    n = ffn_gu.shape[0] * ffn_gu.shape[1]
    gate, up = pl.pallas_call(
        _cast_gu_kernel,
        grid=(n, d // CAST_BLK),
        in_specs=[pl.BlockSpec((1, CAST_BLK, 2 * f), lambda i, r: (i, r, 0))],
        out_specs=[pl.BlockSpec((1, CAST_BLK, FFN_PAD), lambda i, r: (i, r, 0))] * 2,
        out_shape=[jax.ShapeDtypeStruct((n, d, FFN_PAD), _bf16)] * 2,
        compiler_params=_params(("parallel", "parallel")),
        name="cast_gate_up",
    )(ffn_gu.reshape(n, d, 2 * f))
    down = pl.pallas_call(
        _cast_down_kernel,
        grid=(n, d // CAST_BLK),
        in_specs=[pl.BlockSpec((1, f, CAST_BLK), lambda i, r: (i, 0, r))],
        out_specs=pl.BlockSpec((1, FFN_PAD, CAST_BLK), lambda i, r: (i, 0, r)),
        out_shape=jax.ShapeDtypeStruct((n, FFN_PAD, d), _bf16),
        compiler_params=_params(("parallel", "parallel")),
        name="cast_down",
    )(ffn_down.reshape(n, f, d))
    return gate, up, down


def _modulated_norm(x, g, m_ref, base):
    sh = m_ref[0, base:base + 1, :]
    sc = m_ref[0, base + 1:base + 2, :]
    return _rms(x, g) * (1.0 + sc) + sh


def _ffn_kernel(x_ref, m_ref, g_ref, wg_ref, wu_ref, wd_ref, o_ref, h_scr, acc_scr, *, base):
    f = pl.program_id(1)

    @pl.when(f == 0)
    def _():
        h_scr[...] = _bf(_modulated_norm(x_ref[...], g_ref[...], m_ref, base))
        acc_scr[...] = jnp.zeros_like(acc_scr)

    h = h_scr[...]
    gate = _dot(h, wg_ref[...])
    up = _dot(h, wu_ref[...])
    acc_scr[...] += _dot(_bf(_silu(gate) * up), wd_ref[...])

    @pl.when(f == pl.num_programs(1) - 1)
    def _():
        o_ref[...] = x_ref[...] + (0.5 * m_ref[0, base + 2:base + 3, :]) * acc_scr[...]


def ffn(x, mod, norm_g, w_gate, w_up, w_down, widx, base):
    m, d = x.shape
    nf = FFN_PAD // FFN_TILE
    tm = FFN_TOKEN_TILE
    return pl.pallas_call(
        functools.partial(_ffn_kernel, base=base),
        grid=(m // tm, nf),
        in_specs=[pl.BlockSpec((tm, d), lambda i, f: (i, 0)),
                  pl.BlockSpec((1, N_MOD, d), lambda i, f: (_cond_row(i, tm), 0, 0)),
                  pl.BlockSpec((1, d), lambda i, f: (0, 0)),
                  pl.BlockSpec((None, d, FFN_TILE), lambda i, f: (widx, 0, f)),
                  pl.BlockSpec((None, d, FFN_TILE), lambda i, f: (widx, 0, f)),
                  pl.BlockSpec((None, FFN_TILE, d), lambda i, f: (widx, f, 0))],
        out_specs=pl.BlockSpec((tm, d), lambda i, f: (i, 0)),
        out_shape=jax.ShapeDtypeStruct((m, d), _f32),
        scratch_shapes=[pltpu.VMEM((tm, d), _bf16), pltpu.VMEM((tm, d), _f32)],
        compiler_params=_params(("parallel", "arbitrary")),
        name="ffn",
    )(x, mod, norm_g.reshape(1, d), w_gate, w_up, w_down)


def _inproj_kernel(x_ref, m_ref, g_ref, w_ref, o_ref, h_scr):
    @pl.when(pl.program_id(1) == 0)
    def _():
        h_scr[...] = _bf(_modulated_norm(x_ref[...], g_ref[...], m_ref, 3))

    o_ref[...] = _dot(h_scr[...], w_ref[...])


def in_proj(x, mod, norm_g, w_in, layer):
    m, d = x.shape
    tm = INPROJ_TOKEN_TILE
    return pl.pallas_call(
        _inproj_kernel,
        grid=(m // tm, Z_COLS // Z_TILE),
        in_specs=[pl.BlockSpec((tm, d), lambda i, j: (i, 0)),
                  pl.BlockSpec((1, N_MOD, d), lambda i, j: (_cond_row(i, tm), 0, 0)),
                  pl.BlockSpec((1, d), lambda i, j: (0, 0)),
                  pl.BlockSpec((None, d, Z_TILE), lambda i, j: (layer, 0, j))],
        out_specs=pl.BlockSpec((tm, Z_TILE), lambda i, j: (i, j)),
        out_shape=jax.ShapeDtypeStruct((m, Z_COLS), _f32),
        scratch_shapes=[pltpu.VMEM((tm, d), _bf16)],
        compiler_params=_params(("parallel", "arbitrary")),
        name="in_proj",
    )(x, mod, norm_g.reshape(1, d), w_in)


def _outproj_kernel(x_ref, m_ref, a_ref, b_ref, c_ref, d_ref, w_ref, o_ref):
    gw = GROUP_WIDTH
    acc = _dot(a_ref[...], w_ref[0:gw, :])
    acc += _dot(b_ref[...], w_ref[gw:2 * gw, :])
    acc += _dot(c_ref[...], w_ref[2 * gw:3 * gw, :])
    acc += _dot(d_ref[...], w_ref[3 * gw:4 * gw, :])
    o_ref[...] = x_ref[...] + m_ref[0, 5:6, :] * acc


def out_proj(x, mod, o_na, o_gla, o_gqa, o_gdn, w_out, layer):
    m, d = x.shape
    tm = TOKEN_TILE
    grp = pl.BlockSpec((tm, GROUP_WIDTH), lambda i: (i, 0))
    return pl.pallas_call(
        _outproj_kernel,
        grid=(m // tm,),
        in_specs=[pl.BlockSpec((tm, d), lambda i: (i, 0)),
                  pl.BlockSpec((1, N_MOD, d), lambda i: (_cond_row(i, tm), 0, 0)),
                  grp, grp, grp, grp,
                  pl.BlockSpec((None, 4 * GROUP_WIDTH, d), lambda i: (layer, 0, 0))],
        out_specs=pl.BlockSpec((tm, d), lambda i: (i, 0)),
        out_shape=jax.ShapeDtypeStruct((m, d), _f32),
        compiler_params=_params(("parallel",)),
        name="out_proj",
    )(x, mod, o_na, o_gla, o_gqa, o_gdn, w_out)


def _softmax_pv(scores, values):
    mx = scores[0].max(axis=-1, keepdims=True)
    for s in scores[1:]:
        mx = jnp.maximum(mx, s.max(axis=-1, keepdims=True))
    num, den = None, None
    for s, v in zip(scores, values):
        p = jnp.exp(s - mx)
        d_ = p.sum(axis=-1, keepdims=True)
        n_ = _dot(_bf(p), v)
        num = n_ if num is None else num + n_
        den = d_ if den is None else den + d_
    return num / den


def _rope(x, cos, sin):
    lane = lax.broadcasted_iota(jnp.int32, x.shape, 1)
    quarter = HEAD_DIM // 4
    partner = jnp.where((lane % (2 * quarter)) < quarter,
                        pltpu.roll(x, HEAD_DIM - quarter, 1), pltpu.roll(x, quarter, 1))
    return x * cos + partner * sin


def _ctx_attn_kernel(nq_ref, nk_ref, nv_ref, gq_ref, gk_ref, gv_ref, nw_ref, gw_ref, *refs, n_alias):
    ona_ref, ogqa_ref, kn_ref, vn_ref, ka_ref, va_ref = refs[n_alias:]
    hd = HEAD_DIM
    scale = hd ** -0.5
    for h in range(NA_HEADS):
        cols = slice(h * hd, (h + 1) * hd)
        k = _rms(nk_ref[:, cols], nw_ref[1:2, :])
        v = nv_ref[:, cols]
        q = _rms(nq_ref[:, cols], nw_ref[0:1, :])
        kn_ref[0, h] = k
        vn_ref[0, h] = v
        s = _dot_nt(_bf(q), _bf(k)) * scale
        ona_ref[:, cols] = _bf(_softmax_pv([s], [_bf(v)]))
    group = GQA_HEADS // GQA_KV_HEADS
    for kv in range(GQA_KV_HEADS):
        cols = slice(kv * hd, (kv + 1) * hd)
        k = _rms(gk_ref[:, cols], gw_ref[1:2, :])
        v = gv_ref[:, cols]
        ka_ref[0, kv] = k
        va_ref[0, kv] = v
        for g in range(group):
            qcols = slice((kv * group + g) * hd, (kv * group + g + 1) * hd)
            q = _rms(gq_ref[:, qcols], gw_ref[0:1, :])
            s = _dot_nt(_bf(q), _bf(k)) * scale
            ogqa_ref[:, qcols] = _bf(_softmax_pv([s], [_bf(v)]))


def ctx_attention(z, na_w, gqa_w, layer, caches):
    t, hd = SEQ, HEAD_DIM

    def zspec(name, width):
        blk = Z_OFF[name] // width
        assert Z_OFF[name] % width == 0
        return pl.BlockSpec((t, width), lambda b: (b, blk))

    def cache_spec(heads):
        return pl.BlockSpec((1, None, heads, t, hd), lambda b: (b, layer, 0, 0, 0))

    def cache_shape(heads):
        return jax.ShapeDtypeStruct((BATCH, DEPTH, heads, t, hd), _f32)

    gw = GROUP_WIDTH
    kvw = GQA_KV_HEADS * hd
    in_specs = [zspec('na_q', gw), zspec('na_k', gw), zspec('na_v', gw),
                zspec('gqa_q', gw), zspec('gqa_k', kvw), zspec('gqa_v', kvw),
                pl.BlockSpec((2, hd), lambda b: (0, 0)), pl.BlockSpec((2, hd), lambda b: (0, 0))]
    args = [z, z, z, z, z, z, na_w, gqa_w]
    aliases = {}
    if caches is not None:
        aliases = {len(args) + i: 2 + i for i in range(4)}
        in_specs += [pl.BlockSpec(memory_space=pl.ANY)] * 4
        args += list(caches)
    outs = pl.pallas_call(
        functools.partial(_ctx_attn_kernel, n_alias=len(aliases)),
        grid=(BATCH,),
        in_specs=in_specs,
        out_specs=[pl.BlockSpec((t, gw), lambda b: (b, 0)), pl.BlockSpec((t, gw), lambda b: (b, 0)),
                   cache_spec(NA_HEADS), cache_spec(NA_HEADS), cache_spec(GQA_KV_HEADS), cache_spec(GQA_KV_HEADS)],
        out_shape=[jax.ShapeDtypeStruct((z.shape[0], gw), _bf16), jax.ShapeDtypeStruct((z.shape[0], gw), _bf16),
                   cache_shape(NA_HEADS), cache_shape(NA_HEADS), cache_shape(GQA_KV_HEADS),
                   cache_shape(GQA_KV_HEADS)],
        input_output_aliases=aliases,
        compiler_params=_params(("parallel",)),
        name="ctx_attention",
    )(*args)
    return outs[0], outs[1], tuple(outs[2:])


NA_QROWS = 4
NA_KROWS = NA_QROWS + NA_WIN_ROWS
GRID_ROWS = DEC_SEQ // GRID_W


def _na_key_row0(blk):
    return min(max(blk * NA_QROWS - NA_WIN_ROWS // 2, 0), GRID_ROWS - NA_KROWS)


def _na_bias_kernel(rpb_ref, o_ref):
    l, h = pl.program_id(0), pl.program_id(1)
    n_dr, n_dc = 2 * NA_WIN_ROWS - 1, 2 * NA_WIN_COLS - 1
    base = (l * NA_HEADS + h) * (n_dr * n_dc)
    qc = lax.broadcasted_iota(jnp.int32, (GRID_W, GRID_W), 0)
    kc = lax.broadcasted_iota(jnp.int32, (GRID_W, GRID_W), 1)
    dc = jnp.clip(kc - qc + (NA_WIN_COLS - 1), 0, n_dc - 1)
    c0 = jnp.clip(qc - NA_WIN_COLS // 2, 0, GRID_W - NA_WIN_COLS)
    in_win = (kc >= c0) & (kc < c0 + NA_WIN_COLS)
    masked = jnp.full((GRID_W, GRID_W), NEG_INF, _f32)
    tiles = []
    for dr in range(n_dr):
        t = jnp.zeros((GRID_W, GRID_W), _f32)
        for j in range(n_dc):
            t = jnp.where(dc == j, rpb_ref[base + dr * n_dc + j], t)
        tiles.append(jnp.where(in_win, t, NEG_INF))
    for blk in range(GRID_ROWS // NA_QROWS):
        k0 = _na_key_row0(blk)
        for qi in range(NA_QROWS):
            r = blk * NA_QROWS + qi
            krow0 = min(max(r - NA_WIN_ROWS // 2, 0), GRID_ROWS - NA_WIN_ROWS)
            for kj in range(NA_KROWS):
                kr = k0 + kj
                ok = krow0 <= kr < krow0 + NA_WIN_ROWS
                tile = tiles[kr - r + NA_WIN_ROWS - 1] if ok else masked
                o_ref[0, 0, blk, qi * GRID_W:(qi + 1) * GRID_W, kj * GRID_W:(kj + 1) * GRID_W] = tile


def na_bias_tables(na_rpb):
    nblk = GRID_ROWS // NA_QROWS
    shape = (DEPTH, NA_HEADS, nblk, NA_QROWS * GRID_W, NA_KROWS * GRID_W)
    return pl.pallas_call(
        _na_bias_kernel,
        grid=(DEPTH, NA_HEADS),
        in_specs=[pl.BlockSpec(memory_space=pltpu.SMEM)],
        out_specs=pl.BlockSpec((1, 1) + shape[2:], lambda l, h: (l, h, 0, 0, 0)),
        out_shape=jax.ShapeDtypeStruct(shape, _f32),
        compiler_params=_params(("parallel", "parallel")),
        name="na_bias",
    )(na_rpb.reshape(-1))


def _na_lat_kernel(q_ref, k_ref, v_ref, ck_ref, cv_ref, bias_ref, nw_ref, _prev_ref, o_ref, kn_scr, vb_scr):
    blk = pl.program_id(2)
    scale = HEAD_DIM ** -0.5

    @pl.when(blk == 0)
    def _():
        kn_scr[...] = _bf(_rms(k_ref[...], nw_ref[1:2, :]))
        vb_scr[...] = _bf(v_ref[...])

    nblk = GRID_ROWS // NA_QROWS
    row0 = jnp.int32(_na_key_row0(0))
    for b_ in range(1, nblk):
        row0 = jnp.where(blk == b_, _na_key_row0(b_), row0)
    start = pl.multiple_of(row0 * GRID_W, GRID_W)
    nkeys = NA_KROWS * GRID_W
    q = _bf(_rms(q_ref[...], nw_ref[0:1, :]))
    s_ctx = _dot_nt(q, _bf(ck_ref[0, 0, 0])) * scale
    s_loc = _dot_nt(q, kn_scr[pl.ds(start, nkeys), :]) * scale + bias_ref[0, 0, 0]
    o = _softmax_pv([s_ctx, s_loc], [_bf(cv_ref[0, 0, 0]), vb_scr[pl.ds(start, nkeys), :]])
    o_ref[...] = _bf(o)


def na_latent(z, cache_k, cache_v, bias, na_w, layer, prev):
    t, hd = DEC_SEQ, HEAD_DIM
    tq = NA_QROWS * GRID_W
    nblk = t // tq
    ctx_tiles = (BATCH * SEQ) // tq
    ctx_seqs = (BATCH * SEQ) // t
    qb, kb, vb = Z_OFF['na_q'] // hd, Z_OFF['na_k'] // hd, Z_OFF['na_v'] // hd
    cache_spec = pl.BlockSpec((1, 1, 1, PAST_LEN, hd), lambda b, h, r: (b, layer, h, 0, 0))
    return pl.pallas_call(
        _na_lat_kernel,
        grid=(DEC_BATCH, NA_HEADS, nblk),
        in_specs=[pl.BlockSpec((tq, hd), lambda b, h, r: (ctx_tiles + b * nblk + r, qb + h)),
                  pl.BlockSpec((t, hd), lambda b, h, r: (ctx_seqs + b, kb + h)),
                  pl.BlockSpec((t, hd), lambda b, h, r: (ctx_seqs + b, vb + h)),
                  cache_spec, cache_spec,
                  pl.BlockSpec((1, 1, 1, tq, NA_KROWS * GRID_W), lambda b, h, r: (layer, h, r, 0, 0)),
                  pl.BlockSpec((2, hd), lambda b, h, r: (0, 0)),
                  pl.BlockSpec(memory_space=pl.ANY)],
        out_specs=pl.BlockSpec((tq, hd), lambda b, h, r: (ctx_tiles + b * nblk + r, h)),
        out_shape=jax.ShapeDtypeStruct(prev.shape, prev.dtype),
        input_output_aliases={7: 0},
        scratch_shapes=[pltpu.VMEM((t, hd), _bf16), pltpu.VMEM((t, hd), _bf16)],
        compiler_params=_params(("parallel", "parallel", "arbitrary")),
        name="na_latent",
    )(z, z, z, cache_k, cache_v, bias, na_w, prev)


GQA_TQ = 256


def _gqa_lat_kernel(q_ref, k_ref, v_ref, ck_ref, cv_ref, cq_ref, sq_ref, ck_all_ref, sk_all_ref, gw_ref,
                    _prev_ref, o_ref, kr_scr, vb_scr):
    hd = HEAD_DIM
    scale = hd ** -0.5

    @pl.when(pl.program_id(2) == 0)
    def _():
        kr_scr[...] = _bf(_rope(_rms(k_ref[...], gw_ref[1:2, :]), ck_all_ref[...], sk_all_ref[...]))
        vb_scr[...] = _bf(v_ref[...])

    ck = _bf(ck_ref[0, 0, 0])
    cv = _bf(cv_ref[0, 0, 0])
    for g in range(GQA_HEADS // GQA_KV_HEADS):
        cols = slice(g * hd, (g + 1) * hd)
        q = _bf(_rope(_rms(q_ref[:, cols], gw_ref[0:1, :]), cq_ref[...], sq_ref[...]))
        s_ctx = _dot_nt(q, ck) * scale
        s_loc = _dot_nt(q, kr_scr[...]) * scale
        o_ref[:, cols] = _bf(_softmax_pv([s_ctx, s_loc], [cv, vb_scr[...]]))


def gqa_latent(z, cache_k, cache_v, cos, sin, gqa_w, layer, prev):
    t, hd = DEC_SEQ, HEAD_DIM
    tq = GQA_TQ
    nblk = t // tq
    group = GQA_HEADS // GQA_KV_HEADS
    ctx_tiles = (BATCH * SEQ) // tq
    ctx_seqs = (BATCH * SEQ) // t
    qb = Z_OFF['gqa_q'] // (group * hd)
    kb, vb = Z_OFF['gqa_k'] // hd, Z_OFF['gqa_v'] // hd
    cache_spec = pl.BlockSpec((1, 1, 1, PAST_LEN, hd), lambda b, h, r: (b, layer, h, 0, 0))
    return pl.pallas_call(
        _gqa_lat_kernel,
        grid=(DEC_BATCH, GQA_KV_HEADS, nblk),
        in_specs=[pl.BlockSpec((tq, group * hd), lambda b, h, r: (ctx_tiles + b * nblk + r, qb + h)),
                  pl.BlockSpec((t, hd), lambda b, h, r: (ctx_seqs + b, kb + h)),
                  pl.BlockSpec((t, hd), lambda b, h, r: (ctx_seqs + b, vb + h)),
                  cache_spec, cache_spec,
                  pl.BlockSpec((tq, hd), lambda b, h, r: (r, 0)),
                  pl.BlockSpec((tq, hd), lambda b, h, r: (r, 0)),
                  pl.BlockSpec((t, hd), lambda b, h, r: (0, 0)),
                  pl.BlockSpec((t, hd), lambda b, h, r: (0, 0)),
                  pl.BlockSpec((2, hd), lambda b, h, r: (0, 0)),
                  pl.BlockSpec(memory_space=pl.ANY)],
        out_specs=pl.BlockSpec((tq, group * hd), lambda b, h, r: (ctx_tiles + b * nblk + r, h)),
        out_shape=jax.ShapeDtypeStruct(prev.shape, prev.dtype),
        input_output_aliases={10: 0},
        scratch_shapes=[pltpu.VMEM((t, hd), _bf16), pltpu.VMEM((t, hd), _bf16)],
        compiler_params=_params(("parallel", "parallel", "arbitrary")),
        name="gqa_latent",
    )(z, z, z, cache_k, cache_v, cos, sin, cos, sin, gqa_w, prev)


def _rope_tables():
    t = np.arange(DEC_SEQ)
    row = (t // GRID_W).astype(np.float32)
    col = (t % GRID_W).astype(np.float32)
    half = HEAD_DIM // 2
    inv = jnp.asarray(ROPE_THETA, _f32) ** (-jnp.arange(0, half, 2, dtype=_f32) / half)
    ang_r = jnp.asarray(row)[:, None] * inv[None, :]
    ang_c = jnp.asarray(col)[:, None] * inv[None, :]
    cos = jnp.concatenate([jnp.cos(ang_r), jnp.cos(ang_r), jnp.cos(ang_c), jnp.cos(ang_c)], axis=-1)
    sin = jnp.concatenate([-jnp.sin(ang_r), jnp.sin(ang_r), -jnp.sin(ang_c), jnp.sin(ang_c)], axis=-1)
    return cos, sin


def _chunk_constants():
    c = CHUNK
    t = np.arange(c)
    tri, a_cat, pair, mq, mk, causal, strict = [], [], [], [], [], [], []
    for d in range(2):
        tau = t if d == 0 else c - 1 - t
        incl = (tau[None, :] <= tau[:, None]).astype(np.float32)
        tri.append(incl)
        causal.append(incl)
        strict.append((tau[None, :] < tau[:, None]).astype(np.float32))
        rows, pm, qm, km = [], [], [], []
        for li in range(N_LEVELS):
            s = c >> (li + 1)
            ref_tau = 2 * s * (tau // (2 * s)) + s - 1
            sel = (tau[None, :] == ref_tau[:, None]).astype(np.float32)
            rows.append(incl - sel @ incl)
            odd = ((tau // s) % 2 == 1).astype(np.float32)
            pm.append(((tau[:, None] // (2 * s)) == (tau[None, :] // (2 * s))).astype(np.float32))
            qm.append(np.repeat(odd[:, None], GLA_DK, axis=1))
            km.append(np.repeat((1.0 - odd)[:, None], GLA_DK, axis=1))
        pm.append(np.eye(c, dtype=np.float32))
        rows.append(incl)
        rows.append((tau[None, :] > tau[:, None]).astype(np.float32))
        a = np.concatenate(rows, axis=0)
        a_cat.append(np.concatenate([a, a, a], axis=1))
        pair.append(np.stack(pm))
        mq.append(np.stack(qm))
        mk.append(np.stack(km))
    tri_cat = np.stack([np.concatenate([x, x, x], axis=1) for x in tri])
    sub = np.stack([pair[d][:N_LEVELS] * mq[d][:, :, :1] * np.swapaxes(mk[d][:, :, :1], 1, 2) for d in range(2)])
    return dict(sub=jnp.asarray(sub), sub_b=jnp.asarray(sub, _bf16),
                a_cat=jnp.asarray(np.stack(a_cat), _bf16), pair=jnp.asarray(np.stack(pair)),
                mq=jnp.asarray(np.stack(mq)), mk=jnp.asarray(np.stack(mk)),
                tri_cat=jnp.asarray(tri_cat, _bf16),
                causal=jnp.asarray(np.stack([np.stack([causal[d], strict[d]]) for d in range(2)])))


def _chunk_pos(d, c, n):
    return c if d == 0 else n - 1 - c


def _gla_kernel(q_ref, k_ref, v_ref, r_ref, zs_ref, wg_ref, gb_ref, acat_ref, pair_ref, mq_ref, mk_ref,
                s0_ref, nw_ref, _prev_ref, o_ref, sn_ref, la_scr, oacc_scr, s_scr, *, t):
    c, dk, dv, nh = CHUNK, GLA_DK, GLA_DV, GLA_HEADS
    n = t // c
    zs = _bf(zs_ref[...])
    for d in range(2):
        x = _dot(zs, wg_ref[d]) + gb_ref[d]
        la_scr[d] = (jnp.minimum(x, 0.0) - jnp.log(1.0 + jnp.exp(-jnp.abs(x)))) * (1.0 / GLA_TAU)
        for h in range(nh):
            s_scr[d * nh + h] = s0_ref[0, d, h]
    ones = jnp.ones((3 * c, LANES), _bf16)
    chains = [(d, h) for d in range(2) for h in range(nh)]

    def chunk(ci, carry):
        rows = [pl.ds(pl.multiple_of(_chunk_pos(d, ci, n) * c, c), c) for d in range(2)]
        states = [s_scr[d * nh + h] for d, h in chains]
        dall, blast = [], []
        for d in range(2):
            gcat = jnp.concatenate(_split3(la_scr[d, rows[d], :]), axis=0)
            dall.append(_dot(acat_ref[d], gcat))
            blast.append(_dot_tn(gcat, ones))
        qs, ks, vs = [], [], []
        for d, h in chains:
            kc = slice(h * dk, (h + 1) * dk)
            qs.append(q_ref[rows[d], kc] * (dk ** -0.5))
            ks.append(k_ref[rows[d], kc])
            vs.append(_bf(v_ref[rows[d], h * dv:(h + 1) * dv]))
        atts = [_dot_nt(_bf(q), _bf(k)) * pair_ref[d, N_LEVELS] for (d, h), q, k in zip(chains, qs, ks)]
        for li in range(N_LEVELS):
            for j, (d, h) in enumerate(chains):
                f = jnp.exp(-jnp.abs(dall[d][li * c:(li + 1) * c, h * dk:(h + 1) * dk]))
                atts[j] += _dot_nt(_bf(qs[j] * f * mq_ref[d, li]), _bf(ks[j] * f * mk_ref[d, li])) * pair_ref[d, li]
        outs, new_states = [], []
        for j, (d, h) in enumerate(chains):
            kc = slice(h * dk, (h + 1) * dk)
            eb = jnp.exp(dall[d][N_LEVELS * c:(N_LEVELS + 1) * c, kc])
            el = jnp.exp(dall[d][(N_LEVELS + 1) * c:(N_LEVELS + 2) * c, kc])
            outs.append(_dot(_bf(qs[j] * eb), _bf(states[j])) + _dot(_bf(atts[j]), vs[j]))
            new_states.append(states[j] * jnp.exp(blast[d][kc, :]) + _dot_tn(_bf(ks[j] * el), vs[j]))
        for j, (d, h) in enumerate(chains):
            s_scr[d * nh + h] = new_states[j]
            oacc_scr[d, rows[d], h * dv:(h + 1) * dv] = outs[j]
        return carry

    lax.fori_loop(0, n, chunk, 0)

    for d, h in chains:
        sn_ref[0, d, h] = s_scr[d * nh + h]
    for h in range(nh):
        vc = slice(h * dv, (h + 1) * dv)
        o_ref[:, vc] = _bf(_rms(oacc_scr[0, :, vc] + oacc_scr[1, :, vc], nw_ref[...]) * _silu(r_ref[:, vc]))


def gla(z, row_blk0, nseq, t, wg, gbias, consts, s0, out_norm, prev=None):
    qw = GLA_HEADS * GLA_DK
    vw = GLA_HEADS * GLA_DV
    if prev is None:
        prev = jnp.zeros((z.shape[0], vw), _bf16)

    def zspec(name, width):
        blk = Z_OFF[name] // width
        assert Z_OFF[name] % width == 0
        return pl.BlockSpec((t, width), lambda b: (row_blk0 + b, blk))

    def full(a):
        return pl.BlockSpec(a.shape, lambda b, nd=a.ndim: (0,) * nd)

    state_spec = pl.BlockSpec((1, 2, GLA_HEADS, GLA_DK, GLA_DV), lambda b: (b, 0, 0, 0, 0))
    cs = [consts['a_cat'], consts['pair'], consts['mq'], consts['mk']]
    return pl.pallas_call(
        functools.partial(_gla_kernel, t=t),
        grid=(nseq,),
        in_specs=[zspec('gla_q', qw), zspec('gla_k', qw), zspec('gla_v', vw), zspec('gla_r', vw),
                  zspec('small', LANES), full(wg), full(gbias)] + [full(a) for a in cs]
                 + [state_spec, pl.BlockSpec((1, GLA_DV), lambda b: (0, 0)), pl.BlockSpec(memory_space=pl.ANY)],
        out_specs=[pl.BlockSpec((t, vw), lambda b: (row_blk0 + b, 0)), state_spec],
        out_shape=[jax.ShapeDtypeStruct(prev.shape, prev.dtype),
                   jax.ShapeDtypeStruct((nseq, 2, GLA_HEADS, GLA_DK, GLA_DV), _f32)],
        input_output_aliases={7 + len(cs) + 2: 0},
        scratch_shapes=[pltpu.VMEM((2, t, qw), _f32), pltpu.VMEM((2, t, vw), _f32),
                        pltpu.VMEM((2 * GLA_HEADS, GLA_DK, GLA_DV), _f32)],
        compiler_params=_params(("parallel",)),
        name="gla",
    )(z, z, z, z, z, wg, gbias, *cs, s0, out_norm.reshape(1, GLA_DV), prev)


CONV_PAD = 8
ROW_BLK = 128
GDN_LOCAL_CHUNKS = 2


def _gdn_kernel(q_ref, k_ref, v_ref, zz_ref, zs_ref, cw_ref, alog_ref, dtb_ref, tri_ref, causal_ref, sub_ref,
                subb_ref, s0_ref, nw_ref, _prev_ref, o_ref, sn_ref,
                xp_scr, qkv_scr, g_scr, beta_scr, b_scr, u_scr, w_scr, att_scr, oacc_scr, s_scr, *, t):
    c, hd, nh = CHUNK, HEAD_DIM, GDN_HEADS
    n = t // c
    w = nh * hd
    half = CONV_K // 2
    nblk = t // ROW_BLK
    eye = (lax.broadcasted_iota(jnp.int32, (c, c), 0) == lax.broadcasted_iota(jnp.int32, (c, c), 1)).astype(_f32)

    xp_scr[0:CONV_PAD, :] = jnp.zeros((CONV_PAD, w), _f32)
    xp_scr[CONV_PAD + t:2 * CONV_PAD + t, :] = jnp.zeros((CONV_PAD, w), _f32)
    for idx, src in enumerate((q_ref, k_ref, v_ref)):
        def copy_in(i, carry, src=src):
            r0 = pl.multiple_of(i * ROW_BLK, ROW_BLK)
            xp_scr[pl.ds(CONV_PAD + r0, ROW_BLK), :] = src[pl.ds(r0, ROW_BLK), :]
            return carry

        lax.fori_loop(0, nblk, copy_in, 0)

        def conv(i, carry, idx=idx):
            r0 = pl.multiple_of(i * ROW_BLK, ROW_BLK)
            win = xp_scr[pl.ds(r0, ROW_BLK + 2 * CONV_PAD), :]
            y = jnp.zeros((ROW_BLK, w), _f32)
            for j in range(CONV_K):
                lo = CONV_PAD + j - half
                y += win[lo:lo + ROW_BLK, :] * cw_ref[idx, j:j + 1, :]
            y = _silu(y)
            for h in range(nh):
                cols = slice(h * hd, (h + 1) * hd)
                yh = y[:, cols]
                if idx == 0:
                    yh = yh * lax.rsqrt(jnp.sum(yh * yh, axis=-1, keepdims=True) + EPS) * (hd ** -0.5)
                elif idx == 1:
                    yh = yh * lax.rsqrt(jnp.sum(yh * yh, axis=-1, keepdims=True) + EPS)
                qkv_scr[idx, pl.ds(r0, ROW_BLK), cols] = yh
            return carry

        lax.fori_loop(0, nblk, conv, 0)

    zs = zs_ref[...]
    g_scr[...] = -jnp.exp(alog_ref[...]) * _softplus(zs + dtb_ref[...])
    beta_scr[...] = _sigmoid(zs)
    for d in range(2):
        for h in range(nh):
            s_scr[d * nh + h] = s0_ref[0, d, h]

    chains = [(d, h) for d in range(2) for h in range(nh)]

    def local(ci, carry):
        jobs, rows, b_all, lows, xs, rhss = [], [], [], [], [], []
        for cc in range(GDN_LOCAL_CHUNKS):
            rr = pl.ds(pl.multiple_of((ci * GDN_LOCAL_CHUNKS + cc) * c, c), c)
            rows.append(rr)
            gcat = jnp.concatenate(_split3(g_scr[rr, :]), axis=0)
            beta_all = beta_scr[rr, :]
            b_cc = [_dot(tri_ref[d], gcat) for d in range(2)]
            b_all.append(b_cc)
            b_t = [b.T for b in b_cc]
            qk, kk, ks, vs = [], [], [], []
            for h in range(nh):
                cols = slice(h * hd, (h + 1) * hd)
                k = qkv_scr[1, rr, cols]
                kbf = _bf(k)
                ks.append(k)
                vs.append(qkv_scr[2, rr, cols])
                kk.append(_dot_nt(kbf, kbf))
                qk.append(_dot_nt(_bf(qkv_scr[0, rr, cols]), kbf))
            for d, h in chains:
                ia = SMALL_A + d * nh + h
                ib = SMALL_BETA + d * nh + h
                bcol = b_cc[d][:, ia:ia + 1]
                decay = jnp.exp(jnp.minimum(bcol - b_t[d][ia:ia + 1, :], 0.0)) * causal_ref[d, 0]
                beta = beta_all[:, ib:ib + 1]
                att_scr[d * nh + h, rr, :] = _bf(qk[h] * decay)
                low = kk[h] * beta * decay * causal_ref[d, 1]
                jobs.append((cc, d, h))
                lows.append(_split2(low))
                xs.append(eye - low * sub_ref[d, N_LEVELS - 1])
                rhss.append(jnp.concatenate([vs[h] * beta, ks[h] * (beta * jnp.exp(bcol))], axis=1))
        for li in range(N_LEVELS - 2, -1, -1):
            for j, (cc, d, h) in enumerate(jobs):
                mask = subb_ref[d, li]
                xp = _split2(xs[j])
                t1 = _dot_parts(xp, (lows[j][0] * mask, lows[j][1] * mask))
                xs[j] = xs[j] - _dot_parts(_split2(t1), xp)
        for j, (cc, d, h) in enumerate(jobs):
            cols = slice(h * hd, (h + 1) * hd)
            sol = _dot_parts(_split2(xs[j]), _split2(rhss[j]))
            u_scr[d, rows[cc], cols] = sol[:, :hd]
            w_scr[d, rows[cc], cols] = _bf(sol[:, hd:])
        for cc in range(GDN_LOCAL_CHUNKS):
            for d in range(2):
                b_scr[d, rows[cc], :] = b_all[cc][d]
        return carry

    lax.fori_loop(0, n // GDN_LOCAL_CHUNKS, local, 0)

    def step(ci, carry):
        rows = [pl.ds(pl.multiple_of(_chunk_pos(d, ci, n) * c, c), c) for d in range(2)]
        states = [s_scr[d * nh + h] for d, h in chains]
        b_all = [b_scr[d, rows[d], :] for d in range(2)]
        us, ws, atts, qs, ks = [], [], [], [], []
        for d, h in chains:
            cols = slice(h * hd, (h + 1) * hd)
            us.append(u_scr[d, rows[d], cols])
            ws.append(w_scr[d, rows[d], cols])
            atts.append(att_scr[d * nh + h, rows[d], :])
            qs.append(qkv_scr[0, rows[d], cols])
            ks.append(qkv_scr[1, rows[d], cols])
        sbs = [_bf(s) for s in states]
        v_new = [_bf(u - _dot(w_, sb)) for u, w_, sb in zip(us, ws, sbs)]
        outs, new_states = [], []
        for j, (d, h) in enumerate(chains):
            ia = SMALL_A + d * nh + h
            last = c - 1 if d == 0 else 0
            bcol = b_all[d][:, ia:ia + 1]
            b_last = b_all[d][last:last + 1, ia:ia + 1]
            outs.append(_dot(_bf(qs[j] * jnp.exp(bcol)), sbs[j]) + _dot(atts[j], v_new[j]))
            new_states.append(states[j] * jnp.exp(b_last) + _dot_tn(_bf(ks[j] * jnp.exp(b_last - bcol)), v_new[j]))
        for j, (d, h) in enumerate(chains):
            s_scr[d * nh + h] = new_states[j]
            oacc_scr[d, rows[d], h * hd:(h + 1) * hd] = outs[j]
        return carry

    lax.fori_loop(0, n, step, 0)

    for d, h in chains:
        sn_ref[0, d, h] = s_scr[d * nh + h]

    def epilogue(i, carry):
        r0 = pl.multiple_of(i * ROW_BLK, ROW_BLK)
        rr = pl.ds(r0, ROW_BLK)
        for h in range(nh):
            cols = slice(h * hd, (h + 1) * hd)
            o = oacc_scr[0, rr, cols] + oacc_scr[1, rr, cols]
            o_ref[rr, cols] = _bf(_rms(o, nw_ref[...]) * _silu(zz_ref[rr, cols]))
        return carry

    lax.fori_loop(0, nblk, epilogue, 0)


def gdn(z, row_blk0, nseq, t, conv_w, alog_lane, dtb_lane, consts, s0, out_norm, prev=None):
    w = GDN_HEADS * HEAD_DIM
    if prev is None:
        prev = jnp.zeros((z.shape[0], w), _bf16)
    qblk = Z_OFF['gdn_qkv'] // w
    assert Z_OFF['gdn_qkv'] % w == 0 and Z_OFF['gdn_z'] % w == 0 and Z_OFF['small'] % LANES == 0

    def zspec(blk, width):
        return pl.BlockSpec((t, width), lambda b: (row_blk0 + b, blk))

    def full(a):
        return pl.BlockSpec(a.shape, lambda b, nd=a.ndim: (0,) * nd)

    state_spec = pl.BlockSpec((1, 2, GDN_HEADS, GDN_DK, GDN_DV), lambda b: (b, 0, 0, 0, 0))
    return pl.pallas_call(
        functools.partial(_gdn_kernel, t=t),
        grid=(nseq,),
        in_specs=[zspec(qblk, w), zspec(qblk + 1, w), zspec(qblk + 2, w), zspec(Z_OFF['gdn_z'] // w, w),
                  zspec(Z_OFF['small'] // LANES, LANES), full(conv_w), full(alog_lane), full(dtb_lane),
                  full(consts['tri_cat']), full(consts['causal']), full(consts['sub']), full(consts['sub_b']),
                  state_spec, pl.BlockSpec((1, GDN_DV), lambda b: (0, 0)), pl.BlockSpec(memory_space=pl.ANY)],
        out_specs=[pl.BlockSpec((t, w), lambda b: (row_blk0 + b, 0)), state_spec],
        out_shape=[jax.ShapeDtypeStruct(prev.shape, prev.dtype),
                   jax.ShapeDtypeStruct((nseq, 2, GDN_HEADS, GDN_DK, GDN_DV), _f32)],
        input_output_aliases={14: 0},
        scratch_shapes=[pltpu.VMEM((t + 2 * CONV_PAD, w), _f32),
                        pltpu.VMEM((3, t, w), _f32),
                        pltpu.VMEM((t, LANES), _f32),
                        pltpu.VMEM((t, LANES), _f32),
                        pltpu.VMEM((2, t, LANES), _f32),
                        pltpu.VMEM((2, t, w), _f32),
                        pltpu.VMEM((2, t, w), _bf16),
                        pltpu.VMEM((2 * GDN_HEADS, t, CHUNK), _bf16),
                        pltpu.VMEM((2, t, w), _f32),
                        pltpu.VMEM((2 * GDN_HEADS, GDN_DK, GDN_DV), _f32)],
        compiler_params=_params(("parallel",)),
        name="gdn",
    )(z, z, z, z, z, conv_w, alog_lane, dtb_lane, consts['tri_cat'], consts['causal'], consts['sub'],
      consts['sub_b'], s0, out_norm.reshape(1, GDN_DV), prev)


def _column_runs():
    runs, start, n = [], 0, len(Z_PERM)
    while start < n:
        stop = start + 1
        while (stop < n and stop - start < CAST_CHUNK
               and (Z_PERM[stop] == Z_PERM[stop - 1] + 1 if Z_PERM[start] >= 0 else Z_PERM[stop] < 0)):
            stop += 1
        runs.append((start, int(Z_PERM[start]), stop - start))
        start = stop
    return runs


def _permute_kernel(w_ref, o_ref):
    for dst, src, width in _column_runs():
        if src < 0:
            o_ref[0, :, dst:dst + width] = jnp.zeros((CAST_BLK, width), _bf16)
        else:
            o_ref[0, :, dst:dst + width] = _bf(w_ref[0, :, src:src + width])


def _permute_columns(w_in):
    depth, d, cols = w_in.shape
    return pl.pallas_call(
        _permute_kernel,
        grid=(depth, d // CAST_BLK),
        in_specs=[pl.BlockSpec((1, CAST_BLK, cols), lambda l, r: (l, r, 0))],
        out_specs=pl.BlockSpec((1, CAST_BLK, Z_COLS), lambda l, r: (l, r, 0)),
        out_shape=jax.ShapeDtypeStruct((depth, d, Z_COLS), _bf16),
        compiler_params=_params(("parallel", "parallel")),
        name="permute_w_in",
    )(w_in)


def _lane_vector(values, offset):
    k = values.shape[-1]
    return jnp.pad(values.astype(_f32), ((0, 0), (offset, LANES - offset - k)))[:, None, :]


def kernel(x_prompt, x_sample, cache_na_k, cache_na_v, cache_gqa_k, cache_gqa_v, state_gla, state_gdn, c, c_ctx, norm_g, w_mod, b_mod, ffn_gu, ffn_down, w_in, w_out, na_qk_norm, na_rpb, gla_gate_up, gla_gate_bias, gla_out_norm, gqa_qk_norm, gdn_conv, gdn_a_log, gdn_dt_bias, gdn_out_norm):
    assert GRID_ROWS % NA_QROWS == 0 and GRID_ROWS >= NA_KROWS and PAST_LEN == SEQ
    d = D_MODEL
    m_ctx = BATCH * SEQ
    assert m_ctx % DEC_SEQ == 0 and DEC_SEQ % TOKEN_TILE == 0
    x = jnp.concatenate([x_prompt.reshape(m_ctx, d), x_sample.reshape(DEC_BATCH * DEC_SEQ, d)], axis=0)

    n_cond = 1 + DEC_BATCH
    cond = jnp.concatenate([c_ctx[None, :], c], axis=0)
    cond = jnp.pad(cond, ((0, (-n_cond) % 8), (0, 0)))
    mod = adaln_all(cond, w_mod, b_mod).reshape(DEPTH, cond.shape[0], N_MOD, d)

    w_gate, w_up, down = cast_ffn_weights(ffn_gu, ffn_down)
    w_in_p = _permute_columns(w_in)
    w_out_b = w_out.astype(_bf16)

    consts = _chunk_constants()
    cos, sin = _rope_tables()
    bias = na_bias_tables(na_rpb)

    qw = GLA_HEADS * GLA_DK
    wg = jnp.zeros((DEPTH, 2, LANES, qw), _f32)
    wg = wg.at[:, 0, 0:GLA_GATE_RANK].set(gla_gate_up[:, 0])
    wg = wg.at[:, 1, SMALL_GB:SMALL_GB + GLA_GATE_RANK].set(gla_gate_up[:, 1]).astype(_bf16)
    gbias = gla_gate_bias.reshape(DEPTH, 2, 1, qw)
    conv_w = gdn_conv.reshape(DEPTH, CONV_K, 3, GDN_HEADS * HEAD_DIM).transpose(0, 2, 1, 3)
    conv_w = jnp.pad(conv_w, ((0, 0), (0, 0), (0, 8 - CONV_K), (0, 0)))
    alog_lane = _lane_vector(gdn_a_log.reshape(DEPTH, 2 * GDN_HEADS), SMALL_A)
    dtb_lane = _lane_vector(gdn_dt_bias.reshape(DEPTH, 2 * GDN_HEADS), SMALL_A)
    gla_zero = jnp.zeros((BATCH, 2, GLA_HEADS, GLA_DK, GLA_DV), _f32)
    gdn_zero = jnp.zeros((BATCH, 2, GDN_HEADS, GDN_DK, GDN_DV), _f32)
    lat_blk0 = m_ctx // DEC_SEQ

    gla_l, gdn_l = [], []
    caches = None
    for l in range(DEPTH):
        x = ffn(x, mod[l], norm_g[l, 0], w_gate, w_up, down, 2 * l, 0)
        z = in_proj(x, mod[l], norm_g[l, 1], w_in_p, l)

        o_na, o_gqa, caches = ctx_attention(z, na_qk_norm[l], gqa_qk_norm[l], l, caches)
        o_na = na_latent(z, cache_na_k, cache_na_v, bias, na_qk_norm[l], l, o_na)
        o_gqa = gqa_latent(z, cache_gqa_k, cache_gqa_v, cos, sin, gqa_qk_norm[l], l, o_gqa)
        o_gla, sg = gla(z, 0, BATCH, SEQ, wg[l], gbias[l], consts, gla_zero, gla_out_norm[l])
        o_gla, _ = gla(z, lat_blk0, DEC_BATCH, DEC_SEQ, wg[l], gbias[l], consts, state_gla[:, l],
                       gla_out_norm[l], o_gla)
        o_gdn, sd = gdn(z, 0, BATCH, SEQ, conv_w[l], alog_lane[l], dtb_lane[l], consts, gdn_zero,
                        gdn_out_norm[l])
        o_gdn, _ = gdn(z, lat_blk0, DEC_BATCH, DEC_SEQ, conv_w[l], alog_lane[l], dtb_lane[l], consts,
                       state_gdn[:, l], gdn_out_norm[l], o_gdn)

        x = out_proj(x, mod[l], o_na, o_gla, o_gqa, o_gdn, w_out_b, l)
        x = ffn(x, mod[l], norm_g[l, 2], w_gate, w_up, down, 2 * l + 1, 6)

        gla_l.append(sg)
        gdn_l.append(sd)

    y_prompt = x[:m_ctx].reshape(BATCH, SEQ, d)
    y_sample = x[m_ctx:].reshape(DEC_BATCH, DEC_SEQ, d)
    return (y_prompt, y_sample) + caches + (jnp.stack(gla_l, axis=1), jnp.stack(gdn_l, axis=1))
```

```python
import functools
import math

import jax
import jax.numpy as jnp
import numpy as np
from jax import lax
from jax.experimental import pallas as pl
from jax.experimental.pallas import tpu as pltpu

D_MODEL = 2048
BATCH = 16
SEQ = 256
DEPTH = 4
DEC_BATCH = 8
DEC_SEQ = 1024
PAST_LEN = 256

GRID_W = 64
HEAD_DIM = 128
GROUP_WIDTH = D_MODEL // 4
NA_HEADS = GROUP_WIDTH // HEAD_DIM
NA_WIN_ROWS = 8
NA_WIN_COLS = 16
GLA_HEADS = 4
GLA_DV = GROUP_WIDTH // GLA_HEADS
GLA_DK = GLA_DV // 2
GLA_GATE_RANK = 16
GLA_TAU = 16.0
GQA_HEADS = GROUP_WIDTH // HEAD_DIM
GQA_KV_HEADS = GQA_HEADS // 2
GDN_HEADS = GROUP_WIDTH // HEAD_DIM
GDN_DK = HEAD_DIM
GDN_DV = HEAD_DIM
CONV_K = 5
CHUNK = 64
ROPE_THETA = 10000.0
FFN_DIM = ((8 * D_MODEL // 3 + 127) // 128) * 128
N_MOD = 9
EPS = 1e-6
NEG_INF = -1e30

IN_SPLITS = (
    ('na_q', NA_HEADS * HEAD_DIM), ('na_k', NA_HEADS * HEAD_DIM), ('na_v', NA_HEADS * HEAD_DIM),
    ('gla_q', GLA_HEADS * GLA_DK), ('gla_k', GLA_HEADS * GLA_DK), ('gla_v', GLA_HEADS * GLA_DV),
    ('gla_r', GLA_HEADS * GLA_DV), ('gla_gf', GLA_GATE_RANK), ('gla_gb', GLA_GATE_RANK),
    ('gqa_q', GQA_HEADS * HEAD_DIM), ('gqa_k', GQA_KV_HEADS * HEAD_DIM), ('gqa_v', GQA_KV_HEADS * HEAD_DIM),
    ('gdn_qkv', 3 * GDN_HEADS * HEAD_DIM), ('gdn_z', GDN_HEADS * GDN_DV),
    ('gdn_b', 2 * GDN_HEADS), ('gdn_a', 2 * GDN_HEADS),
)

LANES = 128
FFN_TILE = 512
FFN_PAD = ((FFN_DIM + FFN_TILE - 1) // FFN_TILE) * FFN_TILE
TOKEN_TILE = 512
INPROJ_TOKEN_TILE = 1024
FFN_TOKEN_TILE = 512
VMEM_LIMIT = 56 * 1024 * 1024

Z_ORDER = ('na_q', 'na_k', 'na_v', 'gla_q', 'gla_k', 'gla_v', 'gla_r', 'gqa_q', 'gqa_k', 'gqa_v',
           'gdn_qkv', 'gdn_z')
SMALL_ORDER = ('gla_gf', 'gla_gb', 'gdn_b', 'gdn_a')
SMALL_GB = GLA_GATE_RANK
SMALL_BETA = 2 * GLA_GATE_RANK
SMALL_A = SMALL_BETA + 2 * GDN_HEADS
N_LEVELS = int(math.log2(CHUNK))


def _layout():
    src, off = {}, 0
    for name, width in IN_SPLITS:
        src[name] = (off, width)
        off += width
    perm, zoff, pos = [], {}, 0
    for name in Z_ORDER:
        o, w = src[name]
        zoff[name] = pos
        perm.extend(range(o, o + w))
        pos += w
    zoff['small'] = pos
    n_small = 0
    for name in SMALL_ORDER:
        o, w = src[name]
        perm.extend(range(o, o + w))
        n_small += w
    perm.extend([-1] * (LANES - n_small))
    pos += LANES
    perm.extend([-1] * ((-pos) % Z_TILE))
    return np.asarray(perm, np.int32), zoff, len(perm)


Z_TILE = 5 * 256
Z_PERM, Z_OFF, Z_COLS = _layout()

_f32 = jnp.float32
_bf16 = jnp.bfloat16


def _dot(a, b):
    return jnp.dot(a, b, preferred_element_type=_f32)


def _dot_nt(a, b):
    return lax.dot_general(a, b, (((1,), (1,)), ((), ())), preferred_element_type=_f32)


def _dot_tn(a, b):
    return lax.dot_general(a, b, (((0,), (0,)), ((), ())), preferred_element_type=_f32)


def _bf(x):
    return x.astype(_bf16)


def _sigmoid(x):
    return 1.0 / (1.0 + jnp.exp(-x))


def _silu(x):
    return x * _sigmoid(x)


def _softplus(x):
    return jnp.maximum(x, 0.0) + jnp.log(1.0 + jnp.exp(-jnp.abs(x)))


def _rms(x, w):
    return x * lax.rsqrt(jnp.mean(x * x, axis=-1, keepdims=True) + EPS) * w


def _split3(x):
    hi = _bf(x)
    r1 = x - hi.astype(_f32)
    mid = _bf(r1)
    lo = _bf(r1 - mid.astype(_f32))
    return hi, mid, lo


def _split2(x):
    hi = _bf(x)
    return hi, _bf(x - hi.astype(_f32))


def _dot_parts(a, b):
    return _dot(jnp.concatenate([a[0], a[0], a[1]], axis=1), jnp.concatenate([b[0], b[1], b[0]], axis=0))


def _cond_row(i, tile):
    n_ctx = (BATCH * SEQ) // tile
    return jnp.where(i < n_ctx, 0, 1 + (i - n_ctx) // (DEC_SEQ // tile))


def _params(sem):
    return pltpu.CompilerParams(dimension_semantics=sem, vmem_limit_bytes=VMEM_LIMIT)


def _adaln_kernel(c_ref, w_ref, b_ref, o_ref):
    a = _bf(_silu(c_ref[...]))
    o_ref[0] = _dot(a, _bf(w_ref[0])) + b_ref[0]


def adaln_all(cond, w_mod, b_mod):
    nc = cond.shape[0]
    tn = 1024
    n = N_MOD * D_MODEL
    return pl.pallas_call(
        _adaln_kernel,
        grid=(DEPTH, n // tn),
        in_specs=[pl.BlockSpec((nc, D_MODEL), lambda l, j: (0, 0)),
                  pl.BlockSpec((1, D_MODEL, tn), lambda l, j: (l, 0, j)),
                  pl.BlockSpec((1, 1, tn), lambda l, j: (l, 0, j))],
        out_specs=pl.BlockSpec((1, nc, tn), lambda l, j: (l, 0, j)),
        out_shape=jax.ShapeDtypeStruct((DEPTH, nc, n), _f32),
        compiler_params=_params(("parallel", "parallel")),
        name="adaln",
    )(cond, w_mod, b_mod.reshape(DEPTH, 1, n))


CAST_BLK = 256
CAST_CHUNK = 1024


def _cast_gu_kernel(w_ref, g_ref, u_ref):
    f = FFN_DIM
    for c0 in range(0, f, CAST_CHUNK):
        c1 = min(c0 + CAST_CHUNK, f)
        g_ref[0, :, c0:c1] = _bf(w_ref[0, :, c0:c1])
        u_ref[0, :, c0:c1] = _bf(w_ref[0, :, f + c0:f + c1])
    if FFN_PAD > f:
        g_ref[0, :, f:] = jnp.zeros((CAST_BLK, FFN_PAD - f), _bf16)
        u_ref[0, :, f:] = jnp.zeros((CAST_BLK, FFN_PAD - f), _bf16)


def _cast_down_kernel(w_ref, o_ref):
    f = FFN_DIM
    for r0 in range(0, f, CAST_CHUNK):
        r1 = min(r0 + CAST_CHUNK, f)
        o_ref[0, r0:r1, :] = _bf(w_ref[0, r0:r1, :])
    if FFN_PAD > f:
        o_ref[0, f:, :] = jnp.zeros((FFN_PAD - f, CAST_BLK), _bf16)


def cast_ffn_weights(ffn_gu, ffn_down):
    d, f = D_MODEL, FFN_DIM
    n = ffn_gu.shape[0] * ffn_gu.shape[1]
    gate, up = pl.pallas_call(
        _cast_gu_kernel,
        grid=(n, d // CAST_BLK),
        in_specs=[pl.BlockSpec((1, CAST_BLK, 2 * f), lambda i, r: (i, r, 0))],
        out_specs=[pl.BlockSpec((1, CAST_BLK, FFN_PAD), lambda i, r: (i, r, 0))] * 2,
        out_shape=[jax.ShapeDtypeStruct((n, d, FFN_PAD), _bf16)] * 2,
        compiler_params=_params(("parallel", "parallel")),
        name="cast_gate_up",
    )(ffn_gu.reshape(n, d, 2 * f))
    down = pl.pallas_call(
        _cast_down_kernel,
        grid=(n, d // CAST_BLK),
        in_specs=[pl.BlockSpec((1, f, CAST_BLK), lambda i, r: (i, 0, r))],
        out_specs=pl.BlockSpec((1, FFN_PAD, CAST_BLK), lambda i, r: (i, 0, r)),
        out_shape=jax.ShapeDtypeStruct((n, FFN_PAD, d), _bf16),
        compiler_params=_params(("parallel", "parallel")),
        name="cast_down",
    )(ffn_down.reshape(n, f, d))
    return gate, up, down


def _modulated_norm(x, g, m_ref, base):
    sh = m_ref[0, base:base + 1, :]
    sc = m_ref[0, base + 1:base + 2, :]
    return _rms(x, g) * (1.0 + sc) + sh


def _ffn_kernel(x_ref, xn_ref, m_ref, mn_ref, g_ref, wg_ref, wu_ref, wd_ref, o_ref, h0_scr, h1_scr, acc_scr, *,
                base, nsplit):
    i, f = pl.program_id(0), pl.program_id(1)

    @pl.when((i == 0) & (f == 0))
    def _():
        h0_scr[...] = _bf(_modulated_norm(x_ref[...], g_ref[...], m_ref, base))

    @pl.when(f == 0)
    def _():
        acc_scr[...] = jnp.zeros_like(acc_scr)

    rows_per = xn_ref.shape[0]
    rr = pl.ds(pl.multiple_of(jnp.minimum(f, nsplit - 1) * rows_per, rows_per), rows_per)

    def step(h_cur, h_nxt):
        h = h_cur[...]
        gate = _dot(h, wg_ref[...])
        up = _dot(h, wu_ref[...])
        acc_scr[...] += _dot(_bf(_silu(gate) * up), wd_ref[...])
        h_nxt[rr, :] = _bf(_modulated_norm(xn_ref[...], g_ref[...], mn_ref, base))

    @pl.when(i % 2 == 0)
    def _():
        step(h0_scr, h1_scr)

    @pl.when(i % 2 == 1)
    def _():
        step(h1_scr, h0_scr)

    @pl.when(f == pl.num_programs(1) - 1)
    def _():
        o_ref[...] = x_ref[...] + (0.5 * m_ref[0, base + 2:base + 3, :]) * acc_scr[...]


def ffn(x, mod, norm_g, w_gate, w_up, w_down, widx, base):
    m, d = x.shape
    nf = FFN_PAD // FFN_TILE
    tm = FFN_TOKEN_TILE
    nsplit = max(s for s in (1, 2, 4, 8) if s <= nf)
    xn_spec, mn_spec = _lookahead_specs(tm, d, nsplit, m // tm - 1)
    return pl.pallas_call(
        functools.partial(_ffn_kernel, base=base, nsplit=nsplit),
        grid=(m // tm, nf),
        in_specs=[pl.BlockSpec((tm, d), lambda i, f: (i, 0)), xn_spec,
                  pl.BlockSpec((1, N_MOD, d), lambda i, f: (_cond_row(i, tm), 0, 0)), mn_spec,
                  pl.BlockSpec((1, d), lambda i, f: (0, 0)),
                  pl.BlockSpec((None, d, FFN_TILE), lambda i, f: (widx, 0, f)),
                  pl.BlockSpec((None, d, FFN_TILE), lambda i, f: (widx, 0, f)),
                  pl.BlockSpec((None, FFN_TILE, d), lambda i, f: (widx, f, 0))],
        out_specs=pl.BlockSpec((tm, d), lambda i, f: (i, 0)),
        out_shape=jax.ShapeDtypeStruct((m, d), _f32),
        scratch_shapes=[pltpu.VMEM((tm, d), _bf16), pltpu.VMEM((tm, d), _bf16), pltpu.VMEM((tm, d), _f32)],
        compiler_params=_params(("arbitrary", "arbitrary")),
        name="ffn",
    )(x, x, mod, mod, norm_g.reshape(1, d), w_gate, w_up, w_down)


def _inproj_kernel(x_ref, xn_ref, m_ref, mn_ref, g_ref, w_ref, o_ref, h0_scr, h1_scr, *, nsplit):
    i, j = pl.program_id(0), pl.program_id(1)

    @pl.when((i == 0) & (j == 0))
    def _():
        h0_scr[...] = _bf(_modulated_norm(x_ref[...], g_ref[...], m_ref, 3))

    rows_per = xn_ref.shape[0]
    rr = pl.ds(pl.multiple_of(jnp.minimum(j, nsplit - 1) * rows_per, rows_per), rows_per)

    def step(h_cur, h_nxt):
        o_ref[...] = _dot(h_cur[...], w_ref[...])
        h_nxt[rr, :] = _bf(_modulated_norm(xn_ref[...], g_ref[...], mn_ref, 3))

    @pl.when(i % 2 == 0)
    def _():
        step(h0_scr, h1_scr)

    @pl.when(i % 2 == 1)
    def _():
        step(h1_scr, h0_scr)


def _lookahead_specs(tm, d, nsplit, last):
    rows = tm // nsplit
    return (pl.BlockSpec((rows, d), lambda i, s: (jnp.minimum(i + 1, last) * nsplit + jnp.minimum(s, nsplit - 1), 0)),
            pl.BlockSpec((1, N_MOD, d), lambda i, s: (_cond_row(jnp.minimum(i + 1, last), tm), 0, 0)))


def in_proj(x, mod, norm_g, w_in, layer):
    m, d = x.shape
    tm = INPROJ_TOKEN_TILE
    nj = Z_COLS // Z_TILE
    nsplit = max(s for s in (1, 2, 4, 8) if s <= nj)
    xn_spec, mn_spec = _lookahead_specs(tm, d, nsplit, m // tm - 1)
    return pl.pallas_call(
        functools.partial(_inproj_kernel, nsplit=nsplit),
        grid=(m // tm, nj),
        in_specs=[pl.BlockSpec((tm, d), lambda i, j: (0, 0)), xn_spec,
                  pl.BlockSpec((1, N_MOD, d), lambda i, j: (_cond_row(0, tm), 0, 0)), mn_spec,
                  pl.BlockSpec((1, d), lambda i, j: (0, 0)),
                  pl.BlockSpec((None, d, Z_TILE), lambda i, j: (layer, 0, j))],
        out_specs=pl.BlockSpec((tm, Z_TILE), lambda i, j: (i, j)),
        out_shape=jax.ShapeDtypeStruct((m, Z_COLS), _f32),
        scratch_shapes=[pltpu.VMEM((tm, d), _bf16), pltpu.VMEM((tm, d), _bf16)],
        compiler_params=_params(("arbitrary", "arbitrary")),
        name="in_proj",
    )(x, x, mod, mod, norm_g.reshape(1, d), w_in)


def _outproj_kernel(x_ref, m_ref, a_ref, b_ref, c_ref, d_ref, w_ref, o_ref):
    gw = GROUP_WIDTH
    acc = _dot(a_ref[...], w_ref[0:gw, :])
    acc += _dot(b_ref[...], w_ref[gw:2 * gw, :])
    acc += _dot(c_ref[...], w_ref[2 * gw:3 * gw, :])
    acc += _dot(d_ref[...], w_ref[3 * gw:4 * gw, :])
    o_ref[...] = x_ref[...] + m_ref[0, 5:6, :] * acc


def out_proj(x, mod, o_na, o_gla, o_gqa, o_gdn, w_out, layer):
    m, d = x.shape
    tm = TOKEN_TILE
    grp = pl.BlockSpec((tm, GROUP_WIDTH), lambda i: (i, 0))
    return pl.pallas_call(
        _outproj_kernel,
        grid=(m // tm,),
        in_specs=[pl.BlockSpec((tm, d), lambda i: (i, 0)),
                  pl.BlockSpec((1, N_MOD, d), lambda i: (_cond_row(i, tm), 0, 0)),
                  grp, grp, grp, grp,
                  pl.BlockSpec((None, 4 * GROUP_WIDTH, d), lambda i: (layer, 0, 0))],
        out_specs=pl.BlockSpec((tm, d), lambda i: (i, 0)),
        out_shape=jax.ShapeDtypeStruct((m, d), _f32),
        compiler_params=_params(("parallel",)),
        name="out_proj",
    )(x, mod, o_na, o_gla, o_gqa, o_gdn, w_out)


def _softmax_pv(scores, values):
    mx = scores[0].max(axis=-1, keepdims=True)
    for s in scores[1:]:
        mx = jnp.maximum(mx, s.max(axis=-1, keepdims=True))
    num, den = None, None
    for s, v in zip(scores, values):
        p = jnp.exp(s - mx)
        d_ = p.sum(axis=-1, keepdims=True)
        n_ = _dot(_bf(p), v)
        num = n_ if num is None else num + n_
        den = d_ if den is None else den + d_
    return num / den


def _rope(x, cos, sin):
    lane = lax.broadcasted_iota(jnp.int32, x.shape, 1)
    quarter = HEAD_DIM // 4
    partner = jnp.where((lane % (2 * quarter)) < quarter,
                        pltpu.roll(x, HEAD_DIM - quarter, 1), pltpu.roll(x, quarter, 1))
    return x * cos + partner * sin


def _ctx_attn_kernel(nq_ref, nk_ref, nv_ref, gq_ref, gk_ref, gv_ref, nw_ref, gw_ref, *refs, n_alias):
    ona_ref, ogqa_ref, kn_ref, vn_ref, ka_ref, va_ref = refs[n_alias:]
    hd = HEAD_DIM
    scale = hd ** -0.5
    for h in range(NA_HEADS):
        cols = slice(h * hd, (h + 1) * hd)
        k = _rms(nk_ref[:, cols], nw_ref[1:2, :])
        v = nv_ref[:, cols]
        q = _rms(nq_ref[:, cols], nw_ref[0:1, :])
        kn_ref[0, h] = k
        vn_ref[0, h] = v
        s = _dot_nt(_bf(q), _bf(k)) * scale
        ona_ref[:, cols] = _bf(_softmax_pv([s], [_bf(v)]))
    group = GQA_HEADS // GQA_KV_HEADS
    for kv in range(GQA_KV_HEADS):
        cols = slice(kv * hd, (kv + 1) * hd)
        k = _rms(gk_ref[:, cols], gw_ref[1:2, :])
        v = gv_ref[:, cols]
        ka_ref[0, kv] = k
        va_ref[0, kv] = v
        for g in range(group):
            qcols = slice((kv * group + g) * hd, (kv * group + g + 1) * hd)
            q = _rms(gq_ref[:, qcols], gw_ref[0:1, :])
            s = _dot_nt(_bf(q), _bf(k)) * scale
            ogqa_ref[:, qcols] = _bf(_softmax_pv([s], [_bf(v)]))


def ctx_attention(z, na_w, gqa_w, layer, caches):
    t, hd = SEQ, HEAD_DIM

    def zspec(name, width):
        blk = Z_OFF[name] // width
        assert Z_OFF[name] % width == 0
        return pl.BlockSpec((t, width), lambda b: (b, blk))

    def cache_spec(heads):
        return pl.BlockSpec((1, None, heads, t, hd), lambda b: (b, layer, 0, 0, 0))

    def cache_shape(heads):
        return jax.ShapeDtypeStruct((BATCH, DEPTH, heads, t, hd), _f32)

    gw = GROUP_WIDTH
    kvw = GQA_KV_HEADS * hd
    in_specs = [zspec('na_q', gw), zspec('na_k', gw), zspec('na_v', gw),
                zspec('gqa_q', gw), zspec('gqa_k', kvw), zspec('gqa_v', kvw),
                pl.BlockSpec((2, hd), lambda b: (0, 0)), pl.BlockSpec((2, hd), lambda b: (0, 0))]
    args = [z, z, z, z, z, z, na_w, gqa_w]
    aliases = {}
    if caches is not None:
        aliases = {len(args) + i: 2 + i for i in range(4)}
        in_specs += [pl.BlockSpec(memory_space=pl.ANY)] * 4
        args += list(caches)
    outs = pl.pallas_call(
        functools.partial(_ctx_attn_kernel, n_alias=len(aliases)),
        grid=(BATCH,),
        in_specs=in_specs,
        out_specs=[pl.BlockSpec((t, gw), lambda b: (b, 0)), pl.BlockSpec((t, gw), lambda b: (b, 0)),
                   cache_spec(NA_HEADS), cache_spec(NA_HEADS), cache_spec(GQA_KV_HEADS), cache_spec(GQA_KV_HEADS)],
        out_shape=[jax.ShapeDtypeStruct((z.shape[0], gw), _bf16), jax.ShapeDtypeStruct((z.shape[0], gw), _bf16),
                   cache_shape(NA_HEADS), cache_shape(NA_HEADS), cache_shape(GQA_KV_HEADS),
                   cache_shape(GQA_KV_HEADS)],
        input_output_aliases=aliases,
        compiler_params=_params(("parallel",)),
        name="ctx_attention",
    )(*args)
    return outs[0], outs[1], tuple(outs[2:])


NA_QROWS = 4
NA_KROWS = NA_QROWS + NA_WIN_ROWS
GRID_ROWS = DEC_SEQ // GRID_W


def _na_key_row0(blk):
    return min(max(blk * NA_QROWS - NA_WIN_ROWS // 2, 0), GRID_ROWS - NA_KROWS)


def _na_bias_kernel(rpb_ref, o_ref):
    l, h = pl.program_id(0), pl.program_id(1)
    n_dr, n_dc = 2 * NA_WIN_ROWS - 1, 2 * NA_WIN_COLS - 1
    base = (l * NA_HEADS + h) * (n_dr * n_dc)
    qc = lax.broadcasted_iota(jnp.int32, (GRID_W, GRID_W), 0)
    kc = lax.broadcasted_iota(jnp.int32, (GRID_W, GRID_W), 1)
    dc = jnp.clip(kc - qc + (NA_WIN_COLS - 1), 0, n_dc - 1)
    c0 = jnp.clip(qc - NA_WIN_COLS // 2, 0, GRID_W - NA_WIN_COLS)
    in_win = (kc >= c0) & (kc < c0 + NA_WIN_COLS)
    masked = jnp.full((GRID_W, GRID_W), NEG_INF, _f32)
    tiles = []
    for dr in range(n_dr):
        t = jnp.zeros((GRID_W, GRID_W), _f32)
        for j in range(n_dc):
            t = jnp.where(dc == j, rpb_ref[base + dr * n_dc + j], t)
        tiles.append(jnp.where(in_win, t, NEG_INF))
    for blk in range(GRID_ROWS // NA_QROWS):
        k0 = _na_key_row0(blk)
        for qi in range(NA_QROWS):
            r = blk * NA_QROWS + qi
            krow0 = min(max(r - NA_WIN_ROWS // 2, 0), GRID_ROWS - NA_WIN_ROWS)
            for kj in range(NA_KROWS):
                kr = k0 + kj
                ok = krow0 <= kr < krow0 + NA_WIN_ROWS
                tile = tiles[kr - r + NA_WIN_ROWS - 1] if ok else masked
                o_ref[0, 0, blk, qi * GRID_W:(qi + 1) * GRID_W, kj * GRID_W:(kj + 1) * GRID_W] = tile


def na_bias_tables(na_rpb):
    nblk = GRID_ROWS // NA_QROWS
    shape = (DEPTH, NA_HEADS, nblk, NA_QROWS * GRID_W, NA_KROWS * GRID_W)
    return pl.pallas_call(
        _na_bias_kernel,
        grid=(DEPTH, NA_HEADS),
        in_specs=[pl.BlockSpec(memory_space=pltpu.SMEM)],
        out_specs=pl.BlockSpec((1, 1) + shape[2:], lambda l, h: (l, h, 0, 0, 0)),
        out_shape=jax.ShapeDtypeStruct(shape, _f32),
        compiler_params=_params(("parallel", "parallel")),
        name="na_bias",
    )(na_rpb.reshape(-1))


def _na_lat_kernel(q_ref, k_ref, v_ref, ck_ref, cv_ref, bias_ref, nw_ref, _prev_ref, o_ref, kn_scr, vb_scr):
    blk = pl.program_id(2)
    scale = HEAD_DIM ** -0.5

    @pl.when(blk == 0)
    def _():
        kn_scr[...] = _bf(_rms(k_ref[...], nw_ref[1:2, :]))
        vb_scr[...] = _bf(v_ref[...])

    nblk = GRID_ROWS // NA_QROWS
    row0 = jnp.int32(_na_key_row0(0))
    for b_ in range(1, nblk):
        row0 = jnp.where(blk == b_, _na_key_row0(b_), row0)
    start = pl.multiple_of(row0 * GRID_W, GRID_W)
    nkeys = NA_KROWS * GRID_W
    q = _bf(_rms(q_ref[...], nw_ref[0:1, :]))
    s_ctx = _dot_nt(q, _bf(ck_ref[0, 0, 0])) * scale
    s_loc = _dot_nt(q, kn_scr[pl.ds(start, nkeys), :]) * scale + bias_ref[0, 0, 0]
    o = _softmax_pv([s_ctx, s_loc], [_bf(cv_ref[0, 0, 0]), vb_scr[pl.ds(start, nkeys), :]])
    o_ref[...] = _bf(o)


def na_latent(z, cache_k, cache_v, bias, na_w, layer, prev):
    t, hd = DEC_SEQ, HEAD_DIM
    tq = NA_QROWS * GRID_W
    nblk = t // tq
    ctx_tiles = (BATCH * SEQ) // tq
    ctx_seqs = (BATCH * SEQ) // t
    qb, kb, vb = Z_OFF['na_q'] // hd, Z_OFF['na_k'] // hd, Z_OFF['na_v'] // hd
    cache_spec = pl.BlockSpec((1, 1, 1, PAST_LEN, hd), lambda b, h, r: (b, layer, h, 0, 0))
    return pl.pallas_call(
        _na_lat_kernel,
        grid=(DEC_BATCH, NA_HEADS, nblk),
        in_specs=[pl.BlockSpec((tq, hd), lambda b, h, r: (ctx_tiles + b * nblk + r, qb + h)),
                  pl.BlockSpec((t, hd), lambda b, h, r: (ctx_seqs + b, kb + h)),
                  pl.BlockSpec((t, hd), lambda b, h, r: (ctx_seqs + b, vb + h)),
                  cache_spec, cache_spec,
                  pl.BlockSpec((1, 1, 1, tq, NA_KROWS * GRID_W), lambda b, h, r: (layer, h, r, 0, 0)),
                  pl.BlockSpec((2, hd), lambda b, h, r: (0, 0)),
                  pl.BlockSpec(memory_space=pl.ANY)],
        out_specs=pl.BlockSpec((tq, hd), lambda b, h, r: (ctx_tiles + b * nblk + r, h)),
        out_shape=jax.ShapeDtypeStruct(prev.shape, prev.dtype),
        input_output_aliases={7: 0},
        scratch_shapes=[pltpu.VMEM((t, hd), _bf16), pltpu.VMEM((t, hd), _bf16)],
        compiler_params=_params(("parallel", "parallel", "arbitrary")),
        name="na_latent",
    )(z, z, z, cache_k, cache_v, bias, na_w, prev)


GQA_TQ = 256


def _gqa_lat_kernel(q_ref, k_ref, v_ref, ck_ref, cv_ref, cq_ref, sq_ref, ck_all_ref, sk_all_ref, gw_ref,
                    _prev_ref, o_ref, kr_scr, vb_scr):
    hd = HEAD_DIM
    scale = hd ** -0.5

    @pl.when(pl.program_id(2) == 0)
    def _():
        kr_scr[...] = _bf(_rope(_rms(k_ref[...], gw_ref[1:2, :]), ck_all_ref[...], sk_all_ref[...]))
        vb_scr[...] = _bf(v_ref[...])

    ck = _bf(ck_ref[0, 0, 0])
    cv = _bf(cv_ref[0, 0, 0])
    for g in range(GQA_HEADS // GQA_KV_HEADS):
        cols = slice(g * hd, (g + 1) * hd)
        q = _bf(_rope(_rms(q_ref[:, cols], gw_ref[0:1, :]), cq_ref[...], sq_ref[...]))
        s_ctx = _dot_nt(q, ck) * scale
        s_loc = _dot_nt(q, kr_scr[...]) * scale
        o_ref[:, cols] = _bf(_softmax_pv([s_ctx, s_loc], [cv, vb_scr[...]]))


def gqa_latent(z, cache_k, cache_v, cos, sin, gqa_w, layer, prev):
    t, hd = DEC_SEQ, HEAD_DIM
    tq = GQA_TQ
    nblk = t // tq
    group = GQA_HEADS // GQA_KV_HEADS
    ctx_tiles = (BATCH * SEQ) // tq
    ctx_seqs = (BATCH * SEQ) // t
    qb = Z_OFF['gqa_q'] // (group * hd)
    kb, vb = Z_OFF['gqa_k'] // hd, Z_OFF['gqa_v'] // hd
    cache_spec = pl.BlockSpec((1, 1, 1, PAST_LEN, hd), lambda b, h, r: (b, layer, h, 0, 0))
    return pl.pallas_call(
        _gqa_lat_kernel,
        grid=(DEC_BATCH, GQA_KV_HEADS, nblk),
        in_specs=[pl.BlockSpec((tq, group * hd), lambda b, h, r: (ctx_tiles + b * nblk + r, qb + h)),
                  pl.BlockSpec((t, hd), lambda b, h, r: (ctx_seqs + b, kb + h)),
                  pl.BlockSpec((t, hd), lambda b, h, r: (ctx_seqs + b, vb + h)),
                  cache_spec, cache_spec,
                  pl.BlockSpec((tq, hd), lambda b, h, r: (r, 0)),
                  pl.BlockSpec((tq, hd), lambda b, h, r: (r, 0)),
                  pl.BlockSpec((t, hd), lambda b, h, r: (0, 0)),
                  pl.BlockSpec((t, hd), lambda b, h, r: (0, 0)),
                  pl.BlockSpec((2, hd), lambda b, h, r: (0, 0)),
                  pl.BlockSpec(memory_space=pl.ANY)],
        out_specs=pl.BlockSpec((tq, group * hd), lambda b, h, r: (ctx_tiles + b * nblk + r, h)),
        out_shape=jax.ShapeDtypeStruct(prev.shape, prev.dtype),
        input_output_aliases={10: 0},
        scratch_shapes=[pltpu.VMEM((t, hd), _bf16), pltpu.VMEM((t, hd), _bf16)],
        compiler_params=_params(("parallel", "parallel", "arbitrary")),
        name="gqa_latent",
    )(z, z, z, cache_k, cache_v, cos, sin, cos, sin, gqa_w, prev)


def _rope_tables():
    t = np.arange(DEC_SEQ)
    row = (t // GRID_W).astype(np.float32)
    col = (t % GRID_W).astype(np.float32)
    half = HEAD_DIM // 2
    inv = jnp.asarray(ROPE_THETA, _f32) ** (-jnp.arange(0, half, 2, dtype=_f32) / half)
    ang_r = jnp.asarray(row)[:, None] * inv[None, :]
    ang_c = jnp.asarray(col)[:, None] * inv[None, :]
    cos = jnp.concatenate([jnp.cos(ang_r), jnp.cos(ang_r), jnp.cos(ang_c), jnp.cos(ang_c)], axis=-1)
    sin = jnp.concatenate([-jnp.sin(ang_r), jnp.sin(ang_r), -jnp.sin(ang_c), jnp.sin(ang_c)], axis=-1)
    return cos, sin


def _chunk_constants():
    c = CHUNK
    t = np.arange(c)
    tri, a_cat, pair, mq, mk, causal, strict = [], [], [], [], [], [], []
    for d in range(2):
        tau = t if d == 0 else c - 1 - t
        incl = (tau[None, :] <= tau[:, None]).astype(np.float32)
        tri.append(incl)
        causal.append(incl)
        strict.append((tau[None, :] < tau[:, None]).astype(np.float32))
        rows, pm, qm, km = [], [], [], []
        for li in range(N_LEVELS):
            s = c >> (li + 1)
            ref_tau = 2 * s * (tau // (2 * s)) + s - 1
            sel = (tau[None, :] == ref_tau[:, None]).astype(np.float32)
            rows.append(incl - sel @ incl)
            odd = ((tau // s) % 2 == 1).astype(np.float32)
            pm.append(((tau[:, None] // (2 * s)) == (tau[None, :] // (2 * s))).astype(np.float32))
            qm.append(np.repeat(odd[:, None], GLA_DK, axis=1))
            km.append(np.repeat((1.0 - odd)[:, None], GLA_DK, axis=1))
        pm.append(np.eye(c, dtype=np.float32))
        rows.append(incl)
        rows.append((tau[None, :] > tau[:, None]).astype(np.float32))
        a = np.concatenate(rows, axis=0)
        a_cat.append(np.concatenate([a, a, a], axis=1))
        pair.append(np.stack(pm))
        mq.append(np.stack(qm))
        mk.append(np.stack(km))
    tri_cat = np.stack([np.concatenate([x, x, x], axis=1) for x in tri])
    sub = np.stack([pair[d][:N_LEVELS] * mq[d][:, :, :1] * np.swapaxes(mk[d][:, :, :1], 1, 2) for d in range(2)])
    return dict(sub=jnp.asarray(sub), sub_b=jnp.asarray(sub, _bf16),
                a_cat=jnp.asarray(np.stack(a_cat), _bf16), pair=jnp.asarray(np.stack(pair)),
                mq=jnp.asarray(np.stack(mq)), mk=jnp.asarray(np.stack(mk)),
                tri_cat=jnp.asarray(tri_cat, _bf16),
                causal=jnp.asarray(np.stack([np.stack([causal[d], strict[d]]) for d in range(2)])))


def _chunk_pos(d, c, n):
    return c if d == 0 else n - 1 - c


def _gla_kernel(q_ref, k_ref, v_ref, r_ref, zs_ref, wg_ref, gb_ref, acat_ref, pair_ref, mq_ref, mk_ref,
                s0_ref, nw_ref, _prev_ref, o_ref, sn_ref, la_scr, oacc_scr, s_scr, *, t):
    c, dk, dv, nh = CHUNK, GLA_DK, GLA_DV, GLA_HEADS
    n = t // c
    zs = _bf(zs_ref[...])
    for d in range(2):
        x = _dot(zs, wg_ref[d]) + gb_ref[d]
        la_scr[d] = (jnp.minimum(x, 0.0) - jnp.log(1.0 + jnp.exp(-jnp.abs(x)))) * (1.0 / GLA_TAU)
        for h in range(nh):
            s_scr[d * nh + h] = s0_ref[0, d, h]
    ones = jnp.ones((3 * c, LANES), _bf16)
    chains = [(d, h) for d in range(2) for h in range(nh)]

    def chunk(ci, carry):
        rows = [pl.ds(pl.multiple_of(_chunk_pos(d, ci, n) * c, c), c) for d in range(2)]
        states = [s_scr[d * nh + h] for d, h in chains]
        dall, blast = [], []
        for d in range(2):
            gcat = jnp.concatenate(_split3(la_scr[d, rows[d], :]), axis=0)
            dall.append(_dot(acat_ref[d], gcat))
            blast.append(_dot_tn(gcat, ones))
        qs, ks, vs = [], [], []
        for d, h in chains:
            kc = slice(h * dk, (h + 1) * dk)
            qs.append(q_ref[rows[d], kc] * (dk ** -0.5))
            ks.append(k_ref[rows[d], kc])
            vs.append(_bf(v_ref[rows[d], h * dv:(h + 1) * dv]))
        atts = [_dot_nt(_bf(q), _bf(k)) * pair_ref[d, N_LEVELS] for (d, h), q, k in zip(chains, qs, ks)]
        for li in range(N_LEVELS):
            for j, (d, h) in enumerate(chains):
                f = jnp.exp(-jnp.abs(dall[d][li * c:(li + 1) * c, h * dk:(h + 1) * dk]))
                atts[j] += _dot_nt(_bf(qs[j] * f * mq_ref[d, li]), _bf(ks[j] * f * mk_ref[d, li])) * pair_ref[d, li]
        outs, new_states = [], []
        for j, (d, h) in enumerate(chains):
            kc = slice(h * dk, (h + 1) * dk)
            eb = jnp.exp(dall[d][N_LEVELS * c:(N_LEVELS + 1) * c, kc])
            el = jnp.exp(dall[d][(N_LEVELS + 1) * c:(N_LEVELS + 2) * c, kc])
            outs.append(_dot(_bf(qs[j] * eb), _bf(states[j])) + _dot(_bf(atts[j]), vs[j]))
            new_states.append(states[j] * jnp.exp(blast[d][kc, :]) + _dot_tn(_bf(ks[j] * el), vs[j]))
        for j, (d, h) in enumerate(chains):
            s_scr[d * nh + h] = new_states[j]
            oacc_scr[d, rows[d], h * dv:(h + 1) * dv] = outs[j]
        return carry

    lax.fori_loop(0, n, chunk, 0)

    for d, h in chains:
        sn_ref[0, d, h] = s_scr[d * nh + h]
    for h in range(nh):
        vc = slice(h * dv, (h + 1) * dv)
        o_ref[:, vc] = _bf(_rms(oacc_scr[0, :, vc] + oacc_scr[1, :, vc], nw_ref[...]) * _silu(r_ref[:, vc]))


def gla(z, row_blk0, nseq, t, wg, gbias, consts, s0, out_norm, prev=None):
    qw = GLA_HEADS * GLA_DK
    vw = GLA_HEADS * GLA_DV
    if prev is None:
        prev = jnp.zeros((z.shape[0], vw), _bf16)

    def zspec(name, width):
        blk = Z_OFF[name] // width
        assert Z_OFF[name] % width == 0
        return pl.BlockSpec((t, width), lambda b: (row_blk0 + b, blk))

    def full(a):
        return pl.BlockSpec(a.shape, lambda b, nd=a.ndim: (0,) * nd)

    state_spec = pl.BlockSpec((1, 2, GLA_HEADS, GLA_DK, GLA_DV), lambda b: (b, 0, 0, 0, 0))
    cs = [consts['a_cat'], consts['pair'], consts['mq'], consts['mk']]
    return pl.pallas_call(
        functools.partial(_gla_kernel, t=t),
        grid=(nseq,),
        in_specs=[zspec('gla_q', qw), zspec('gla_k', qw), zspec('gla_v', vw), zspec('gla_r', vw),
                  zspec('small', LANES), full(wg), full(gbias)] + [full(a) for a in cs]
                 + [state_spec, pl.BlockSpec((1, GLA_DV), lambda b: (0, 0)), pl.BlockSpec(memory_space=pl.ANY)],
        out_specs=[pl.BlockSpec((t, vw), lambda b: (row_blk0 + b, 0)), state_spec],
        out_shape=[jax.ShapeDtypeStruct(prev.shape, prev.dtype),
                   jax.ShapeDtypeStruct((nseq, 2, GLA_HEADS, GLA_DK, GLA_DV), _f32)],
        input_output_aliases={7 + len(cs) + 2: 0},
        scratch_shapes=[pltpu.VMEM((2, t, qw), _f32), pltpu.VMEM((2, t, vw), _f32),
                        pltpu.VMEM((2 * GLA_HEADS, GLA_DK, GLA_DV), _f32)],
        compiler_params=_params(("parallel",)),
        name="gla",
    )(z, z, z, z, z, wg, gbias, *cs, s0, out_norm.reshape(1, GLA_DV), prev)


CONV_PAD = 8
ROW_BLK = 128
GDN_LOCAL_CHUNKS = 2


def _gdn_kernel(q_ref, k_ref, v_ref, zz_ref, zs_ref, cw_ref, alog_ref, dtb_ref, tri_ref, causal_ref, sub_ref,
                subb_ref, s0_ref, nw_ref, _prev_ref, o_ref, sn_ref,
                xp_scr, qkv_scr, g_scr, beta_scr, b_scr, u_scr, w_scr, att_scr, oacc_scr, s_scr, *, t):
    c, hd, nh = CHUNK, HEAD_DIM, GDN_HEADS
    n = t // c
    w = nh * hd
    half = CONV_K // 2
    nblk = t // ROW_BLK
    eye = (lax.broadcasted_iota(jnp.int32, (c, c), 0) == lax.broadcasted_iota(jnp.int32, (c, c), 1)).astype(_f32)

    xp_scr[0:CONV_PAD, :] = jnp.zeros((CONV_PAD, w), _f32)
    xp_scr[CONV_PAD + t:2 * CONV_PAD + t, :] = jnp.zeros((CONV_PAD, w), _f32)
    for idx, src in enumerate((q_ref, k_ref, v_ref)):
        def copy_in(i, carry, src=src):
            r0 = pl.multiple_of(i * ROW_BLK, ROW_BLK)
            xp_scr[pl.ds(CONV_PAD + r0, ROW_BLK), :] = src[pl.ds(r0, ROW_BLK), :]
            return carry

        lax.fori_loop(0, nblk, copy_in, 0)

        def conv(i, carry, idx=idx):
            r0 = pl.multiple_of(i * ROW_BLK, ROW_BLK)
            win = xp_scr[pl.ds(r0, ROW_BLK + 2 * CONV_PAD), :]
            y = jnp.zeros((ROW_BLK, w), _f32)
            for j in range(CONV_K):
                lo = CONV_PAD + j - half
                y += win[lo:lo + ROW_BLK, :] * cw_ref[idx, j:j + 1, :]
            y = _silu(y)
            for h in range(nh):
                cols = slice(h * hd, (h + 1) * hd)
                yh = y[:, cols]
                if idx == 0:
                    yh = yh * lax.rsqrt(jnp.sum(yh * yh, axis=-1, keepdims=True) + EPS) * (hd ** -0.5)
                elif idx == 1:
                    yh = yh * lax.rsqrt(jnp.sum(yh * yh, axis=-1, keepdims=True) + EPS)
                qkv_scr[idx, pl.ds(r0, ROW_BLK), cols] = yh
            return carry

        lax.fori_loop(0, nblk, conv, 0)

    zs = zs_ref[...]
    g_scr[...] = -jnp.exp(alog_ref[...]) * _softplus(zs + dtb_ref[...])
    beta_scr[...] = _sigmoid(zs)
    for d in range(2):
        for h in range(nh):
            s_scr[d * nh + h] = s0_ref[0, d, h]

    chains = [(d, h) for d in range(2) for h in range(nh)]

    def local(ci, carry):
        jobs, rows, b_all, lows, xs, rhss = [], [], [], [], [], []
        for cc in range(GDN_LOCAL_CHUNKS):
            rr = pl.ds(pl.multiple_of((ci * GDN_LOCAL_CHUNKS + cc) * c, c), c)
            rows.append(rr)
            gcat = jnp.concatenate(_split3(g_scr[rr, :]), axis=0)
            beta_all = beta_scr[rr, :]
            b_cc = [_dot(tri_ref[d], gcat) for d in range(2)]
            b_all.append(b_cc)
            b_t = [b.T for b in b_cc]
            qk, kk, ks, vs = [], [], [], []
            for h in range(nh):
                cols = slice(h * hd, (h + 1) * hd)
                k = qkv_scr[1, rr, cols]
                kbf = _bf(k)
                ks.append(k)
                vs.append(qkv_scr[2, rr, cols])
                kk.append(_dot_nt(kbf, kbf))
                qk.append(_dot_nt(_bf(qkv_scr[0, rr, cols]), kbf))
            for d, h in chains:
                ia = SMALL_A + d * nh + h
                ib = SMALL_BETA + d * nh + h
                bcol = b_cc[d][:, ia:ia + 1]
                decay = jnp.exp(jnp.minimum(bcol - b_t[d][ia:ia + 1, :], 0.0)) * causal_ref[d, 0]
                beta = beta_all[:, ib:ib + 1]
                att_scr[d * nh + h, rr, :] = _bf(qk[h] * decay)
                low = kk[h] * beta * decay * causal_ref[d, 1]
                jobs.append((cc, d, h))
                lows.append(_split2(low))
                xs.append(eye - low * sub_ref[d, N_LEVELS - 1])
                rhss.append(jnp.concatenate([vs[h] * beta, ks[h] * (beta * jnp.exp(bcol))], axis=1))
        for li in range(N_LEVELS - 2, -1, -1):
            for j, (cc, d, h) in enumerate(jobs):
                mask = subb_ref[d, li]
                xp = _split2(xs[j])
                t1 = _dot_parts(xp, (lows[j][0] * mask, lows[j][1] * mask))
                xs[j] = xs[j] - _dot_parts(_split2(t1), xp)
        for j, (cc, d, h) in enumerate(jobs):
            cols = slice(h * hd, (h + 1) * hd)
            sol = _dot_parts(_split2(xs[j]), _split2(rhss[j]))
            u_scr[d, rows[cc], cols] = sol[:, :hd]
            w_scr[d, rows[cc], cols] = _bf(sol[:, hd:])
        for cc in range(GDN_LOCAL_CHUNKS):
            for d in range(2):
                b_scr[d, rows[cc], :] = b_all[cc][d]
        return carry

    lax.fori_loop(0, n // GDN_LOCAL_CHUNKS, local, 0)

    def step(ci, carry):
        rows = [pl.ds(pl.multiple_of(_chunk_pos(d, ci, n) * c, c), c) for d in range(2)]
        states = [s_scr[d * nh + h] for d, h in chains]
        b_all = [b_scr[d, rows[d], :] for d in range(2)]
        us, ws, atts, qs, ks = [], [], [], [], []
        for d, h in chains:
            cols = slice(h * hd, (h + 1) * hd)
            us.append(u_scr[d, rows[d], cols])
            ws.append(w_scr[d, rows[d], cols])
            atts.append(att_scr[d * nh + h, rows[d], :])
            qs.append(qkv_scr[0, rows[d], cols])
            ks.append(qkv_scr[1, rows[d], cols])
        sbs = [_bf(s) for s in states]
        v_new = [_bf(u - _dot(w_, sb)) for u, w_, sb in zip(us, ws, sbs)]
        outs, new_states = [], []
        for j, (d, h) in enumerate(chains):
            ia = SMALL_A + d * nh + h
            last = c - 1 if d == 0 else 0
            bcol = b_all[d][:, ia:ia + 1]
            b_last = b_all[d][last:last + 1, ia:ia + 1]
            outs.append(_dot(_bf(qs[j] * jnp.exp(bcol)), sbs[j]) + _dot(atts[j], v_new[j]))
            new_states.append(states[j] * jnp.exp(b_last) + _dot_tn(_bf(ks[j] * jnp.exp(b_last - bcol)), v_new[j]))
        for j, (d, h) in enumerate(chains):
            s_scr[d * nh + h] = new_states[j]
            oacc_scr[d, rows[d], h * hd:(h + 1) * hd] = outs[j]
        return carry

    lax.fori_loop(0, n, step, 0)

    for d, h in chains:
        sn_ref[0, d, h] = s_scr[d * nh + h]

    def epilogue(i, carry):
        r0 = pl.multiple_of(i * ROW_BLK, ROW_BLK)
        rr = pl.ds(r0, ROW_BLK)
        for h in range(nh):
            cols = slice(h * hd, (h + 1) * hd)
            o = oacc_scr[0, rr, cols] + oacc_scr[1, rr, cols]
            o_ref[rr, cols] = _bf(_rms(o, nw_ref[...]) * _silu(zz_ref[rr, cols]))
        return carry

    lax.fori_loop(0, nblk, epilogue, 0)


def gdn(z, row_blk0, nseq, t, conv_w, alog_lane, dtb_lane, consts, s0, out_norm, prev=None):
    w = GDN_HEADS * HEAD_DIM
    if prev is None:
        prev = jnp.zeros((z.shape[0], w), _bf16)
    qblk = Z_OFF['gdn_qkv'] // w
    assert Z_OFF['gdn_qkv'] % w == 0 and Z_OFF['gdn_z'] % w == 0 and Z_OFF['small'] % LANES == 0

    def zspec(blk, width):
        return pl.BlockSpec((t, width), lambda b: (row_blk0 + b, blk))

    def full(a):
        return pl.BlockSpec(a.shape, lambda b, nd=a.ndim: (0,) * nd)

    state_spec = pl.BlockSpec((1, 2, GDN_HEADS, GDN_DK, GDN_DV), lambda b: (b, 0, 0, 0, 0))
    return pl.pallas_call(
        functools.partial(_gdn_kernel, t=t),
        grid=(nseq,),
        in_specs=[zspec(qblk, w), zspec(qblk + 1, w), zspec(qblk + 2, w), zspec(Z_OFF['gdn_z'] // w, w),
                  zspec(Z_OFF['small'] // LANES, LANES), full(conv_w), full(alog_lane), full(dtb_lane),
                  full(consts['tri_cat']), full(consts['causal']), full(consts['sub']), full(consts['sub_b']),
                  state_spec, pl.BlockSpec((1, GDN_DV), lambda b: (0, 0)), pl.BlockSpec(memory_space=pl.ANY)],
        out_specs=[pl.BlockSpec((t, w), lambda b: (row_blk0 + b, 0)), state_spec],
        out_shape=[jax.ShapeDtypeStruct(prev.shape, prev.dtype),
                   jax.ShapeDtypeStruct((nseq, 2, GDN_HEADS, GDN_DK, GDN_DV), _f32)],
        input_output_aliases={14: 0},
        scratch_shapes=[pltpu.VMEM((t + 2 * CONV_PAD, w), _f32),
                        pltpu.VMEM((3, t, w), _f32),
                        pltpu.VMEM((t, LANES), _f32),
                        pltpu.VMEM((t, LANES), _f32),
                        pltpu.VMEM((2, t, LANES), _f32),
                        pltpu.VMEM((2, t, w), _f32),
                        pltpu.VMEM((2, t, w), _bf16),
                        pltpu.VMEM((2 * GDN_HEADS, t, CHUNK), _bf16),
                        pltpu.VMEM((2, t, w), _f32),
                        pltpu.VMEM((2 * GDN_HEADS, GDN_DK, GDN_DV), _f32)],
        compiler_params=_params(("parallel",)),
        name="gdn",
    )(z, z, z, z, z, conv_w, alog_lane, dtb_lane, consts['tri_cat'], consts['causal'], consts['sub'],
      consts['sub_b'], s0, out_norm.reshape(1, GDN_DV), prev)


def _column_runs():
    runs, start, n = [], 0, len(Z_PERM)
    while start < n:
        stop = start + 1
        while (stop < n and stop - start < CAST_CHUNK
               and (Z_PERM[stop] == Z_PERM[stop - 1] + 1 if Z_PERM[start] >= 0 else Z_PERM[stop] < 0)):
            stop += 1
        runs.append((start, int(Z_PERM[start]), stop - start))
        start = stop
    return runs


def _permute_kernel(w_ref, o_ref):
    for dst, src, width in _column_runs():
        if src < 0:
            o_ref[0, :, dst:dst + width] = jnp.zeros((CAST_BLK, width), _bf16)
        else:
            o_ref[0, :, dst:dst + width] = _bf(w_ref[0, :, src:src + width])


def _permute_columns(w_in):
    depth, d, cols = w_in.shape
    return pl.pallas_call(
        _permute_kernel,
        grid=(depth, d // CAST_BLK),
        in_specs=[pl.BlockSpec((1, CAST_BLK, cols), lambda l, r: (l, r, 0))],
        out_specs=pl.BlockSpec((1, CAST_BLK, Z_COLS), lambda l, r: (l, r, 0)),
        out_shape=jax.ShapeDtypeStruct((depth, d, Z_COLS), _bf16),
        compiler_params=_params(("parallel", "parallel")),
        name="permute_w_in",
    )(w_in)


def _lane_vector(values, offset):
    k = values.shape[-1]
    return jnp.pad(values.astype(_f32), ((0, 0), (offset, LANES - offset - k)))[:, None, :]


def kernel(x_prompt, x_sample, cache_na_k, cache_na_v, cache_gqa_k, cache_gqa_v, state_gla, state_gdn, c, c_ctx, norm_g, w_mod, b_mod, ffn_gu, ffn_down, w_in, w_out, na_qk_norm, na_rpb, gla_gate_up, gla_gate_bias, gla_out_norm, gqa_qk_norm, gdn_conv, gdn_a_log, gdn_dt_bias, gdn_out_norm):
    assert GRID_ROWS % NA_QROWS == 0 and GRID_ROWS >= NA_KROWS and PAST_LEN == SEQ
    d = D_MODEL
    m_ctx = BATCH * SEQ
    assert m_ctx % DEC_SEQ == 0 and DEC_SEQ % TOKEN_TILE == 0
    x = jnp.concatenate([x_prompt.reshape(m_ctx, d), x_sample.reshape(DEC_BATCH * DEC_SEQ, d)], axis=0)

    n_cond = 1 + DEC_BATCH
    cond = jnp.concatenate([c_ctx[None, :], c], axis=0)
    cond = jnp.pad(cond, ((0, (-n_cond) % 8), (0, 0)))
    mod = adaln_all(cond, w_mod, b_mod).reshape(DEPTH, cond.shape[0], N_MOD, d)

    w_gate, w_up, down = cast_ffn_weights(ffn_gu, ffn_down)
    w_in_p = _permute_columns(w_in)
    w_out_b = w_out.astype(_bf16)

    consts = _chunk_constants()
    cos, sin = _rope_tables()
    bias = na_bias_tables(na_rpb)

    qw = GLA_HEADS * GLA_DK
    wg = jnp.zeros((DEPTH, 2, LANES, qw), _f32)
    wg = wg.at[:, 0, 0:GLA_GATE_RANK].set(gla_gate_up[:, 0])
    wg = wg.at[:, 1, SMALL_GB:SMALL_GB + GLA_GATE_RANK].set(gla_gate_up[:, 1]).astype(_bf16)
    gbias = gla_gate_bias.reshape(DEPTH, 2, 1, qw)
    conv_w = gdn_conv.reshape(DEPTH, CONV_K, 3, GDN_HEADS * HEAD_DIM).transpose(0, 2, 1, 3)
    conv_w = jnp.pad(conv_w, ((0, 0), (0, 0), (0, 8 - CONV_K), (0, 0)))
    alog_lane = _lane_vector(gdn_a_log.reshape(DEPTH, 2 * GDN_HEADS), SMALL_A)
    dtb_lane = _lane_vector(gdn_dt_bias.reshape(DEPTH, 2 * GDN_HEADS), SMALL_A)
    gla_zero = jnp.zeros((BATCH, 2, GLA_HEADS, GLA_DK, GLA_DV), _f32)
    gdn_zero = jnp.zeros((BATCH, 2, GDN_HEADS, GDN_DK, GDN_DV), _f32)
    lat_blk0 = m_ctx // DEC_SEQ

    gla_l, gdn_l = [], []
    caches = None
    for l in range(DEPTH):
        x = ffn(x, mod[l], norm_g[l, 0], w_gate, w_up, down, 2 * l, 0)
        z = in_proj(x, mod[l], norm_g[l, 1], w_in_p, l)

        o_na, o_gqa, caches = ctx_attention(z, na_qk_norm[l], gqa_qk_norm[l], l, caches)
        o_na = na_latent(z, cache_na_k, cache_na_v, bias, na_qk_norm[l], l, o_na)
        o_gqa = gqa_latent(z, cache_gqa_k, cache_gqa_v, cos, sin, gqa_qk_norm[l], l, o_gqa)
        o_gla, sg = gla(z, 0, BATCH, SEQ, wg[l], gbias[l], consts, gla_zero, gla_out_norm[l])
        o_gla, _ = gla(z, lat_blk0, DEC_BATCH, DEC_SEQ, wg[l], gbias[l], consts, state_gla[:, l],
                       gla_out_norm[l], o_gla)
        o_gdn, sd = gdn(z, 0, BATCH, SEQ, conv_w[l], alog_lane[l], dtb_lane[l], consts, gdn_zero,
                        gdn_out_norm[l])
        o_gdn, _ = gdn(z, lat_blk0, DEC_BATCH, DEC_SEQ, conv_w[l], alog_lane[l], dtb_lane[l], consts,
                       state_gdn[:, l], gdn_out_norm[l], o_gdn)

        x = out_proj(x, mod[l], o_na, o_gla, o_gqa, o_gdn, w_out_b, l)
        x = ffn(x, mod[l], norm_g[l, 2], w_gate, w_up, down, 2 * l + 1, 6)

        gla_l.append(sg)
        gdn_l.append(sd)

    y_prompt = x[:m_ctx].reshape(BATCH, SEQ, d)
    y_sample = x[m_ctx:].reshape(DEC_BATCH, DEC_SEQ, d)
    return (y_prompt, y_sample) + caches + (jnp.stack(gla_l, axis=1), jnp.stack(gdn_l, axis=1))
```

```python
import functools
import math

import jax
import jax.numpy as jnp
import numpy as np
from jax import lax
from jax.experimental import pallas as pl
from jax.experimental.pallas import tpu as pltpu

D_MODEL = 2048
BATCH = 16
SEQ = 256
DEPTH = 4
DEC_BATCH = 8
DEC_SEQ = 1024
PAST_LEN = 256

GRID_W = 64
HEAD_DIM = 128
GROUP_WIDTH = D_MODEL // 4
NA_HEADS = GROUP_WIDTH // HEAD_DIM
NA_WIN_ROWS = 8
NA_WIN_COLS = 16
GLA_HEADS = 4
GLA_DV = GROUP_WIDTH // GLA_HEADS
GLA_DK = GLA_DV // 2
GLA_GATE_RANK = 16
GLA_TAU = 16.0
GQA_HEADS = GROUP_WIDTH // HEAD_DIM
GQA_KV_HEADS = GQA_HEADS // 2
GDN_HEADS = GROUP_WIDTH // HEAD_DIM
GDN_DK = HEAD_DIM
GDN_DV = HEAD_DIM
CONV_K = 5
CHUNK = 64
ROPE_THETA = 10000.0
FFN_DIM = ((8 * D_MODEL // 3 + 127) // 128) * 128
N_MOD = 9
EPS = 1e-6
NEG_INF = -1e30

IN_SPLITS = (
    ('na_q', NA_HEADS * HEAD_DIM), ('na_k', NA_HEADS * HEAD_DIM), ('na_v', NA_HEADS * HEAD_DIM),
    ('gla_q', GLA_HEADS * GLA_DK), ('gla_k', GLA_HEADS * GLA_DK), ('gla_v', GLA_HEADS * GLA_DV),
    ('gla_r', GLA_HEADS * GLA_DV), ('gla_gf', GLA_GATE_RANK), ('gla_gb', GLA_GATE_RANK),
    ('gqa_q', GQA_HEADS * HEAD_DIM), ('gqa_k', GQA_KV_HEADS * HEAD_DIM), ('gqa_v', GQA_KV_HEADS * HEAD_DIM),
    ('gdn_qkv', 3 * GDN_HEADS * HEAD_DIM), ('gdn_z', GDN_HEADS * GDN_DV),
    ('gdn_b', 2 * GDN_HEADS), ('gdn_a', 2 * GDN_HEADS),
)

LANES = 128
FFN_TILE = 512
FFN_PAD = ((FFN_DIM + FFN_TILE - 1) // FFN_TILE) * FFN_TILE
TOKEN_TILE = 512
INPROJ_TOKEN_TILE = 1024
FFN_TOKEN_TILE = 512
VMEM_LIMIT = 56 * 1024 * 1024

Z_ORDER = ('na_q', 'na_k', 'na_v', 'gla_q', 'gla_k', 'gla_v', 'gla_r', 'gqa_q', 'gqa_k', 'gqa_v',
           'gdn_qkv', 'gdn_z')
SMALL_ORDER = ('gla_gf', 'gla_gb', 'gdn_b', 'gdn_a')
SMALL_GB = GLA_GATE_RANK
SMALL_BETA = 2 * GLA_GATE_RANK
SMALL_A = SMALL_BETA + 2 * GDN_HEADS
N_LEVELS = int(math.log2(CHUNK))


def _layout():
    src, off = {}, 0
    for name, width in IN_SPLITS:
        src[name] = (off, width)
        off += width
    perm, zoff, pos = [], {}, 0
    for name in Z_ORDER:
        o, w = src[name]
        zoff[name] = pos
        perm.extend(range(o, o + w))
        pos += w
    zoff['small'] = pos
    n_small = 0
    for name in SMALL_ORDER:
        o, w = src[name]
        perm.extend(range(o, o + w))
        n_small += w
    perm.extend([-1] * (LANES - n_small))
    pos += LANES
    perm.extend([-1] * ((-pos) % Z_TILE))
    return np.asarray(perm, np.int32), zoff, len(perm)


Z_TILE = 5 * 256
Z_PERM, Z_OFF, Z_COLS = _layout()

_f32 = jnp.float32
_bf16 = jnp.bfloat16


def _dot(a, b):
    return jnp.dot(a, b, preferred_element_type=_f32)


def _dot_nt(a, b):
    return lax.dot_general(a, b, (((1,), (1,)), ((), ())), preferred_element_type=_f32)


def _dot_tn(a, b):
    return lax.dot_general(a, b, (((0,), (0,)), ((), ())), preferred_element_type=_f32)


def _bf(x):
    return x.astype(_bf16)


def _sigmoid(x):
    return 1.0 / (1.0 + jnp.exp(-x))


def _silu(x):
    return x * _sigmoid(x)


def _softplus(x):
    return jnp.maximum(x, 0.0) + jnp.log(1.0 + jnp.exp(-jnp.abs(x)))


def _rms(x, w):
    return x * lax.rsqrt(jnp.mean(x * x, axis=-1, keepdims=True) + EPS) * w


def _split3(x):
    hi = _bf(x)
    r1 = x - hi.astype(_f32)
    mid = _bf(r1)
    lo = _bf(r1 - mid.astype(_f32))
    return hi, mid, lo


def _split2(x):
    hi = _bf(x)
    return hi, _bf(x - hi.astype(_f32))


def _dot_parts(a, b):
    return _dot(jnp.concatenate([a[0], a[0], a[1]], axis=1), jnp.concatenate([b[0], b[1], b[0]], axis=0))


def _cond_row(i, tile):
    n_ctx = (BATCH * SEQ) // tile
    return jnp.where(i < n_ctx, 0, 1 + (i - n_ctx) // (DEC_SEQ // tile))


def _params(sem):
    return pltpu.CompilerParams(dimension_semantics=sem, vmem_limit_bytes=VMEM_LIMIT)


def _adaln_kernel(c_ref, w_ref, b_ref, o_ref):
    a = _bf(_silu(c_ref[...]))
    o_ref[0] = _dot(a, _bf(w_ref[0])) + b_ref[0]


def adaln_all(cond, w_mod, b_mod):
    nc = cond.shape[0]
    tn = 1024
    n = N_MOD * D_MODEL
    return pl.pallas_call(
        _adaln_kernel,
        grid=(DEPTH, n // tn),
        in_specs=[pl.BlockSpec((nc, D_MODEL), lambda l, j: (0, 0)),
                  pl.BlockSpec((1, D_MODEL, tn), lambda l, j: (l, 0, j)),
                  pl.BlockSpec((1, 1, tn), lambda l, j: (l, 0, j))],
        out_specs=pl.BlockSpec((1, nc, tn), lambda l, j: (l, 0, j)),
        out_shape=jax.ShapeDtypeStruct((DEPTH, nc, n), _f32),
        compiler_params=_params(("parallel", "parallel")),
        name="adaln",
    )(cond, w_mod, b_mod.reshape(DEPTH, 1, n))


CAST_BLK = 256
CAST_CHUNK = 1024


def _cast_gu_kernel(w_ref, g_ref, u_ref):
    f = FFN_DIM
    for t in range(FFN_PAD // FFN_TILE):
        c0 = t * FFN_TILE
        n_real = max(0, min(FFN_TILE, f - c0))
        if n_real:
            g_ref[0, t, :, :n_real] = _bf(w_ref[0, :, c0:c0 + n_real])
            u_ref[0, t, :, :n_real] = _bf(w_ref[0, :, f + c0:f + c0 + n_real])
        if n_real < FFN_TILE:
            g_ref[0, t, :, n_real:] = jnp.zeros((CAST_BLK, FFN_TILE - n_real), _bf16)
            u_ref[0, t, :, n_real:] = jnp.zeros((CAST_BLK, FFN_TILE - n_real), _bf16)


def _cast_down_kernel(w_ref, o_ref):
    f = FFN_DIM
    for r0 in range(0, f, CAST_CHUNK):
        r1 = min(r0 + CAST_CHUNK, f)
        o_ref[0, r0:r1, :] = _bf(w_ref[0, r0:r1, :])
    if FFN_PAD > f:
        o_ref[0, f:, :] = jnp.zeros((FFN_PAD - f, CAST_BLK), _bf16)


def cast_ffn_weights(ffn_gu, ffn_down):
    d, f = D_MODEL, FFN_DIM
    n = ffn_gu.shape[0] * ffn_gu.shape[1]
    nf = FFN_PAD // FFN_TILE
    gate, up = pl.pallas_call(
        _cast_gu_kernel,
        grid=(n, d // CAST_BLK),
        in_specs=[pl.BlockSpec((1, CAST_BLK, 2 * f), lambda i, r: (i, r, 0))],
        out_specs=[pl.BlockSpec((1, nf, CAST_BLK, FFN_TILE), lambda i, r: (i, 0, r, 0))] * 2,
        out_shape=[jax.ShapeDtypeStruct((n, nf, d, FFN_TILE), _bf16)] * 2,
        compiler_params=_params(("parallel", "parallel")),
        name="cast_gate_up",
    )(ffn_gu.reshape(n, d, 2 * f))
    down = pl.pallas_call(
        _cast_down_kernel,
        grid=(n, d // CAST_BLK),
        in_specs=[pl.BlockSpec((1, f, CAST_BLK), lambda i, r: (i, 0, r))],
        out_specs=pl.BlockSpec((1, FFN_PAD, CAST_BLK), lambda i, r: (i, 0, r)),
        out_shape=jax.ShapeDtypeStruct((n, FFN_PAD, d), _bf16),
        compiler_params=_params(("parallel", "parallel")),
        name="cast_down",
    )(ffn_down.reshape(n, f, d))
    return gate, up, down


def _modulated_norm(x, g, m_ref, base):
    sh = m_ref[0, base:base + 1, :]
    sc = m_ref[0, base + 1:base + 2, :]
    return _rms(x, g) * (1.0 + sc) + sh


def _ffn_kernel(x_ref, xn_ref, m_ref, mn_ref, g_ref, wg_ref, wu_ref, wd_ref, o_ref, h0_scr, h1_scr, acc_scr, *,
                base, nsplit):
    i, f = pl.program_id(0), pl.program_id(1)

    @pl.when((i == 0) & (f == 0))
    def _():
        h0_scr[...] = _bf(_modulated_norm(x_ref[...], g_ref[...], m_ref, base))

    @pl.when(f == 0)
    def _():
        acc_scr[...] = jnp.zeros_like(acc_scr)

    rows_per = xn_ref.shape[0]
    rr = pl.ds(pl.multiple_of(jnp.minimum(f, nsplit - 1) * rows_per, rows_per), rows_per)

    def step(h_cur, h_nxt):
        h = h_cur[...]
        gate = _dot(h, wg_ref[...])
        up = _dot(h, wu_ref[...])
        acc_scr[...] += _dot(_bf(_silu(gate) * up), wd_ref[...])
        h_nxt[rr, :] = _bf(_modulated_norm(xn_ref[...], g_ref[...], mn_ref, base))

    @pl.when(i % 2 == 0)
    def _():
        step(h0_scr, h1_scr)

    @pl.when(i % 2 == 1)
    def _():
        step(h1_scr, h0_scr)

    @pl.when(f == pl.num_programs(1) - 1)
    def _():
        o_ref[...] = x_ref[...] + (0.5 * m_ref[0, base + 2:base + 3, :]) * acc_scr[...]


def ffn(x, mod, norm_g, w_gate, w_up, w_down, widx, base):
    m, d = x.shape
    nf = FFN_PAD // FFN_TILE
    tm = FFN_TOKEN_TILE
    nsplit = max(s for s in (1, 2, 4, 8) if s <= nf)
    xn_spec, mn_spec = _lookahead_specs(tm, d, nsplit, m // tm - 1)
    return pl.pallas_call(
        functools.partial(_ffn_kernel, base=base, nsplit=nsplit),
        grid=(m // tm, nf),
        in_specs=[pl.BlockSpec((tm, d), lambda i, f: (i, 0)), xn_spec,
                  pl.BlockSpec((1, N_MOD, d), lambda i, f: (_cond_row(i, tm), 0, 0)), mn_spec,
                  pl.BlockSpec((1, d), lambda i, f: (0, 0)),
                  pl.BlockSpec((None, None, d, FFN_TILE), lambda i, f: (widx, f, 0, 0)),
                  pl.BlockSpec((None, None, d, FFN_TILE), lambda i, f: (widx, f, 0, 0)),
                  pl.BlockSpec((None, FFN_TILE, d), lambda i, f: (widx, f, 0))],
        out_specs=pl.BlockSpec((tm, d), lambda i, f: (i, 0)),
        out_shape=jax.ShapeDtypeStruct((m, d), _f32),
        scratch_shapes=[pltpu.VMEM((tm, d), _bf16), pltpu.VMEM((tm, d), _bf16), pltpu.VMEM((tm, d), _f32)],
        compiler_params=_params(("arbitrary", "arbitrary")),
        name="ffn",
    )(x, x, mod, mod, norm_g.reshape(1, d), w_gate, w_up, w_down)


def _inproj_kernel(x_ref, xn_ref, m_ref, mn_ref, g_ref, w_ref, o_ref, h0_scr, h1_scr, *, nsplit):
    i, j = pl.program_id(0), pl.program_id(1)

    @pl.when((i == 0) & (j == 0))
    def _():
        h0_scr[...] = _bf(_modulated_norm(x_ref[...], g_ref[...], m_ref, 3))

    rows_per = xn_ref.shape[0]
    rr = pl.ds(pl.multiple_of(jnp.minimum(j, nsplit - 1) * rows_per, rows_per), rows_per)

    def step(h_cur, h_nxt):
        o_ref[...] = _dot(h_cur[...], w_ref[...])
        h_nxt[rr, :] = _bf(_modulated_norm(xn_ref[...], g_ref[...], mn_ref, 3))

    @pl.when(i % 2 == 0)
    def _():
        step(h0_scr, h1_scr)

    @pl.when(i % 2 == 1)
    def _():
        step(h1_scr, h0_scr)


def _lookahead_specs(tm, d, nsplit, last):
    rows = tm // nsplit
    return (pl.BlockSpec((rows, d), lambda i, s: (jnp.minimum(i + 1, last) * nsplit + jnp.minimum(s, nsplit - 1), 0)),
            pl.BlockSpec((1, N_MOD, d), lambda i, s: (_cond_row(jnp.minimum(i + 1, last), tm), 0, 0)))


def in_proj(x, mod, norm_g, w_in, layer):
    m, d = x.shape
    tm = INPROJ_TOKEN_TILE
    nj = Z_COLS // Z_TILE
    nsplit = max(s for s in (1, 2, 4, 8) if s <= nj)
    xn_spec, mn_spec = _lookahead_specs(tm, d, nsplit, m // tm - 1)
    return pl.pallas_call(
        functools.partial(_inproj_kernel, nsplit=nsplit),
        grid=(m // tm, nj),
        in_specs=[pl.BlockSpec((tm, d), lambda i, j: (0, 0)), xn_spec,
                  pl.BlockSpec((1, N_MOD, d), lambda i, j: (_cond_row(0, tm), 0, 0)), mn_spec,
                  pl.BlockSpec((1, d), lambda i, j: (0, 0)),
                  pl.BlockSpec((None, None, d, Z_TILE), lambda i, j: (layer, j, 0, 0))],
        out_specs=pl.BlockSpec((tm, Z_TILE), lambda i, j: (i, j)),
        out_shape=jax.ShapeDtypeStruct((m, Z_COLS), _f32),
        scratch_shapes=[pltpu.VMEM((tm, d), _bf16), pltpu.VMEM((tm, d), _bf16)],
        compiler_params=_params(("arbitrary", "arbitrary")),
        name="in_proj",
    )(x, x, mod, mod, norm_g.reshape(1, d), w_in)


def _outproj_kernel(x_ref, m_ref, a_ref, b_ref, c_ref, d_ref, w_ref, o_ref):
    gw = GROUP_WIDTH
    acc = _dot(a_ref[...], w_ref[0:gw, :])
    acc += _dot(b_ref[...], w_ref[gw:2 * gw, :])
    acc += _dot(c_ref[...], w_ref[2 * gw:3 * gw, :])
    acc += _dot(d_ref[...], w_ref[3 * gw:4 * gw, :])
    o_ref[...] = x_ref[...] + m_ref[0, 5:6, :] * acc


def out_proj(x, mod, o_na, o_gla, o_gqa, o_gdn, w_out, layer):
    m, d = x.shape
    tm = TOKEN_TILE
    grp = pl.BlockSpec((tm, GROUP_WIDTH), lambda i: (i, 0))
    return pl.pallas_call(
        _outproj_kernel,
        grid=(m // tm,),
        in_specs=[pl.BlockSpec((tm, d), lambda i: (i, 0)),
                  pl.BlockSpec((1, N_MOD, d), lambda i: (_cond_row(i, tm), 0, 0)),
                  grp, grp, grp, grp,
                  pl.BlockSpec((None, 4 * GROUP_WIDTH, d), lambda i: (layer, 0, 0))],
        out_specs=pl.BlockSpec((tm, d), lambda i: (i, 0)),
        out_shape=jax.ShapeDtypeStruct((m, d), _f32),
        compiler_params=_params(("parallel",)),
        name="out_proj",
    )(x, mod, o_na, o_gla, o_gqa, o_gdn, w_out)


def _softmax_pv(scores, values):
    mx = scores[0].max(axis=-1, keepdims=True)
    for s in scores[1:]:
        mx = jnp.maximum(mx, s.max(axis=-1, keepdims=True))
    num, den = None, None
    for s, v in zip(scores, values):
        p = jnp.exp(s - mx)
        d_ = p.sum(axis=-1, keepdims=True)
        n_ = _dot(_bf(p), v)
        num = n_ if num is None else num + n_
        den = d_ if den is None else den + d_
    return num / den


def _rope(x, cos, sin):
    lane = lax.broadcasted_iota(jnp.int32, x.shape, 1)
    quarter = HEAD_DIM // 4
    partner = jnp.where((lane % (2 * quarter)) < quarter,
                        pltpu.roll(x, HEAD_DIM - quarter, 1), pltpu.roll(x, quarter, 1))
    return x * cos + partner * sin


def _ctx_attn_kernel(nq_ref, nk_ref, nv_ref, gq_ref, gk_ref, gv_ref, nw_ref, gw_ref, *refs, n_alias):
    ona_ref, ogqa_ref, kn_ref, vn_ref, ka_ref, va_ref = refs[n_alias:]
    hd = HEAD_DIM
    scale = hd ** -0.5
    for h in range(NA_HEADS):
        cols = slice(h * hd, (h + 1) * hd)
        k = _rms(nk_ref[:, cols], nw_ref[1:2, :])
        v = nv_ref[:, cols]
        q = _rms(nq_ref[:, cols], nw_ref[0:1, :])
        kn_ref[0, h] = k
        vn_ref[0, h] = v
        s = _dot_nt(_bf(q), _bf(k)) * scale
        ona_ref[:, cols] = _bf(_softmax_pv([s], [_bf(v)]))
    group = GQA_HEADS // GQA_KV_HEADS
    for kv in range(GQA_KV_HEADS):
        cols = slice(kv * hd, (kv + 1) * hd)
        k = _rms(gk_ref[:, cols], gw_ref[1:2, :])
        v = gv_ref[:, cols]
        ka_ref[0, kv] = k
        va_ref[0, kv] = v
        for g in range(group):
            qcols = slice((kv * group + g) * hd, (kv * group + g + 1) * hd)
            q = _rms(gq_ref[:, qcols], gw_ref[0:1, :])
            s = _dot_nt(_bf(q), _bf(k)) * scale
            ogqa_ref[:, qcols] = _bf(_softmax_pv([s], [_bf(v)]))


def ctx_attention(z, na_w, gqa_w, layer, caches):
    t, hd = SEQ, HEAD_DIM

    def zspec(name, width):
        blk = Z_OFF[name] // width
        assert Z_OFF[name] % width == 0
        return pl.BlockSpec((t, width), lambda b: (b, blk))

    def cache_spec(heads):
        return pl.BlockSpec((1, None, heads, t, hd), lambda b: (b, layer, 0, 0, 0))

    def cache_shape(heads):
        return jax.ShapeDtypeStruct((BATCH, DEPTH, heads, t, hd), _f32)

    gw = GROUP_WIDTH
    kvw = GQA_KV_HEADS * hd
    in_specs = [zspec('na_q', gw), zspec('na_k', gw), zspec('na_v', gw),
                zspec('gqa_q', gw), zspec('gqa_k', kvw), zspec('gqa_v', kvw),
                pl.BlockSpec((2, hd), lambda b: (0, 0)), pl.BlockSpec((2, hd), lambda b: (0, 0))]
    args = [z, z, z, z, z, z, na_w, gqa_w]
    aliases = {}
    if caches is not None:
        aliases = {len(args) + i: 2 + i for i in range(4)}
        in_specs += [pl.BlockSpec(memory_space=pl.ANY)] * 4
        args += list(caches)
    outs = pl.pallas_call(
        functools.partial(_ctx_attn_kernel, n_alias=len(aliases)),
        grid=(BATCH,),
        in_specs=in_specs,
        out_specs=[pl.BlockSpec((t, gw), lambda b: (b, 0)), pl.BlockSpec((t, gw), lambda b: (b, 0)),
                   cache_spec(NA_HEADS), cache_spec(NA_HEADS), cache_spec(GQA_KV_HEADS), cache_spec(GQA_KV_HEADS)],
        out_shape=[jax.ShapeDtypeStruct((z.shape[0], gw), _bf16), jax.ShapeDtypeStruct((z.shape[0], gw), _bf16),
                   cache_shape(NA_HEADS), cache_shape(NA_HEADS), cache_shape(GQA_KV_HEADS),
                   cache_shape(GQA_KV_HEADS)],
        input_output_aliases=aliases,
        compiler_params=_params(("parallel",)),
        name="ctx_attention",
    )(*args)
    return outs[0], outs[1], tuple(outs[2:])


NA_QROWS = 4
NA_KROWS = NA_QROWS + NA_WIN_ROWS
GRID_ROWS = DEC_SEQ // GRID_W


def _na_key_row0(blk):
    return min(max(blk * NA_QROWS - NA_WIN_ROWS // 2, 0), GRID_ROWS - NA_KROWS)


def _na_bias_kernel(rpb_ref, o_ref):
    l, h = pl.program_id(0), pl.program_id(1)
    n_dr, n_dc = 2 * NA_WIN_ROWS - 1, 2 * NA_WIN_COLS - 1
    base = (l * NA_HEADS + h) * (n_dr * n_dc)
    qc = lax.broadcasted_iota(jnp.int32, (GRID_W, GRID_W), 0)
    kc = lax.broadcasted_iota(jnp.int32, (GRID_W, GRID_W), 1)
    dc = jnp.clip(kc - qc + (NA_WIN_COLS - 1), 0, n_dc - 1)
    c0 = jnp.clip(qc - NA_WIN_COLS // 2, 0, GRID_W - NA_WIN_COLS)
    in_win = (kc >= c0) & (kc < c0 + NA_WIN_COLS)
    masked = jnp.full((GRID_W, GRID_W), NEG_INF, _f32)
    tiles = []
    for dr in range(n_dr):
        t = jnp.zeros((GRID_W, GRID_W), _f32)
        for j in range(n_dc):
            t = jnp.where(dc == j, rpb_ref[base + dr * n_dc + j], t)
        tiles.append(jnp.where(in_win, t, NEG_INF))
    for blk in range(GRID_ROWS // NA_QROWS):
        k0 = _na_key_row0(blk)
        for qi in range(NA_QROWS):
            r = blk * NA_QROWS + qi
            krow0 = min(max(r - NA_WIN_ROWS // 2, 0), GRID_ROWS - NA_WIN_ROWS)
            for kj in range(NA_KROWS):
                kr = k0 + kj
                ok = krow0 <= kr < krow0 + NA_WIN_ROWS
                tile = tiles[kr - r + NA_WIN_ROWS - 1] if ok else masked
                o_ref[0, 0, blk, qi * GRID_W:(qi + 1) * GRID_W, kj * GRID_W:(kj + 1) * GRID_W] = tile


def na_bias_tables(na_rpb):
    nblk = GRID_ROWS // NA_QROWS
    shape = (DEPTH, NA_HEADS, nblk, NA_QROWS * GRID_W, NA_KROWS * GRID_W)
    return pl.pallas_call(
        _na_bias_kernel,
        grid=(DEPTH, NA_HEADS),
        in_specs=[pl.BlockSpec(memory_space=pltpu.SMEM)],
        out_specs=pl.BlockSpec((1, 1) + shape[2:], lambda l, h: (l, h, 0, 0, 0)),
        out_shape=jax.ShapeDtypeStruct(shape, _f32),
        compiler_params=_params(("parallel", "parallel")),
        name="na_bias",
    )(na_rpb.reshape(-1))


def _na_lat_kernel(q_ref, k_ref, v_ref, ck_ref, cv_ref, bias_ref, nw_ref, _prev_ref, o_ref, kn_scr, vb_scr):
    blk = pl.program_id(2)
    scale = HEAD_DIM ** -0.5

    @pl.when(blk == 0)
    def _():
        kn_scr[...] = _bf(_rms(k_ref[...], nw_ref[1:2, :]))
        vb_scr[...] = _bf(v_ref[...])

    nblk = GRID_ROWS // NA_QROWS
    row0 = jnp.int32(_na_key_row0(0))
    for b_ in range(1, nblk):
        row0 = jnp.where(blk == b_, _na_key_row0(b_), row0)
    start = pl.multiple_of(row0 * GRID_W, GRID_W)
    nkeys = NA_KROWS * GRID_W
    q = _bf(_rms(q_ref[...], nw_ref[0:1, :]))
    s_ctx = _dot_nt(q, _bf(ck_ref[0, 0, 0])) * scale
    s_loc = _dot_nt(q, kn_scr[pl.ds(start, nkeys), :]) * scale + bias_ref[0, 0, 0]
    o = _softmax_pv([s_ctx, s_loc], [_bf(cv_ref[0, 0, 0]), vb_scr[pl.ds(start, nkeys), :]])
    o_ref[...] = _bf(o)


def na_latent(z, cache_k, cache_v, bias, na_w, layer, prev):
    t, hd = DEC_SEQ, HEAD_DIM
    tq = NA_QROWS * GRID_W
    nblk = t // tq
    ctx_tiles = (BATCH * SEQ) // tq
    ctx_seqs = (BATCH * SEQ) // t
    qb, kb, vb = Z_OFF['na_q'] // hd, Z_OFF['na_k'] // hd, Z_OFF['na_v'] // hd
    cache_spec = pl.BlockSpec((1, 1, 1, PAST_LEN, hd), lambda b, h, r: (b, layer, h, 0, 0))
    return pl.pallas_call(
        _na_lat_kernel,
        grid=(DEC_BATCH, NA_HEADS, nblk),
        in_specs=[pl.BlockSpec((tq, hd), lambda b, h, r: (ctx_tiles + b * nblk + r, qb + h)),
                  pl.BlockSpec((t, hd), lambda b, h, r: (ctx_seqs + b, kb + h)),
                  pl.BlockSpec((t, hd), lambda b, h, r: (ctx_seqs + b, vb + h)),
                  cache_spec, cache_spec,
                  pl.BlockSpec((1, 1, 1, tq, NA_KROWS * GRID_W), lambda b, h, r: (layer, h, r, 0, 0)),
                  pl.BlockSpec((2, hd), lambda b, h, r: (0, 0)),
                  pl.BlockSpec(memory_space=pl.ANY)],
        out_specs=pl.BlockSpec((tq, hd), lambda b, h, r: (ctx_tiles + b * nblk + r, h)),
        out_shape=jax.ShapeDtypeStruct(prev.shape, prev.dtype),
        input_output_aliases={7: 0},
        scratch_shapes=[pltpu.VMEM((t, hd), _bf16), pltpu.VMEM((t, hd), _bf16)],
        compiler_params=_params(("parallel", "parallel", "arbitrary")),
        name="na_latent",
    )(z, z, z, cache_k, cache_v, bias, na_w, prev)


GQA_TQ = 256


def _gqa_lat_kernel(q_ref, k_ref, v_ref, ck_ref, cv_ref, cq_ref, sq_ref, ck_all_ref, sk_all_ref, gw_ref,
                    _prev_ref, o_ref, kr_scr, vb_scr):
    hd = HEAD_DIM
    scale = hd ** -0.5

    @pl.when(pl.program_id(2) == 0)
    def _():
        kr_scr[...] = _bf(_rope(_rms(k_ref[...], gw_ref[1:2, :]), ck_all_ref[...], sk_all_ref[...]))
        vb_scr[...] = _bf(v_ref[...])

    ck = _bf(ck_ref[0, 0, 0])
    cv = _bf(cv_ref[0, 0, 0])
    for g in range(GQA_HEADS // GQA_KV_HEADS):
        cols = slice(g * hd, (g + 1) * hd)
        q = _bf(_rope(_rms(q_ref[:, cols], gw_ref[0:1, :]), cq_ref[...], sq_ref[...]))
        s_ctx = _dot_nt(q, ck) * scale
        s_loc = _dot_nt(q, kr_scr[...]) * scale
        o_ref[:, cols] = _bf(_softmax_pv([s_ctx, s_loc], [cv, vb_scr[...]]))


def gqa_latent(z, cache_k, cache_v, cos, sin, gqa_w, layer, prev):
    t, hd = DEC_SEQ, HEAD_DIM
    tq = GQA_TQ
    nblk = t // tq
    group = GQA_HEADS // GQA_KV_HEADS
    ctx_tiles = (BATCH * SEQ) // tq
    ctx_seqs = (BATCH * SEQ) // t
    qb = Z_OFF['gqa_q'] // (group * hd)
    kb, vb = Z_OFF['gqa_k'] // hd, Z_OFF['gqa_v'] // hd
    cache_spec = pl.BlockSpec((1, 1, 1, PAST_LEN, hd), lambda b, h, r: (b, layer, h, 0, 0))
    return pl.pallas_call(
        _gqa_lat_kernel,
        grid=(DEC_BATCH, GQA_KV_HEADS, nblk),
        in_specs=[pl.BlockSpec((tq, group * hd), lambda b, h, r: (ctx_tiles + b * nblk + r, qb + h)),
                  pl.BlockSpec((t, hd), lambda b, h, r: (ctx_seqs + b, kb + h)),
                  pl.BlockSpec((t, hd), lambda b, h, r: (ctx_seqs + b, vb + h)),
                  cache_spec, cache_spec,
                  pl.BlockSpec((tq, hd), lambda b, h, r: (r, 0)),
                  pl.BlockSpec((tq, hd), lambda b, h, r: (r, 0)),
                  pl.BlockSpec((t, hd), lambda b, h, r: (0, 0)),
                  pl.BlockSpec((t, hd), lambda b, h, r: (0, 0)),
                  pl.BlockSpec((2, hd), lambda b, h, r: (0, 0)),
                  pl.BlockSpec(memory_space=pl.ANY)],
        out_specs=pl.BlockSpec((tq, group * hd), lambda b, h, r: (ctx_tiles + b * nblk + r, h)),
        out_shape=jax.ShapeDtypeStruct(prev.shape, prev.dtype),
        input_output_aliases={10: 0},
        scratch_shapes=[pltpu.VMEM((t, hd), _bf16), pltpu.VMEM((t, hd), _bf16)],
        compiler_params=_params(("parallel", "parallel", "arbitrary")),
        name="gqa_latent",
    )(z, z, z, cache_k, cache_v, cos, sin, cos, sin, gqa_w, prev)


def _rope_tables():
    t = np.arange(DEC_SEQ)
    row = (t // GRID_W).astype(np.float32)
    col = (t % GRID_W).astype(np.float32)
    half = HEAD_DIM // 2
    inv = jnp.asarray(ROPE_THETA, _f32) ** (-jnp.arange(0, half, 2, dtype=_f32) / half)
    ang_r = jnp.asarray(row)[:, None] * inv[None, :]
    ang_c = jnp.asarray(col)[:, None] * inv[None, :]
    cos = jnp.concatenate([jnp.cos(ang_r), jnp.cos(ang_r), jnp.cos(ang_c), jnp.cos(ang_c)], axis=-1)
    sin = jnp.concatenate([-jnp.sin(ang_r), jnp.sin(ang_r), -jnp.sin(ang_c), jnp.sin(ang_c)], axis=-1)
    return cos, sin


def _chunk_constants():
    c = CHUNK
    t = np.arange(c)
    tri, a_cat, pair, mq, mk, causal, strict = [], [], [], [], [], [], []
    for d in range(2):
        tau = t if d == 0 else c - 1 - t
        incl = (tau[None, :] <= tau[:, None]).astype(np.float32)
        tri.append(incl)
        causal.append(incl)
        strict.append((tau[None, :] < tau[:, None]).astype(np.float32))
        rows, pm, qm, km = [], [], [], []
        for li in range(N_LEVELS):
            s = c >> (li + 1)
            ref_tau = 2 * s * (tau // (2 * s)) + s - 1
            sel = (tau[None, :] == ref_tau[:, None]).astype(np.float32)
            rows.append(incl - sel @ incl)
            odd = ((tau // s) % 2 == 1).astype(np.float32)
            pm.append(((tau[:, None] // (2 * s)) == (tau[None, :] // (2 * s))).astype(np.float32))
            qm.append(np.repeat(odd[:, None], GLA_DK, axis=1))
            km.append(np.repeat((1.0 - odd)[:, None], GLA_DK, axis=1))
        pm.append(np.eye(c, dtype=np.float32))
        rows.append(incl)
        rows.append((tau[None, :] > tau[:, None]).astype(np.float32))
        a = np.concatenate(rows, axis=0)
        a_cat.append(np.concatenate([a, a, a], axis=1))
        pair.append(np.stack(pm))
        mq.append(np.stack(qm))
        mk.append(np.stack(km))
    tri_cat = np.stack([np.concatenate([x, x, x], axis=1) for x in tri])
    sub = np.stack([pair[d][:N_LEVELS] * mq[d][:, :, :1] * np.swapaxes(mk[d][:, :, :1], 1, 2) for d in range(2)])
    return dict(sub=jnp.asarray(sub), sub_b=jnp.asarray(sub, _bf16),
                a_cat=jnp.asarray(np.stack(a_cat), _bf16), pair=jnp.asarray(np.stack(pair)),
                mq=jnp.asarray(np.stack(mq)), mk=jnp.asarray(np.stack(mk)),
                tri_cat=jnp.asarray(tri_cat, _bf16),
                causal=jnp.asarray(np.stack([np.stack([causal[d], strict[d]]) for d in range(2)])))


def _chunk_pos(d, c, n):
    return c if d == 0 else n - 1 - c


def _gla_kernel(q_ref, k_ref, v_ref, r_ref, zs_ref, wg_ref, gb_ref, acat_ref, pair_ref, mq_ref, mk_ref,
                s0_ref, nw_ref, _prev_ref, o_ref, sn_ref, la_scr, oacc_scr, s_scr, *, t):
    c, dk, dv, nh = CHUNK, GLA_DK, GLA_DV, GLA_HEADS
    n = t // c
    zs = _bf(zs_ref[...])
    for d in range(2):
        x = _dot(zs, wg_ref[d]) + gb_ref[d]
        la_scr[d] = (jnp.minimum(x, 0.0) - jnp.log(1.0 + jnp.exp(-jnp.abs(x)))) * (1.0 / GLA_TAU)
        for h in range(nh):
            s_scr[d * nh + h] = s0_ref[0, d, h]
    ones = jnp.ones((3 * c, LANES), _bf16)
    chains = [(d, h) for d in range(2) for h in range(nh)]

    def chunk(ci, carry):
        rows = [pl.ds(pl.multiple_of(_chunk_pos(d, ci, n) * c, c), c) for d in range(2)]
        states = [s_scr[d * nh + h] for d, h in chains]
        dall, blast = [], []
        for d in range(2):
            gcat = jnp.concatenate(_split3(la_scr[d, rows[d], :]), axis=0)
            dall.append(_dot(acat_ref[d], gcat))
            blast.append(_dot_tn(gcat, ones))
        qs, ks, vs = [], [], []
        for d, h in chains:
            kc = slice(h * dk, (h + 1) * dk)
            qs.append(q_ref[rows[d], kc] * (dk ** -0.5))
            ks.append(k_ref[rows[d], kc])
            vs.append(_bf(v_ref[rows[d], h * dv:(h + 1) * dv]))
        atts = [_dot_nt(_bf(q), _bf(k)) * pair_ref[d, N_LEVELS] for (d, h), q, k in zip(chains, qs, ks)]
        for li in range(N_LEVELS):
            for j, (d, h) in enumerate(chains):
                f = jnp.exp(-jnp.abs(dall[d][li * c:(li + 1) * c, h * dk:(h + 1) * dk]))
                atts[j] += _dot_nt(_bf(qs[j] * f * mq_ref[d, li]), _bf(ks[j] * f * mk_ref[d, li])) * pair_ref[d, li]
        outs, new_states = [], []
        for j, (d, h) in enumerate(chains):
            kc = slice(h * dk, (h + 1) * dk)
            eb = jnp.exp(dall[d][N_LEVELS * c:(N_LEVELS + 1) * c, kc])
            el = jnp.exp(dall[d][(N_LEVELS + 1) * c:(N_LEVELS + 2) * c, kc])
            outs.append(_dot(_bf(qs[j] * eb), _bf(states[j])) + _dot(_bf(atts[j]), vs[j]))
            new_states.append(states[j] * jnp.exp(blast[d][kc, :]) + _dot_tn(_bf(ks[j] * el), vs[j]))
        for j, (d, h) in enumerate(chains):
            s_scr[d * nh + h] = new_states[j]
            oacc_scr[d, rows[d], h * dv:(h + 1) * dv] = outs[j]
        return carry

    lax.fori_loop(0, n, chunk, 0)

    for d, h in chains:
        sn_ref[0, d, h] = s_scr[d * nh + h]
    for h in range(nh):
        vc = slice(h * dv, (h + 1) * dv)
        o_ref[:, vc] = _bf(_rms(oacc_scr[0, :, vc] + oacc_scr[1, :, vc], nw_ref[...]) * _silu(r_ref[:, vc]))


def gla(z, row_blk0, nseq, t, wg, gbias, consts, s0, out_norm, prev=None):
    qw = GLA_HEADS * GLA_DK
    vw = GLA_HEADS * GLA_DV
    if prev is None:
        prev = jnp.zeros((z.shape[0], vw), _bf16)

    def zspec(name, width):
        blk = Z_OFF[name] // width
        assert Z_OFF[name] % width == 0
        return pl.BlockSpec((t, width), lambda b: (row_blk0 + b, blk))

    def full(a):
        return pl.BlockSpec(a.shape, lambda b, nd=a.ndim: (0,) * nd)

    state_spec = pl.BlockSpec((1, 2, GLA_HEADS, GLA_DK, GLA_DV), lambda b: (b, 0, 0, 0, 0))
    cs = [consts['a_cat'], consts['pair'], consts['mq'], consts['mk']]
    return pl.pallas_call(
        functools.partial(_gla_kernel, t=t),
        grid=(nseq,),
        in_specs=[zspec('gla_q', qw), zspec('gla_k', qw), zspec('gla_v', vw), zspec('gla_r', vw),
                  zspec('small', LANES), full(wg), full(gbias)] + [full(a) for a in cs]
                 + [state_spec, pl.BlockSpec((1, GLA_DV), lambda b: (0, 0)), pl.BlockSpec(memory_space=pl.ANY)],
        out_specs=[pl.BlockSpec((t, vw), lambda b: (row_blk0 + b, 0)), state_spec],
        out_shape=[jax.ShapeDtypeStruct(prev.shape, prev.dtype),
                   jax.ShapeDtypeStruct((nseq, 2, GLA_HEADS, GLA_DK, GLA_DV), _f32)],
        input_output_aliases={7 + len(cs) + 2: 0},
        scratch_shapes=[pltpu.VMEM((2, t, qw), _f32), pltpu.VMEM((2, t, vw), _f32),
                        pltpu.VMEM((2 * GLA_HEADS, GLA_DK, GLA_DV), _f32)],
        compiler_params=_params(("parallel",)),
        name="gla",
    )(z, z, z, z, z, wg, gbias, *cs, s0, out_norm.reshape(1, GLA_DV), prev)


CONV_PAD = 8
ROW_BLK = 128
GDN_LOCAL_CHUNKS = 2


def _gdn_kernel(q_ref, k_ref, v_ref, zz_ref, zs_ref, cw_ref, alog_ref, dtb_ref, tri_ref, causal_ref, sub_ref,
                subb_ref, s0_ref, nw_ref, _prev_ref, o_ref, sn_ref,
                xp_scr, qkv_scr, g_scr, beta_scr, b_scr, u_scr, w_scr, att_scr, oacc_scr, s_scr, *, t):
    c, hd, nh = CHUNK, HEAD_DIM, GDN_HEADS
    n = t // c
    w = nh * hd
    half = CONV_K // 2
    nblk = t // ROW_BLK
    eye = (lax.broadcasted_iota(jnp.int32, (c, c), 0) == lax.broadcasted_iota(jnp.int32, (c, c), 1)).astype(_f32)

    xp_scr[0:CONV_PAD, :] = jnp.zeros((CONV_PAD, w), _f32)
    xp_scr[CONV_PAD + t:2 * CONV_PAD + t, :] = jnp.zeros((CONV_PAD, w), _f32)
    for idx, src in enumerate((q_ref, k_ref, v_ref)):
        def copy_in(i, carry, src=src):
            r0 = pl.multiple_of(i * ROW_BLK, ROW_BLK)
            xp_scr[pl.ds(CONV_PAD + r0, ROW_BLK), :] = src[pl.ds(r0, ROW_BLK), :]
            return carry

        lax.fori_loop(0, nblk, copy_in, 0)

        def conv(i, carry, idx=idx):
            r0 = pl.multiple_of(i * ROW_BLK, ROW_BLK)
            win = xp_scr[pl.ds(r0, ROW_BLK + 2 * CONV_PAD), :]
            y = jnp.zeros((ROW_BLK, w), _f32)
            for j in range(CONV_K):
                lo = CONV_PAD + j - half
                y += win[lo:lo + ROW_BLK, :] * cw_ref[idx, j:j + 1, :]
            y = _silu(y)
            for h in range(nh):
                cols = slice(h * hd, (h + 1) * hd)
                yh = y[:, cols]
                if idx == 0:
                    yh = yh * lax.rsqrt(jnp.sum(yh * yh, axis=-1, keepdims=True) + EPS) * (hd ** -0.5)
                elif idx == 1:
                    yh = yh * lax.rsqrt(jnp.sum(yh * yh, axis=-1, keepdims=True) + EPS)
                qkv_scr[idx, pl.ds(r0, ROW_BLK), cols] = yh
            return carry

        lax.fori_loop(0, nblk, conv, 0)

    zs = zs_ref[...]
    g_scr[...] = -jnp.exp(alog_ref[...]) * _softplus(zs + dtb_ref[...])
    beta_scr[...] = _sigmoid(zs)
    for d in range(2):
        for h in range(nh):
            s_scr[d * nh + h] = s0_ref[0, d, h]

    chains = [(d, h) for d in range(2) for h in range(nh)]

    def local(ci, carry):
        jobs, rows, b_all, lows, xs, rhss = [], [], [], [], [], []
        for cc in range(GDN_LOCAL_CHUNKS):
            rr = pl.ds(pl.multiple_of((ci * GDN_LOCAL_CHUNKS + cc) * c, c), c)
            rows.append(rr)
            gcat = jnp.concatenate(_split3(g_scr[rr, :]), axis=0)
            beta_all = beta_scr[rr, :]
            b_cc = [_dot(tri_ref[d], gcat) for d in range(2)]
            b_all.append(b_cc)
            b_t = [b.T for b in b_cc]
            qk, kk, ks, vs = [], [], [], []
            for h in range(nh):
                cols = slice(h * hd, (h + 1) * hd)
                k = qkv_scr[1, rr, cols]
                kbf = _bf(k)
                ks.append(k)
                vs.append(qkv_scr[2, rr, cols])
                kk.append(_dot_nt(kbf, kbf))
                qk.append(_dot_nt(_bf(qkv_scr[0, rr, cols]), kbf))
            for d, h in chains:
                ia = SMALL_A + d * nh + h
                ib = SMALL_BETA + d * nh + h
                bcol = b_cc[d][:, ia:ia + 1]
                decay = jnp.exp(jnp.minimum(bcol - b_t[d][ia:ia + 1, :], 0.0)) * causal_ref[d, 0]
                beta = beta_all[:, ib:ib + 1]
                att_scr[d * nh + h, rr, :] = _bf(qk[h] * decay)
                low = kk[h] * beta * decay * causal_ref[d, 1]
                jobs.append((cc, d, h))
                lows.append(_split2(low))
                xs.append(eye - low * sub_ref[d, N_LEVELS - 1])
                rhss.append(jnp.concatenate([vs[h] * beta, ks[h] * (beta * jnp.exp(bcol))], axis=1))
        for li in range(N_LEVELS - 2, -1, -1):
            for j, (cc, d, h) in enumerate(jobs):
                mask = subb_ref[d, li]
                xp = _split2(xs[j])
                t1 = _dot_parts(xp, (lows[j][0] * mask, lows[j][1] * mask))
                xs[j] = xs[j] - _dot_parts(_split2(t1), xp)
        for j, (cc, d, h) in enumerate(jobs):
            cols = slice(h * hd, (h + 1) * hd)
            sol = _dot_parts(_split2(xs[j]), _split2(rhss[j]))
            u_scr[d, rows[cc], cols] = sol[:, :hd]
            w_scr[d, rows[cc], cols] = _bf(sol[:, hd:])
        for cc in range(GDN_LOCAL_CHUNKS):
            for d in range(2):
                b_scr[d, rows[cc], :] = b_all[cc][d]
        return carry

    lax.fori_loop(0, n // GDN_LOCAL_CHUNKS, local, 0)

    def step(ci, carry):
        rows = [pl.ds(pl.multiple_of(_chunk_pos(d, ci, n) * c, c), c) for d in range(2)]
        states = [s_scr[d * nh + h] for d, h in chains]
        b_all = [b_scr[d, rows[d], :] for d in range(2)]
        us, ws, atts, qs, ks = [], [], [], [], []
        for d, h in chains:
            cols = slice(h * hd, (h + 1) * hd)
            us.append(u_scr[d, rows[d], cols])
            ws.append(w_scr[d, rows[d], cols])
            atts.append(att_scr[d * nh + h, rows[d], :])
            qs.append(qkv_scr[0, rows[d], cols])
            ks.append(qkv_scr[1, rows[d], cols])
        sbs = [_bf(s) for s in states]
        v_new = [_bf(u - _dot(w_, sb)) for u, w_, sb in zip(us, ws, sbs)]
        outs, new_states = [], []
        for j, (d, h) in enumerate(chains):
            ia = SMALL_A + d * nh + h
            last = c - 1 if d == 0 else 0
            bcol = b_all[d][:, ia:ia + 1]
            b_last = b_all[d][last:last + 1, ia:ia + 1]
            outs.append(_dot(_bf(qs[j] * jnp.exp(bcol)), sbs[j]) + _dot(atts[j], v_new[j]))
            new_states.append(states[j] * jnp.exp(b_last) + _dot_tn(_bf(ks[j] * jnp.exp(b_last - bcol)), v_new[j]))
        for j, (d, h) in enumerate(chains):
            s_scr[d * nh + h] = new_states[j]
            oacc_scr[d, rows[d], h * hd:(h + 1) * hd] = outs[j]
        return carry

    lax.fori_loop(0, n, step, 0)

    for d, h in chains:
        sn_ref[0, d, h] = s_scr[d * nh + h]

    def epilogue(i, carry):
        r0 = pl.multiple_of(i * ROW_BLK, ROW_BLK)
        rr = pl.ds(r0, ROW_BLK)
        for h in range(nh):
            cols = slice(h * hd, (h + 1) * hd)
            o = oacc_scr[0, rr, cols] + oacc_scr[1, rr, cols]
            o_ref[rr, cols] = _bf(_rms(o, nw_ref[...]) * _silu(zz_ref[rr, cols]))
        return carry

    lax.fori_loop(0, nblk, epilogue, 0)


def gdn(z, row_blk0, nseq, t, conv_w, alog_lane, dtb_lane, consts, s0, out_norm, prev=None):
    w = GDN_HEADS * HEAD_DIM
    if prev is None:
        prev = jnp.zeros((z.shape[0], w), _bf16)
    qblk = Z_OFF['gdn_qkv'] // w
    assert Z_OFF['gdn_qkv'] % w == 0 and Z_OFF['gdn_z'] % w == 0 and Z_OFF['small'] % LANES == 0

    def zspec(blk, width):
        return pl.BlockSpec((t, width), lambda b: (row_blk0 + b, blk))

    def full(a):
        return pl.BlockSpec(a.shape, lambda b, nd=a.ndim: (0,) * nd)

    state_spec = pl.BlockSpec((1, 2, GDN_HEADS, GDN_DK, GDN_DV), lambda b: (b, 0, 0, 0, 0))
    return pl.pallas_call(
        functools.partial(_gdn_kernel, t=t),
        grid=(nseq,),
        in_specs=[zspec(qblk, w), zspec(qblk + 1, w), zspec(qblk + 2, w), zspec(Z_OFF['gdn_z'] // w, w),
                  zspec(Z_OFF['small'] // LANES, LANES), full(conv_w), full(alog_lane), full(dtb_lane),
                  full(consts['tri_cat']), full(consts['causal']), full(consts['sub']), full(consts['sub_b']),
                  state_spec, pl.BlockSpec((1, GDN_DV), lambda b: (0, 0)), pl.BlockSpec(memory_space=pl.ANY)],
        out_specs=[pl.BlockSpec((t, w), lambda b: (row_blk0 + b, 0)), state_spec],
        out_shape=[jax.ShapeDtypeStruct(prev.shape, prev.dtype),
                   jax.ShapeDtypeStruct((nseq, 2, GDN_HEADS, GDN_DK, GDN_DV), _f32)],
        input_output_aliases={14: 0},
        scratch_shapes=[pltpu.VMEM((t + 2 * CONV_PAD, w), _f32),
                        pltpu.VMEM((3, t, w), _f32),
                        pltpu.VMEM((t, LANES), _f32),
                        pltpu.VMEM((t, LANES), _f32),
                        pltpu.VMEM((2, t, LANES), _f32),
                        pltpu.VMEM((2, t, w), _f32),
                        pltpu.VMEM((2, t, w), _bf16),
                        pltpu.VMEM((2 * GDN_HEADS, t, CHUNK), _bf16),
                        pltpu.VMEM((2, t, w), _f32),
                        pltpu.VMEM((2 * GDN_HEADS, GDN_DK, GDN_DV), _f32)],
        compiler_params=_params(("parallel",)),
        name="gdn",
    )(z, z, z, z, z, conv_w, alog_lane, dtb_lane, consts['tri_cat'], consts['causal'], consts['sub'],
      consts['sub_b'], s0, out_norm.reshape(1, GDN_DV), prev)


def _column_runs():
    runs, start, n = [], 0, len(Z_PERM)
    while start < n:
        stop = start + 1
        while (stop < n and stop - start < CAST_CHUNK and stop % Z_TILE != 0
               and (Z_PERM[stop] == Z_PERM[stop - 1] + 1 if Z_PERM[start] >= 0 else Z_PERM[stop] < 0)):
            stop += 1
        runs.append((start, int(Z_PERM[start]), stop - start))
        start = stop
    return runs


def _permute_kernel(w_ref, o_ref):
    for dst, src, width in _column_runs():
        t, c = divmod(dst, Z_TILE)
        if src < 0:
            o_ref[0, t, :, c:c + width] = jnp.zeros((CAST_BLK, width), _bf16)
        else:
            o_ref[0, t, :, c:c + width] = _bf(w_ref[0, :, src:src + width])


def _permute_columns(w_in):
    depth, d, cols = w_in.shape
    nj = Z_COLS // Z_TILE
    return pl.pallas_call(
        _permute_kernel,
        grid=(depth, d // CAST_BLK),
        in_specs=[pl.BlockSpec((1, CAST_BLK, cols), lambda l, r: (l, r, 0))],
        out_specs=pl.BlockSpec((1, nj, CAST_BLK, Z_TILE), lambda l, r: (l, 0, r, 0)),
        out_shape=jax.ShapeDtypeStruct((depth, nj, d, Z_TILE), _bf16),
        compiler_params=_params(("parallel", "parallel")),
        name="permute_w_in",
    )(w_in)


def _lane_vector(values, offset):
    k = values.shape[-1]
    return jnp.pad(values.astype(_f32), ((0, 0), (offset, LANES - offset - k)))[:, None, :]


def kernel(x_prompt, x_sample, cache_na_k, cache_na_v, cache_gqa_k, cache_gqa_v, state_gla, state_gdn, c, c_ctx, norm_g, w_mod, b_mod, ffn_gu, ffn_down, w_in, w_out, na_qk_norm, na_rpb, gla_gate_up, gla_gate_bias, gla_out_norm, gqa_qk_norm, gdn_conv, gdn_a_log, gdn_dt_bias, gdn_out_norm):
    assert GRID_ROWS % NA_QROWS == 0 and GRID_ROWS >= NA_KROWS and PAST_LEN == SEQ
    d = D_MODEL
    m_ctx = BATCH * SEQ
    assert m_ctx % DEC_SEQ == 0 and DEC_SEQ % TOKEN_TILE == 0
    x = jnp.concatenate([x_prompt.reshape(m_ctx, d), x_sample.reshape(DEC_BATCH * DEC_SEQ, d)], axis=0)

    n_cond = 1 + DEC_BATCH
    cond = jnp.concatenate([c_ctx[None, :], c], axis=0)
    cond = jnp.pad(cond, ((0, (-n_cond) % 8), (0, 0)))
    mod = adaln_all(cond, w_mod, b_mod).reshape(DEPTH, cond.shape[0], N_MOD, d)

    w_gate, w_up, down = cast_ffn_weights(ffn_gu, ffn_down)
    w_in_p = _permute_columns(w_in)
    w_out_b = w_out.astype(_bf16)

    consts = _chunk_constants()
    cos, sin = _rope_tables()
    bias = na_bias_tables(na_rpb)

    qw = GLA_HEADS * GLA_DK
    wg = jnp.zeros((DEPTH, 2, LANES, qw), _f32)
    wg = wg.at[:, 0, 0:GLA_GATE_RANK].set(gla_gate_up[:, 0])
    wg = wg.at[:, 1, SMALL_GB:SMALL_GB + GLA_GATE_RANK].set(gla_gate_up[:, 1]).astype(_bf16)
    gbias = gla_gate_bias.reshape(DEPTH, 2, 1, qw)
    conv_w = gdn_conv.reshape(DEPTH, CONV_K, 3, GDN_HEADS * HEAD_DIM).transpose(0, 2, 1, 3)
    conv_w = jnp.pad(conv_w, ((0, 0), (0, 0), (0, 8 - CONV_K), (0, 0)))
    alog_lane = _lane_vector(gdn_a_log.reshape(DEPTH, 2 * GDN_HEADS), SMALL_A)
    dtb_lane = _lane_vector(gdn_dt_bias.reshape(DEPTH, 2 * GDN_HEADS), SMALL_A)
    gla_zero = jnp.zeros((BATCH, 2, GLA_HEADS, GLA_DK, GLA_DV), _f32)
    gdn_zero = jnp.zeros((BATCH, 2, GDN_HEADS, GDN_DK, GDN_DV), _f32)
    lat_blk0 = m_ctx // DEC_SEQ

    gla_l, gdn_l = [], []
    caches = None
    for l in range(DEPTH):
        x = ffn(x, mod[l], norm_g[l, 0], w_gate, w_up, down, 2 * l, 0)
        z = in_proj(x, mod[l], norm_g[l, 1], w_in_p, l)

        o_na, o_gqa, caches = ctx_attention(z, na_qk_norm[l], gqa_qk_norm[l], l, caches)
        o_na = na_latent(z, cache_na_k, cache_na_v, bias, na_qk_norm[l], l, o_na)
        o_gqa = gqa_latent(z, cache_gqa_k, cache_gqa_v, cos, sin, gqa_qk_norm[l], l, o_gqa)
        o_gla, sg = gla(z, 0, BATCH, SEQ, wg[l], gbias[l], consts, gla_zero, gla_out_norm[l])
        o_gla, _ = gla(z, lat_blk0, DEC_BATCH, DEC_SEQ, wg[l], gbias[l], consts, state_gla[:, l],
                       gla_out_norm[l], o_gla)
        o_gdn, sd = gdn(z, 0, BATCH, SEQ, conv_w[l], alog_lane[l], dtb_lane[l], consts, gdn_zero,
                        gdn_out_norm[l])
        o_gdn, _ = gdn(z, lat_blk0, DEC_BATCH, DEC_SEQ, conv_w[l], alog_lane[l], dtb_lane[l], consts,
                       state_gdn[:, l], gdn_out_norm[l], o_gdn)

        x = out_proj(x, mod[l], o_na, o_gla, o_gqa, o_gdn, w_out_b, l)
        x = ffn(x, mod[l], norm_g[l, 2], w_gate, w_up, down, 2 * l + 1, 6)

        gla_l.append(sg)
        gdn_l.append(sd)

    y_prompt = x[:m_ctx].reshape(BATCH, SEQ, d)
    y_sample = x[m_ctx:].reshape(DEC_BATCH, DEC_SEQ, d)
    return (y_prompt, y_sample) + caches + (jnp.stack(gla_l, axis=1), jnp.stack(gdn_l, axis=1))
```

```python
import functools
import math

import jax
import jax.numpy as jnp
import numpy as np
from jax import lax
from jax.experimental import pallas as pl
from jax.experimental.pallas import tpu as pltpu

D_MODEL = 2048
BATCH = 16
SEQ = 256
DEPTH = 4
DEC_BATCH = 8
DEC_SEQ = 1024
PAST_LEN = 256

GRID_W = 64
HEAD_DIM = 128
GROUP_WIDTH = D_MODEL // 4
NA_HEADS = GROUP_WIDTH // HEAD_DIM
NA_WIN_ROWS = 8
NA_WIN_COLS = 16
GLA_HEADS = 4
GLA_DV = GROUP_WIDTH // GLA_HEADS
GLA_DK = GLA_DV // 2
GLA_GATE_RANK = 16
GLA_TAU = 16.0
GQA_HEADS = GROUP_WIDTH // HEAD_DIM
GQA_KV_HEADS = GQA_HEADS // 2
GDN_HEADS = GROUP_WIDTH // HEAD_DIM
GDN_DK = HEAD_DIM
GDN_DV = HEAD_DIM
CONV_K = 5
CHUNK = 64
ROPE_THETA = 10000.0
FFN_DIM = ((8 * D_MODEL // 3 + 127) // 128) * 128
N_MOD = 9
EPS = 1e-6
NEG_INF = -1e30

IN_SPLITS = (
    ('na_q', NA_HEADS * HEAD_DIM), ('na_k', NA_HEADS * HEAD_DIM), ('na_v', NA_HEADS * HEAD_DIM),
    ('gla_q', GLA_HEADS * GLA_DK), ('gla_k', GLA_HEADS * GLA_DK), ('gla_v', GLA_HEADS * GLA_DV),
    ('gla_r', GLA_HEADS * GLA_DV), ('gla_gf', GLA_GATE_RANK), ('gla_gb', GLA_GATE_RANK),
    ('gqa_q', GQA_HEADS * HEAD_DIM), ('gqa_k', GQA_KV_HEADS * HEAD_DIM), ('gqa_v', GQA_KV_HEADS * HEAD_DIM),
    ('gdn_qkv', 3 * GDN_HEADS * HEAD_DIM), ('gdn_z', GDN_HEADS * GDN_DV),
    ('gdn_b', 2 * GDN_HEADS), ('gdn_a', 2 * GDN_HEADS),
)

LANES = 128
FFN_TILE = 512
FFN_PAD = ((FFN_DIM + FFN_TILE - 1) // FFN_TILE) * FFN_TILE
TOKEN_TILE = 512
INPROJ_TOKEN_TILE = 1024
FFN_TOKEN_TILE = 512
VMEM_LIMIT = 56 * 1024 * 1024

Z_ORDER = ('na_q', 'na_k', 'na_v', 'gla_q', 'gla_k', 'gla_v', 'gla_r', 'gqa_q', 'gqa_k', 'gqa_v',
           'gdn_qkv', 'gdn_z')
SMALL_ORDER = ('gla_gf', 'gla_gb', 'gdn_b', 'gdn_a')
SMALL_GB = GLA_GATE_RANK
SMALL_BETA = 2 * GLA_GATE_RANK
SMALL_A = SMALL_BETA + 2 * GDN_HEADS
N_LEVELS = int(math.log2(CHUNK))


def _layout():
    src, off = {}, 0
    for name, width in IN_SPLITS:
        src[name] = (off, width)
        off += width
    perm, zoff, pos = [], {}, 0
    for name in Z_ORDER:
        o, w = src[name]
        zoff[name] = pos
        perm.extend(range(o, o + w))
        pos += w
    zoff['small'] = pos
    n_small = 0
    for name in SMALL_ORDER:
        o, w = src[name]
        perm.extend(range(o, o + w))
        n_small += w
    perm.extend([-1] * (LANES - n_small))
    pos += LANES
    perm.extend([-1] * ((-pos) % Z_TILE))
    return np.asarray(perm, np.int32), zoff, len(perm)


Z_TILE = 5 * 256
Z_PERM, Z_OFF, Z_COLS = _layout()

_f32 = jnp.float32
_bf16 = jnp.bfloat16


def _dot(a, b):
    return jnp.dot(a, b, preferred_element_type=_f32)


def _dot_nt(a, b):
    return lax.dot_general(a, b, (((1,), (1,)), ((), ())), preferred_element_type=_f32)


def _dot_tn(a, b):
    return lax.dot_general(a, b, (((0,), (0,)), ((), ())), preferred_element_type=_f32)


def _bf(x):
    return x.astype(_bf16)


def _sigmoid(x):
    return 1.0 / (1.0 + jnp.exp(-x))


def _silu(x):
    return x * _sigmoid(x)


def _softplus(x):
    return jnp.maximum(x, 0.0) + jnp.log(1.0 + jnp.exp(-jnp.abs(x)))


def _rms(x, w):
    return x * lax.rsqrt(jnp.mean(x * x, axis=-1, keepdims=True) + EPS) * w


def _split3(x):
    hi = _bf(x)
    r1 = x - hi.astype(_f32)
    mid = _bf(r1)
    lo = _bf(r1 - mid.astype(_f32))
    return hi, mid, lo


def _split2(x):
    hi = _bf(x)
    return hi, _bf(x - hi.astype(_f32))


def _dot_parts(a, b):
    return _dot(jnp.concatenate([a[0], a[0], a[1]], axis=1), jnp.concatenate([b[0], b[1], b[0]], axis=0))


def _cond_row(i, tile):
    n_ctx = (BATCH * SEQ) // tile
    return jnp.where(i < n_ctx, 0, 1 + (i - n_ctx) // (DEC_SEQ // tile))


def _params(sem):
    return pltpu.CompilerParams(dimension_semantics=sem, vmem_limit_bytes=VMEM_LIMIT)


def _adaln_kernel(c_ref, w_ref, b_ref, o_ref):
    a = _bf(_silu(c_ref[...]))
    o_ref[0] = _dot(a, _bf(w_ref[0])) + b_ref[0]


def adaln_all(cond, w_mod, b_mod):
    nc = cond.shape[0]
    tn = 1024
    n = N_MOD * D_MODEL
    return pl.pallas_call(
        _adaln_kernel,
        grid=(DEPTH, n // tn),
        in_specs=[pl.BlockSpec((nc, D_MODEL), lambda l, j: (0, 0)),
                  pl.BlockSpec((1, D_MODEL, tn), lambda l, j: (l, 0, j)),
                  pl.BlockSpec((1, 1, tn), lambda l, j: (l, 0, j))],
        out_specs=pl.BlockSpec((1, nc, tn), lambda l, j: (l, 0, j)),
        out_shape=jax.ShapeDtypeStruct((DEPTH, nc, n), _f32),
        compiler_params=_params(("parallel", "parallel")),
        name="adaln",
    )(cond, w_mod, b_mod.reshape(DEPTH, 1, n))


CAST_BLK = 256
CAST_CHUNK = 1024


def _cast_gu_kernel(w_ref, g_ref, u_ref):
    f = FFN_DIM
    for t in range(FFN_PAD // FFN_TILE):
        c0 = t * FFN_TILE
        n_real = max(0, min(FFN_TILE, f - c0))
        if n_real:
            g_ref[0, t, :, :n_real] = _bf(w_ref[0, :, c0:c0 + n_real])
            u_ref[0, t, :, :n_real] = _bf(w_ref[0, :, f + c0:f + c0 + n_real])
        if n_real < FFN_TILE:
            g_ref[0, t, :, n_real:] = jnp.zeros((CAST_BLK, FFN_TILE - n_real), _bf16)
            u_ref[0, t, :, n_real:] = jnp.zeros((CAST_BLK, FFN_TILE - n_real), _bf16)


def _cast_down_kernel(w_ref, o_ref):
    f = FFN_DIM
    for r0 in range(0, f, CAST_CHUNK):
        r1 = min(r0 + CAST_CHUNK, f)
        o_ref[0, r0:r1, :] = _bf(w_ref[0, r0:r1, :])
    if FFN_PAD > f:
        o_ref[0, f:, :] = jnp.zeros((FFN_PAD - f, CAST_BLK), _bf16)


def cast_ffn_weights(ffn_gu, ffn_down):
    d, f = D_MODEL, FFN_DIM
    n = ffn_gu.shape[0] * ffn_gu.shape[1]
    nf = FFN_PAD // FFN_TILE
    gate, up = pl.pallas_call(
        _cast_gu_kernel,
        grid=(n, d // CAST_BLK),
        in_specs=[pl.BlockSpec((1, CAST_BLK, 2 * f), lambda i, r: (i, r, 0))],
        out_specs=[pl.BlockSpec((1, nf, CAST_BLK, FFN_TILE), lambda i, r: (i, 0, r, 0))] * 2,
        out_shape=[jax.ShapeDtypeStruct((n, nf, d, FFN_TILE), _bf16)] * 2,
        compiler_params=_params(("parallel", "parallel")),
        name="cast_gate_up",
    )(ffn_gu.reshape(n, d, 2 * f))
    down = pl.pallas_call(
        _cast_down_kernel,
        grid=(n, d // CAST_BLK),
        in_specs=[pl.BlockSpec((1, f, CAST_BLK), lambda i, r: (i, 0, r))],
        out_specs=pl.BlockSpec((1, FFN_PAD, CAST_BLK), lambda i, r: (i, 0, r)),
        out_shape=jax.ShapeDtypeStruct((n, FFN_PAD, d), _bf16),
        compiler_params=_params(("parallel", "parallel")),
        name="cast_down",
    )(ffn_down.reshape(n, f, d))
    return gate, up, down


def _modulated_norm(x, g, m_ref, base):
    sh = m_ref[0, base:base + 1, :]
    sc = m_ref[0, base + 1:base + 2, :]
    return _rms(x, g) * (1.0 + sc) + sh


def _ffn_kernel(x_ref, xn_ref, m_ref, mn_ref, g_ref, wg_ref, wu_ref, wd_ref, o_ref, h0_scr, h1_scr, acc_scr, *,
                base, nsplit):
    i, f = pl.program_id(0), pl.program_id(1)

    @pl.when((i == 0) & (f == 0))
    def _():
        h0_scr[...] = _bf(_modulated_norm(x_ref[...], g_ref[...], m_ref, base))

    @pl.when(f == 0)
    def _():
        acc_scr[...] = jnp.zeros_like(acc_scr)

    rows_per = xn_ref.shape[0]
    rr = pl.ds(pl.multiple_of(jnp.minimum(f, nsplit - 1) * rows_per, rows_per), rows_per)

    def step(h_cur, h_nxt):
        h = h_cur[...]
        gate = _dot(h, wg_ref[...])
        up = _dot(h, wu_ref[...])
        acc_scr[...] += _dot(_bf(_silu(gate) * up), wd_ref[...])
        h_nxt[rr, :] = _bf(_modulated_norm(xn_ref[...], g_ref[...], mn_ref, base))

    @pl.when(i % 2 == 0)
    def _():
        step(h0_scr, h1_scr)

    @pl.when(i % 2 == 1)
    def _():
        step(h1_scr, h0_scr)

    @pl.when(f == pl.num_programs(1) - 1)
    def _():
        o_ref[...] = x_ref[...] + (0.5 * m_ref[0, base + 2:base + 3, :]) * acc_scr[...]


def ffn(x, mod, norm_g, w_gate, w_up, w_down, widx, base):
    m, d = x.shape
    nf = FFN_PAD // FFN_TILE
    tm = FFN_TOKEN_TILE
    nsplit = max(s for s in (1, 2, 4, 8) if s <= nf)
    xn_spec, mn_spec = _lookahead_specs(tm, d, nsplit, m // tm - 1)
    return pl.pallas_call(
        functools.partial(_ffn_kernel, base=base, nsplit=nsplit),
        grid=(m // tm, nf),
        in_specs=[pl.BlockSpec((tm, d), lambda i, f: (i, 0)), xn_spec,
                  pl.BlockSpec((1, N_MOD, d), lambda i, f: (_cond_row(i, tm), 0, 0)), mn_spec,
                  pl.BlockSpec((1, d), lambda i, f: (0, 0)),
                  pl.BlockSpec((None, None, d, FFN_TILE), lambda i, f: (widx, f, 0, 0)),
                  pl.BlockSpec((None, None, d, FFN_TILE), lambda i, f: (widx, f, 0, 0)),
                  pl.BlockSpec((None, FFN_TILE, d), lambda i, f: (widx, f, 0))],
        out_specs=pl.BlockSpec((tm, d), lambda i, f: (i, 0)),
        out_shape=jax.ShapeDtypeStruct((m, d), _f32),
        scratch_shapes=[pltpu.VMEM((tm, d), _bf16), pltpu.VMEM((tm, d), _bf16), pltpu.VMEM((tm, d), _f32)],
        compiler_params=_params(("arbitrary", "arbitrary")),
        name="ffn",
    )(x, x, mod, mod, norm_g.reshape(1, d), w_gate, w_up, w_down)


def _inproj_kernel(x_ref, xn_ref, m_ref, mn_ref, g_ref, w_ref, o_ref, h0_scr, h1_scr, *, nsplit):
    i, j = pl.program_id(0), pl.program_id(1)

    @pl.when((i == 0) & (j == 0))
    def _():
        h0_scr[...] = _bf(_modulated_norm(x_ref[...], g_ref[...], m_ref, 3))

    rows_per = xn_ref.shape[0]
    rr = pl.ds(pl.multiple_of(jnp.minimum(j, nsplit - 1) * rows_per, rows_per), rows_per)

    def step(h_cur, h_nxt):
        o_ref[...] = _dot(h_cur[...], w_ref[...])
        h_nxt[rr, :] = _bf(_modulated_norm(xn_ref[...], g_ref[...], mn_ref, 3))

    @pl.when(i % 2 == 0)
    def _():
        step(h0_scr, h1_scr)

    @pl.when(i % 2 == 1)
    def _():
        step(h1_scr, h0_scr)


def _lookahead_specs(tm, d, nsplit, last):
    rows = tm // nsplit
    return (pl.BlockSpec((rows, d), lambda i, s: (jnp.minimum(i + 1, last) * nsplit + jnp.minimum(s, nsplit - 1), 0)),
            pl.BlockSpec((1, N_MOD, d), lambda i, s: (_cond_row(jnp.minimum(i + 1, last), tm), 0, 0)))


def in_proj(x, mod, norm_g, w_in, layer):
    m, d = x.shape
    tm = INPROJ_TOKEN_TILE
    nj = Z_COLS // Z_TILE
    nsplit = max(s for s in (1, 2, 4, 8) if s <= nj)
    xn_spec, mn_spec = _lookahead_specs(tm, d, nsplit, m // tm - 1)
    return pl.pallas_call(
        functools.partial(_inproj_kernel, nsplit=nsplit),
        grid=(m // tm, nj),
        in_specs=[pl.BlockSpec((tm, d), lambda i, j: (0, 0)), xn_spec,
                  pl.BlockSpec((1, N_MOD, d), lambda i, j: (_cond_row(0, tm), 0, 0)), mn_spec,
                  pl.BlockSpec((1, d), lambda i, j: (0, 0)),
                  pl.BlockSpec((None, None, d, Z_TILE), lambda i, j: (layer, j, 0, 0))],
        out_specs=pl.BlockSpec((tm, Z_TILE), lambda i, j: (i, j)),
        out_shape=jax.ShapeDtypeStruct((m, Z_COLS), _f32),
        scratch_shapes=[pltpu.VMEM((tm, d), _bf16), pltpu.VMEM((tm, d), _bf16)],
        compiler_params=_params(("arbitrary", "arbitrary")),
        name="in_proj",
    )(x, x, mod, mod, norm_g.reshape(1, d), w_in)


def _outproj_kernel(x_ref, m_ref, a_ref, b_ref, c_ref, d_ref, w_ref, o_ref):
    gw = GROUP_WIDTH
    acc = _dot(a_ref[...], w_ref[0:gw, :])
    acc += _dot(b_ref[...], w_ref[gw:2 * gw, :])
    acc += _dot(c_ref[...], w_ref[2 * gw:3 * gw, :])
    acc += _dot(d_ref[...], w_ref[3 * gw:4 * gw, :])
    o_ref[...] = x_ref[...] + m_ref[0, 5:6, :] * acc


def out_proj(x, mod, o_na, o_gla, o_gqa, o_gdn, w_out, layer):
    m, d = x.shape
    tm = TOKEN_TILE
    grp = pl.BlockSpec((tm, GROUP_WIDTH), lambda i: (i, 0))
    return pl.pallas_call(
        _outproj_kernel,
        grid=(m // tm,),
        in_specs=[pl.BlockSpec((tm, d), lambda i: (i, 0)),
                  pl.BlockSpec((1, N_MOD, d), lambda i: (_cond_row(i, tm), 0, 0)),
                  grp, grp, grp, grp,
                  pl.BlockSpec((None, 4 * GROUP_WIDTH, d), lambda i: (layer, 0, 0))],
        out_specs=pl.BlockSpec((tm, d), lambda i: (i, 0)),
        out_shape=jax.ShapeDtypeStruct((m, d), _f32),
        compiler_params=_params(("parallel",)),
        name="out_proj",
    )(x, mod, o_na, o_gla, o_gqa, o_gdn, w_out)


LOG2E = math.log2(math.e)
ATTN_Q_SCALE = HEAD_DIM ** -0.5 * LOG2E


def _softmax_pv(scores, values):
    mx = scores[0].max(axis=-1, keepdims=True)
    for s in scores[1:]:
        mx = jnp.maximum(mx, s.max(axis=-1, keepdims=True))
    num, den = None, None
    for s, v in zip(scores, values):
        p = jnp.exp2(s - mx)
        d_ = p.sum(axis=-1, keepdims=True)
        n_ = _dot(_bf(p), v)
        num = n_ if num is None else num + n_
        den = d_ if den is None else den + d_
    return num / den


def _rope(x, cos, sin):
    lane = lax.broadcasted_iota(jnp.int32, x.shape, 1)
    quarter = HEAD_DIM // 4
    partner = jnp.where((lane % (2 * quarter)) < quarter,
                        pltpu.roll(x, HEAD_DIM - quarter, 1), pltpu.roll(x, quarter, 1))
    return x * cos + partner * sin


def _ctx_attn_kernel(nq_ref, nk_ref, nv_ref, gq_ref, gk_ref, gv_ref, nw_ref, gw_ref, *refs, n_alias):
    ona_ref, ogqa_ref, kn_ref, vn_ref, ka_ref, va_ref = refs[n_alias:]
    hd = HEAD_DIM
    for h in range(NA_HEADS):
        cols = slice(h * hd, (h + 1) * hd)
        k = _rms(nk_ref[:, cols], nw_ref[1:2, :])
        v = nv_ref[:, cols]
        q = _rms(nq_ref[:, cols], nw_ref[0:1, :]) * ATTN_Q_SCALE
        kn_ref[0, h] = k
        vn_ref[0, h] = v
        s = _dot_nt(_bf(q), _bf(k))
        ona_ref[:, cols] = _bf(_softmax_pv([s], [_bf(v)]))
    group = GQA_HEADS // GQA_KV_HEADS
    for kv in range(GQA_KV_HEADS):
        cols = slice(kv * hd, (kv + 1) * hd)
        k = _rms(gk_ref[:, cols], gw_ref[1:2, :])
        v = gv_ref[:, cols]
        ka_ref[0, kv] = k
        va_ref[0, kv] = v
        for g in range(group):
            qcols = slice((kv * group + g) * hd, (kv * group + g + 1) * hd)
            q = _rms(gq_ref[:, qcols], gw_ref[0:1, :]) * ATTN_Q_SCALE
            s = _dot_nt(_bf(q), _bf(k))
            ogqa_ref[:, qcols] = _bf(_softmax_pv([s], [_bf(v)]))


def ctx_attention(z, na_w, gqa_w, layer, caches):
    t, hd = SEQ, HEAD_DIM

    def zspec(name, width):
        blk = Z_OFF[name] // width
        assert Z_OFF[name] % width == 0
        return pl.BlockSpec((t, width), lambda b: (b, blk))

    def cache_spec(heads):
        return pl.BlockSpec((1, None, heads, t, hd), lambda b: (b, layer, 0, 0, 0))

    def cache_shape(heads):
        return jax.ShapeDtypeStruct((BATCH, DEPTH, heads, t, hd), _f32)

    gw = GROUP_WIDTH
    kvw = GQA_KV_HEADS * hd
    in_specs = [zspec('na_q', gw), zspec('na_k', gw), zspec('na_v', gw),
                zspec('gqa_q', gw), zspec('gqa_k', kvw), zspec('gqa_v', kvw),
                pl.BlockSpec((2, hd), lambda b: (0, 0)), pl.BlockSpec((2, hd), lambda b: (0, 0))]
    args = [z, z, z, z, z, z, na_w, gqa_w]
    aliases = {}
    if caches is not None:
        aliases = {len(args) + i: 2 + i for i in range(4)}
        in_specs += [pl.BlockSpec(memory_space=pl.ANY)] * 4
        args += list(caches)
    outs = pl.pallas_call(
        functools.partial(_ctx_attn_kernel, n_alias=len(aliases)),
        grid=(BATCH,),
        in_specs=in_specs,
        out_specs=[pl.BlockSpec((t, gw), lambda b: (b, 0)), pl.BlockSpec((t, gw), lambda b: (b, 0)),
                   cache_spec(NA_HEADS), cache_spec(NA_HEADS), cache_spec(GQA_KV_HEADS), cache_spec(GQA_KV_HEADS)],
        out_shape=[jax.ShapeDtypeStruct((z.shape[0], gw), _bf16), jax.ShapeDtypeStruct((z.shape[0], gw), _bf16),
                   cache_shape(NA_HEADS), cache_shape(NA_HEADS), cache_shape(GQA_KV_HEADS),
                   cache_shape(GQA_KV_HEADS)],
        input_output_aliases=aliases,
        compiler_params=_params(("parallel",)),
        name="ctx_attention",
    )(*args)
    return outs[0], outs[1], tuple(outs[2:])


NA_QROWS = 4
NA_KROWS = NA_QROWS + NA_WIN_ROWS
GRID_ROWS = DEC_SEQ // GRID_W


def _na_key_row0(blk):
    return min(max(blk * NA_QROWS - NA_WIN_ROWS // 2, 0), GRID_ROWS - NA_KROWS)


def _na_bias_kernel(rpb_ref, o_ref):
    l, h = pl.program_id(0), pl.program_id(1)
    n_dr, n_dc = 2 * NA_WIN_ROWS - 1, 2 * NA_WIN_COLS - 1
    base = (l * NA_HEADS + h) * (n_dr * n_dc)
    qc = lax.broadcasted_iota(jnp.int32, (GRID_W, GRID_W), 0)
    kc = lax.broadcasted_iota(jnp.int32, (GRID_W, GRID_W), 1)
    dc = jnp.clip(kc - qc + (NA_WIN_COLS - 1), 0, n_dc - 1)
    c0 = jnp.clip(qc - NA_WIN_COLS // 2, 0, GRID_W - NA_WIN_COLS)
    in_win = (kc >= c0) & (kc < c0 + NA_WIN_COLS)
    masked = jnp.full((GRID_W, GRID_W), NEG_INF, _f32)
    tiles = []
    for dr in range(n_dr):
        t = jnp.zeros((GRID_W, GRID_W), _f32)
        for j in range(n_dc):
            t = jnp.where(dc == j, rpb_ref[base + dr * n_dc + j], t)
        tiles.append(jnp.where(in_win, t * LOG2E, NEG_INF))
    for blk in range(GRID_ROWS // NA_QROWS):
        k0 = _na_key_row0(blk)
        for qi in range(NA_QROWS):
            r = blk * NA_QROWS + qi
            krow0 = min(max(r - NA_WIN_ROWS // 2, 0), GRID_ROWS - NA_WIN_ROWS)
            for kj in range(NA_KROWS):
                kr = k0 + kj
                ok = krow0 <= kr < krow0 + NA_WIN_ROWS
                tile = tiles[kr - r + NA_WIN_ROWS - 1] if ok else masked
                o_ref[0, 0, blk, qi * GRID_W:(qi + 1) * GRID_W, kj * GRID_W:(kj + 1) * GRID_W] = tile


def na_bias_tables(na_rpb):
    nblk = GRID_ROWS // NA_QROWS
    shape = (DEPTH, NA_HEADS, nblk, NA_QROWS * GRID_W, NA_KROWS * GRID_W)
    return pl.pallas_call(
        _na_bias_kernel,
        grid=(DEPTH, NA_HEADS),
        in_specs=[pl.BlockSpec(memory_space=pltpu.SMEM)],
        out_specs=pl.BlockSpec((1, 1) + shape[2:], lambda l, h: (l, h, 0, 0, 0)),
        out_shape=jax.ShapeDtypeStruct(shape, _f32),
        compiler_params=_params(("parallel", "parallel")),
        name="na_bias",
    )(na_rpb.reshape(-1))


def _na_lat_kernel(q_ref, k_ref, v_ref, ck_ref, cv_ref, bias_ref, nw_ref, _prev_ref, o_ref, kn_scr, vb_scr):
    kn_scr[...] = _bf(_rms(k_ref[...], nw_ref[1:2, :]))
    vb_scr[...] = _bf(v_ref[...])
    ck = _bf(ck_ref[0, 0, 0])
    cv = _bf(cv_ref[0, 0, 0])
    tq = NA_QROWS * GRID_W
    nkeys = NA_KROWS * GRID_W
    for blk in range(GRID_ROWS // NA_QROWS):
        start = _na_key_row0(blk) * GRID_W
        rows = slice(blk * tq, (blk + 1) * tq)
        q = _bf(_rms(q_ref[rows, :], nw_ref[0:1, :]) * ATTN_Q_SCALE)
        s_ctx = _dot_nt(q, ck)
        s_loc = _dot_nt(q, kn_scr[start:start + nkeys, :]) + bias_ref[0, 0, blk]
        o_ref[rows, :] = _bf(_softmax_pv([s_ctx, s_loc], [cv, vb_scr[start:start + nkeys, :]]))


def na_latent(z, cache_k, cache_v, bias, na_w, layer, prev):
    t, hd = DEC_SEQ, HEAD_DIM
    tq = NA_QROWS * GRID_W
    nblk = t // tq
    ctx_seqs = (BATCH * SEQ) // t
    qb, kb, vb = Z_OFF['na_q'] // hd, Z_OFF['na_k'] // hd, Z_OFF['na_v'] // hd
    cache_spec = pl.BlockSpec((1, 1, 1, PAST_LEN, hd), lambda b, h: (b, layer, h, 0, 0))
    return pl.pallas_call(
        _na_lat_kernel,
        grid=(DEC_BATCH, NA_HEADS),
        in_specs=[pl.BlockSpec((t, hd), lambda b, h: (ctx_seqs + b, qb + h)),
                  pl.BlockSpec((t, hd), lambda b, h: (ctx_seqs + b, kb + h)),
                  pl.BlockSpec((t, hd), lambda b, h: (ctx_seqs + b, vb + h)),
                  cache_spec, cache_spec,
                  pl.BlockSpec((1, 1, nblk, tq, NA_KROWS * GRID_W), lambda b, h: (layer, h, 0, 0, 0)),
                  pl.BlockSpec((2, hd), lambda b, h: (0, 0)),
                  pl.BlockSpec(memory_space=pl.ANY)],
        out_specs=pl.BlockSpec((t, hd), lambda b, h: (ctx_seqs + b, h)),
        out_shape=jax.ShapeDtypeStruct(prev.shape, prev.dtype),
        input_output_aliases={7: 0},
        scratch_shapes=[pltpu.VMEM((t, hd), _bf16), pltpu.VMEM((t, hd), _bf16)],
        compiler_params=_params(("parallel", "parallel")),
        name="na_latent",
    )(z, z, z, cache_k, cache_v, bias, na_w, prev)


GQA_TQ = 256


def _gqa_lat_kernel(q_ref, k_ref, v_ref, ck_ref, cv_ref, cos_ref, sin_ref, gw_ref, _prev_ref, o_ref,
                    kr_scr, vb_scr):
    hd = HEAD_DIM
    kr_scr[...] = _bf(_rope(_rms(k_ref[...], gw_ref[1:2, :]), cos_ref[...], sin_ref[...]))
    vb_scr[...] = _bf(v_ref[...])
    ck = _bf(ck_ref[0, 0, 0])
    cv = _bf(cv_ref[0, 0, 0])
    for blk in range(q_ref.shape[0] // GQA_TQ):
        rows = slice(blk * GQA_TQ, (blk + 1) * GQA_TQ)
        for g in range(GQA_HEADS // GQA_KV_HEADS):
            cols = slice(g * hd, (g + 1) * hd)
            q = _rope(_rms(q_ref[rows, cols], gw_ref[0:1, :]), cos_ref[rows, :], sin_ref[rows, :])
            q = _bf(q * ATTN_Q_SCALE)
            s_ctx = _dot_nt(q, ck)
            s_loc = _dot_nt(q, kr_scr[...])
            o_ref[rows, cols] = _bf(_softmax_pv([s_ctx, s_loc], [cv, vb_scr[...]]))


def gqa_latent(z, cache_k, cache_v, cos, sin, gqa_w, layer, prev):
    t, hd = DEC_SEQ, HEAD_DIM
    assert t % GQA_TQ == 0
    group = GQA_HEADS // GQA_KV_HEADS
    ctx_seqs = (BATCH * SEQ) // t
    qb = Z_OFF['gqa_q'] // (group * hd)
    kb, vb = Z_OFF['gqa_k'] // hd, Z_OFF['gqa_v'] // hd
    cache_spec = pl.BlockSpec((1, 1, 1, PAST_LEN, hd), lambda b, h: (b, layer, h, 0, 0))
    return pl.pallas_call(
        _gqa_lat_kernel,
        grid=(DEC_BATCH, GQA_KV_HEADS),
        in_specs=[pl.BlockSpec((t, group * hd), lambda b, h: (ctx_seqs + b, qb + h)),
                  pl.BlockSpec((t, hd), lambda b, h: (ctx_seqs + b, kb + h)),
                  pl.BlockSpec((t, hd), lambda b, h: (ctx_seqs + b, vb + h)),
                  cache_spec, cache_spec,
                  pl.BlockSpec((t, hd), lambda b, h: (0, 0)),
                  pl.BlockSpec((t, hd), lambda b, h: (0, 0)),
                  pl.BlockSpec((2, hd), lambda b, h: (0, 0)),
                  pl.BlockSpec(memory_space=pl.ANY)],
        out_specs=pl.BlockSpec((t, group * hd), lambda b, h: (ctx_seqs + b, h)),
        out_shape=jax.ShapeDtypeStruct(prev.shape, prev.dtype),
        input_output_aliases={8: 0},
        scratch_shapes=[pltpu.VMEM((t, hd), _bf16), pltpu.VMEM((t, hd), _bf16)],
        compiler_params=_params(("parallel", "parallel")),
        name="gqa_latent",
    )(z, z, z, cache_k, cache_v, cos, sin, gqa_w, prev)


def _rope_tables():
    t = np.arange(DEC_SEQ)
    row = (t // GRID_W).astype(np.float32)
    col = (t % GRID_W).astype(np.float32)
    half = HEAD_DIM // 2
    inv = jnp.asarray(ROPE_THETA, _f32) ** (-jnp.arange(0, half, 2, dtype=_f32) / half)
    ang_r = jnp.asarray(row)[:, None] * inv[None, :]
    ang_c = jnp.asarray(col)[:, None] * inv[None, :]
    cos = jnp.concatenate([jnp.cos(ang_r), jnp.cos(ang_r), jnp.cos(ang_c), jnp.cos(ang_c)], axis=-1)
    sin = jnp.concatenate([-jnp.sin(ang_r), jnp.sin(ang_r), -jnp.sin(ang_c), jnp.sin(ang_c)], axis=-1)
    return cos, sin


def _chunk_constants():
    c = CHUNK
    t = np.arange(c)
    tri, a_cat, pair, mq, mk, causal, strict = [], [], [], [], [], [], []
    for d in range(2):
        tau = t if d == 0 else c - 1 - t
        incl = (tau[None, :] <= tau[:, None]).astype(np.float32)
        tri.append(incl)
        causal.append(incl)
        strict.append((tau[None, :] < tau[:, None]).astype(np.float32))
        rows, pm, qm, km = [], [], [], []
        for li in range(N_LEVELS):
            s = c >> (li + 1)
            ref_tau = 2 * s * (tau // (2 * s)) + s - 1
            sel = (tau[None, :] == ref_tau[:, None]).astype(np.float32)
            rows.append(incl - sel @ incl)
            odd = ((tau // s) % 2 == 1).astype(np.float32)
            pm.append(((tau[:, None] // (2 * s)) == (tau[None, :] // (2 * s))).astype(np.float32))
            qm.append(np.repeat(odd[:, None], GLA_DK, axis=1))
            km.append(np.repeat((1.0 - odd)[:, None], GLA_DK, axis=1))
        pm.append(np.eye(c, dtype=np.float32))
        rows.append(incl)
        rows.append((tau[None, :] > tau[:, None]).astype(np.float32))
        a = np.concatenate(rows, axis=0)
        a_cat.append(np.concatenate([a, a, a], axis=1))
        pair.append(np.stack(pm))
        mq.append(np.stack(qm))
        mk.append(np.stack(km))
    tri_cat = np.stack([np.concatenate([x, x, x], axis=1) for x in tri])
    sub = np.stack([pair[d][:N_LEVELS] * mq[d][:, :, :1] * np.swapaxes(mk[d][:, :, :1], 1, 2) for d in range(2)])
    return dict(sub=jnp.asarray(sub), sub_b=jnp.asarray(sub, _bf16),
                a_cat=jnp.asarray(np.stack(a_cat), _bf16), pair=jnp.asarray(np.stack(pair)),
                mq=jnp.asarray(np.stack(mq)), mk=jnp.asarray(np.stack(mk)),
                tri_cat=jnp.asarray(tri_cat, _bf16),
                causal=jnp.asarray(np.stack([np.stack([causal[d], strict[d]]) for d in range(2)])))


def _chunk_pos(d, c, n):
    return c if d == 0 else n - 1 - c


def _gla_kernel(q_ref, k_ref, v_ref, r_ref, zs_ref, wg_ref, gb_ref, acat_ref, pair_ref, mq_ref, mk_ref,
                s0_ref, nw_ref, _prev_ref, o_ref, sn_ref, la_scr, oacc_scr, s_scr, *, t):
    c, dk, dv, nh = CHUNK, GLA_DK, GLA_DV, GLA_HEADS
    n = t // c
    zs = _bf(zs_ref[...])
    for d in range(2):
        x = _dot(zs, wg_ref[d]) + gb_ref[d]
        la_scr[d] = (jnp.minimum(x, 0.0) - jnp.log(1.0 + jnp.exp(-jnp.abs(x)))) * (1.0 / GLA_TAU)
        for h in range(nh):
            s_scr[d * nh + h] = s0_ref[0, d, h]
    ones = jnp.ones((3 * c, LANES), _bf16)
    chains = [(d, h) for d in range(2) for h in range(nh)]

    def chunk(ci, carry):
        rows = [pl.ds(pl.multiple_of(_chunk_pos(d, ci, n) * c, c), c) for d in range(2)]
        states = [s_scr[d * nh + h] for d, h in chains]
        dall, blast = [], []
        for d in range(2):
            gcat = jnp.concatenate(_split3(la_scr[d, rows[d], :]), axis=0)
            dall.append(_dot(acat_ref[d], gcat))
            blast.append(_dot_tn(gcat, ones))
        qs, ks, vs = [], [], []
        for d, h in chains:
            kc = slice(h * dk, (h + 1) * dk)
            qs.append(q_ref[rows[d], kc] * (dk ** -0.5))
            ks.append(k_ref[rows[d], kc])
            vs.append(_bf(v_ref[rows[d], h * dv:(h + 1) * dv]))
        atts = [_dot_nt(_bf(q), _bf(k)) * pair_ref[d, N_LEVELS] for (d, h), q, k in zip(chains, qs, ks)]
        for li in range(N_LEVELS):
            for j, (d, h) in enumerate(chains):
                f = jnp.exp(-jnp.abs(dall[d][li * c:(li + 1) * c, h * dk:(h + 1) * dk]))
                atts[j] += _dot_nt(_bf(qs[j] * f * mq_ref[d, li]), _bf(ks[j] * f * mk_ref[d, li])) * pair_ref[d, li]
        outs, new_states = [], []
        for j, (d, h) in enumerate(chains):
            kc = slice(h * dk, (h + 1) * dk)
            eb = jnp.exp(dall[d][N_LEVELS * c:(N_LEVELS + 1) * c, kc])
            el = jnp.exp(dall[d][(N_LEVELS + 1) * c:(N_LEVELS + 2) * c, kc])
            outs.append(_dot(_bf(qs[j] * eb), _bf(states[j])) + _dot(_bf(atts[j]), vs[j]))
            new_states.append(states[j] * jnp.exp(blast[d][kc, :]) + _dot_tn(_bf(ks[j] * el), vs[j]))
        for j, (d, h) in enumerate(chains):
            s_scr[d * nh + h] = new_states[j]
            oacc_scr[d, rows[d], h * dv:(h + 1) * dv] = outs[j]
        return carry

    lax.fori_loop(0, n, chunk, 0)

    for d, h in chains:
        sn_ref[0, d, h] = s_scr[d * nh + h]
    for h in range(nh):
        vc = slice(h * dv, (h + 1) * dv)
        o_ref[:, vc] = _bf(_rms(oacc_scr[0, :, vc] + oacc_scr[1, :, vc], nw_ref[...]) * _silu(r_ref[:, vc]))


def gla(z, row_blk0, nseq, t, wg, gbias, consts, s0, out_norm, prev=None):
    qw = GLA_HEADS * GLA_DK
    vw = GLA_HEADS * GLA_DV
    if prev is None:
        prev = jnp.zeros((z.shape[0], vw), _bf16)

    def zspec(name, width):
        blk = Z_OFF[name] // width
        assert Z_OFF[name] % width == 0
        return pl.BlockSpec((t, width), lambda b: (row_blk0 + b, blk))

    def full(a):
        return pl.BlockSpec(a.shape, lambda b, nd=a.ndim: (0,) * nd)

    state_spec = pl.BlockSpec((1, 2, GLA_HEADS, GLA_DK, GLA_DV), lambda b: (b, 0, 0, 0, 0))
    cs = [consts['a_cat'], consts['pair'], consts['mq'], consts['mk']]
    return pl.pallas_call(
        functools.partial(_gla_kernel, t=t),
        grid=(nseq,),
        in_specs=[zspec('gla_q', qw), zspec('gla_k', qw), zspec('gla_v', vw), zspec('gla_r', vw),
                  zspec('small', LANES), full(wg), full(gbias)] + [full(a) for a in cs]
                 + [state_spec, pl.BlockSpec((1, GLA_DV), lambda b: (0, 0)), pl.BlockSpec(memory_space=pl.ANY)],
        out_specs=[pl.BlockSpec((t, vw), lambda b: (row_blk0 + b, 0)), state_spec],
        out_shape=[jax.ShapeDtypeStruct(prev.shape, prev.dtype),
                   jax.ShapeDtypeStruct((nseq, 2, GLA_HEADS, GLA_DK, GLA_DV), _f32)],
        input_output_aliases={7 + len(cs) + 2: 0},
        scratch_shapes=[pltpu.VMEM((2, t, qw), _f32), pltpu.VMEM((2, t, vw), _f32),
                        pltpu.VMEM((2 * GLA_HEADS, GLA_DK, GLA_DV), _f32)],
        compiler_params=_params(("parallel",)),
        name="gla",
    )(z, z, z, z, z, wg, gbias, *cs, s0, out_norm.reshape(1, GLA_DV), prev)


CONV_PAD = 8
ROW_BLK = 128
GDN_LOCAL_CHUNKS = 2


def _gdn_kernel(q_ref, k_ref, v_ref, zz_ref, zs_ref, cw_ref, alog_ref, dtb_ref, tri_ref, causal_ref, sub_ref,
                subb_ref, s0_ref, nw_ref, _prev_ref, o_ref, sn_ref,
                xp_scr, qkv_scr, g_scr, beta_scr, b_scr, u_scr, w_scr, att_scr, oacc_scr, s_scr, *, t):
    c, hd, nh = CHUNK, HEAD_DIM, GDN_HEADS
    n = t // c
    w = nh * hd
    half = CONV_K // 2
    nblk = t // ROW_BLK
    eye = (lax.broadcasted_iota(jnp.int32, (c, c), 0) == lax.broadcasted_iota(jnp.int32, (c, c), 1)).astype(_f32)

    xp_scr[0:CONV_PAD, :] = jnp.zeros((CONV_PAD, w), _f32)
    xp_scr[CONV_PAD + t:2 * CONV_PAD + t, :] = jnp.zeros((CONV_PAD, w), _f32)
    for idx, src in enumerate((q_ref, k_ref, v_ref)):
        def copy_in(i, carry, src=src):
            r0 = pl.multiple_of(i * ROW_BLK, ROW_BLK)
            xp_scr[pl.ds(CONV_PAD + r0, ROW_BLK), :] = src[pl.ds(r0, ROW_BLK), :]
            return carry

        lax.fori_loop(0, nblk, copy_in, 0)

        def conv(i, carry, idx=idx):
            r0 = pl.multiple_of(i * ROW_BLK, ROW_BLK)
            win = xp_scr[pl.ds(r0, ROW_BLK + 2 * CONV_PAD), :]
            y = jnp.zeros((ROW_BLK, w), _f32)
            for j in range(CONV_K):
                lo = CONV_PAD + j - half
                y += win[lo:lo + ROW_BLK, :] * cw_ref[idx, j:j + 1, :]
            y = _silu(y)
            for h in range(nh):
                cols = slice(h * hd, (h + 1) * hd)
                yh = y[:, cols]
                if idx == 0:
                    yh = yh * lax.rsqrt(jnp.sum(yh * yh, axis=-1, keepdims=True) + EPS) * (hd ** -0.5)
                elif idx == 1:
                    yh = yh * lax.rsqrt(jnp.sum(yh * yh, axis=-1, keepdims=True) + EPS)
                qkv_scr[idx, pl.ds(r0, ROW_BLK), cols] = yh
            return carry

        lax.fori_loop(0, nblk, conv, 0)

    zs = zs_ref[...]
    g_scr[...] = -jnp.exp(alog_ref[...]) * _softplus(zs + dtb_ref[...])
    beta_scr[...] = _sigmoid(zs)
    for d in range(2):
        for h in range(nh):
            s_scr[d * nh + h] = s0_ref[0, d, h]

    chains = [(d, h) for d in range(2) for h in range(nh)]

    def local(ci, carry):
        jobs, rows, b_all, lows, xs, rhss = [], [], [], [], [], []
        for cc in range(GDN_LOCAL_CHUNKS):
            rr = pl.ds(pl.multiple_of((ci * GDN_LOCAL_CHUNKS + cc) * c, c), c)
            rows.append(rr)
            gcat = jnp.concatenate(_split3(g_scr[rr, :]), axis=0)
            beta_all = beta_scr[rr, :]
            b_cc = [_dot(tri_ref[d], gcat) for d in range(2)]
            b_all.append(b_cc)
            b_t = [b.T for b in b_cc]
            qk, kk, ks, vs = [], [], [], []
            for h in range(nh):
                cols = slice(h * hd, (h + 1) * hd)
                k = qkv_scr[1, rr, cols]
                kbf = _bf(k)
                ks.append(k)
                vs.append(qkv_scr[2, rr, cols])
                kk.append(_dot_nt(kbf, kbf))
                qk.append(_dot_nt(_bf(qkv_scr[0, rr, cols]), kbf))
            for d, h in chains:
                ia = SMALL_A + d * nh + h
                ib = SMALL_BETA + d * nh + h
                bcol = b_cc[d][:, ia:ia + 1]
                decay = jnp.exp(jnp.minimum(bcol - b_t[d][ia:ia + 1, :], 0.0)) * causal_ref[d, 0]
                beta = beta_all[:, ib:ib + 1]
                att_scr[d * nh + h, rr, :] = _bf(qk[h] * decay)
                low = kk[h] * beta * decay * causal_ref[d, 1]
                jobs.append((cc, d, h))
                lows.append(_split2(low))
                xs.append(eye - low * sub_ref[d, N_LEVELS - 1])
                rhss.append(jnp.concatenate([vs[h] * beta, ks[h] * (beta * jnp.exp(bcol))], axis=1))
        for li in range(N_LEVELS - 2, -1, -1):
            for j, (cc, d, h) in enumerate(jobs):
                mask = subb_ref[d, li]
                xp = _split2(xs[j])
                t1 = _dot_parts(xp, (lows[j][0] * mask, lows[j][1] * mask))
                xs[j] = xs[j] - _dot_parts(_split2(t1), xp)
        for j, (cc, d, h) in enumerate(jobs):
            cols = slice(h * hd, (h + 1) * hd)
            sol = _dot_parts(_split2(xs[j]), _split2(rhss[j]))
            u_scr[d, rows[cc], cols] = sol[:, :hd]
            w_scr[d, rows[cc], cols] = _bf(sol[:, hd:])
        for cc in range(GDN_LOCAL_CHUNKS):
            for d in range(2):
                b_scr[d, rows[cc], :] = b_all[cc][d]
        return carry

    lax.fori_loop(0, n // GDN_LOCAL_CHUNKS, local, 0)

    def step(ci, carry):
        rows = [pl.ds(pl.multiple_of(_chunk_pos(d, ci, n) * c, c), c) for d in range(2)]
        states = [s_scr[d * nh + h] for d, h in chains]
        b_all = [b_scr[d, rows[d], :] for d in range(2)]
        us, ws, atts, qs, ks = [], [], [], [], []
        for d, h in chains:
            cols = slice(h * hd, (h + 1) * hd)
            us.append(u_scr[d, rows[d], cols])
            ws.append(w_scr[d, rows[d], cols])
            atts.append(att_scr[d * nh + h, rows[d], :])
            qs.append(qkv_scr[0, rows[d], cols])
            ks.append(qkv_scr[1, rows[d], cols])
        sbs = [_bf(s) for s in states]
        v_new = [_bf(u - _dot(w_, sb)) for u, w_, sb in zip(us, ws, sbs)]
        outs, new_states = [], []
        for j, (d, h) in enumerate(chains):
            ia = SMALL_A + d * nh + h
            last = c - 1 if d == 0 else 0
            bcol = b_all[d][:, ia:ia + 1]
            b_last = b_all[d][last:last + 1, ia:ia + 1]
            outs.append(_dot(_bf(qs[j] * jnp.exp(bcol)), sbs[j]) + _dot(atts[j], v_new[j]))
            new_states.append(states[j] * jnp.exp(b_last) + _dot_tn(_bf(ks[j] * jnp.exp(b_last - bcol)), v_new[j]))
        for j, (d, h) in enumerate(chains):
            s_scr[d * nh + h] = new_states[j]
            oacc_scr[d, rows[d], h * hd:(h + 1) * hd] = outs[j]
        return carry

    lax.fori_loop(0, n, step, 0)

    for d, h in chains:
        sn_ref[0, d, h] = s_scr[d * nh + h]

    def epilogue(i, carry):
        r0 = pl.multiple_of(i * ROW_BLK, ROW_BLK)
        rr = pl.ds(r0, ROW_BLK)
        for h in range(nh):
            cols = slice(h * hd, (h + 1) * hd)
            o = oacc_scr[0, rr, cols] + oacc_scr[1, rr, cols]
            o_ref[rr, cols] = _bf(_rms(o, nw_ref[...]) * _silu(zz_ref[rr, cols]))
        return carry

    lax.fori_loop(0, nblk, epilogue, 0)


def gdn(z, row_blk0, nseq, t, conv_w, alog_lane, dtb_lane, consts, s0, out_norm, prev=None):
    w = GDN_HEADS * HEAD_DIM
    if prev is None:
        prev = jnp.zeros((z.shape[0], w), _bf16)
    qblk = Z_OFF['gdn_qkv'] // w
    assert Z_OFF['gdn_qkv'] % w == 0 and Z_OFF['gdn_z'] % w == 0 and Z_OFF['small'] % LANES == 0

    def zspec(blk, width):
        return pl.BlockSpec((t, width), lambda b: (row_blk0 + b, blk))

    def full(a):
        return pl.BlockSpec(a.shape, lambda b, nd=a.ndim: (0,) * nd)

    state_spec = pl.BlockSpec((1, 2, GDN_HEADS, GDN_DK, GDN_DV), lambda b: (b, 0, 0, 0, 0))
    return pl.pallas_call(
        functools.partial(_gdn_kernel, t=t),
        grid=(nseq,),
        in_specs=[zspec(qblk, w), zspec(qblk + 1, w), zspec(qblk + 2, w), zspec(Z_OFF['gdn_z'] // w, w),
                  zspec(Z_OFF['small'] // LANES, LANES), full(conv_w), full(alog_lane), full(dtb_lane),
                  full(consts['tri_cat']), full(consts['causal']), full(consts['sub']), full(consts['sub_b']),
                  state_spec, pl.BlockSpec((1, GDN_DV), lambda b: (0, 0)), pl.BlockSpec(memory_space=pl.ANY)],
        out_specs=[pl.BlockSpec((t, w), lambda b: (row_blk0 + b, 0)), state_spec],
        out_shape=[jax.ShapeDtypeStruct(prev.shape, prev.dtype),
                   jax.ShapeDtypeStruct((nseq, 2, GDN_HEADS, GDN_DK, GDN_DV), _f32)],
        input_output_aliases={14: 0},
        scratch_shapes=[pltpu.VMEM((t + 2 * CONV_PAD, w), _f32),
                        pltpu.VMEM((3, t, w), _f32),
                        pltpu.VMEM((t, LANES), _f32),
                        pltpu.VMEM((t, LANES), _f32),
                        pltpu.VMEM((2, t, LANES), _f32),
                        pltpu.VMEM((2, t, w), _f32),
                        pltpu.VMEM((2, t, w), _bf16),
                        pltpu.VMEM((2 * GDN_HEADS, t, CHUNK), _bf16),
                        pltpu.VMEM((2, t, w), _f32),
                        pltpu.VMEM((2 * GDN_HEADS, GDN_DK, GDN_DV), _f32)],
        compiler_params=_params(("parallel",)),
        name="gdn",
    )(z, z, z, z, z, conv_w, alog_lane, dtb_lane, consts['tri_cat'], consts['causal'], consts['sub'],
      consts['sub_b'], s0, out_norm.reshape(1, GDN_DV), prev)


def _column_runs():
    runs, start, n = [], 0, len(Z_PERM)
    while start < n:
        stop = start + 1
        while (stop < n and stop - start < CAST_CHUNK and stop % Z_TILE != 0
               and (Z_PERM[stop] == Z_PERM[stop - 1] + 1 if Z_PERM[start] >= 0 else Z_PERM[stop] < 0)):
            stop += 1
        runs.append((start, int(Z_PERM[start]), stop - start))
        start = stop
    return runs


def _permute_kernel(w_ref, o_ref):
    for dst, src, width in _column_runs():
        t, c = divmod(dst, Z_TILE)
        if src < 0:
            o_ref[0, t, :, c:c + width] = jnp.zeros((CAST_BLK, width), _bf16)
        else:
            o_ref[0, t, :, c:c + width] = _bf(w_ref[0, :, src:src + width])


def _permute_columns(w_in):
    depth, d, cols = w_in.shape
    nj = Z_COLS // Z_TILE
    return pl.pallas_call(
        _permute_kernel,
        grid=(depth, d // CAST_BLK),
        in_specs=[pl.BlockSpec((1, CAST_BLK, cols), lambda l, r: (l, r, 0))],
        out_specs=pl.BlockSpec((1, nj, CAST_BLK, Z_TILE), lambda l, r: (l, 0, r, 0)),
        out_shape=jax.ShapeDtypeStruct((depth, nj, d, Z_TILE), _bf16),
        compiler_params=_params(("parallel", "parallel")),
        name="permute_w_in",
    )(w_in)


def _lane_vector(values, offset):
    k = values.shape[-1]
    return jnp.pad(values.astype(_f32), ((0, 0), (offset, LANES - offset - k)))[:, None, :]


def kernel(x_prompt, x_sample, cache_na_k, cache_na_v, cache_gqa_k, cache_gqa_v, state_gla, state_gdn, c, c_ctx, norm_g, w_mod, b_mod, ffn_gu, ffn_down, w_in, w_out, na_qk_norm, na_rpb, gla_gate_up, gla_gate_bias, gla_out_norm, gqa_qk_norm, gdn_conv, gdn_a_log, gdn_dt_bias, gdn_out_norm):
    assert GRID_ROWS % NA_QROWS == 0 and GRID_ROWS >= NA_KROWS and PAST_LEN == SEQ
    d = D_MODEL
    m_ctx = BATCH * SEQ
    assert m_ctx % DEC_SEQ == 0 and DEC_SEQ % TOKEN_TILE == 0
    x = jnp.concatenate([x_prompt.reshape(m_ctx, d), x_sample.reshape(DEC_BATCH * DEC_SEQ, d)], axis=0)

    n_cond = 1 + DEC_BATCH
    cond = jnp.concatenate([c_ctx[None, :], c], axis=0)
    cond = jnp.pad(cond, ((0, (-n_cond) % 8), (0, 0)))
    mod = adaln_all(cond, w_mod, b_mod).reshape(DEPTH, cond.shape[0], N_MOD, d)

    w_gate, w_up, down = cast_ffn_weights(ffn_gu, ffn_down)
    w_in_p = _permute_columns(w_in)
    w_out_b = w_out.astype(_bf16)

    consts = _chunk_constants()
    cos, sin = _rope_tables()
    bias = na_bias_tables(na_rpb)

    qw = GLA_HEADS * GLA_DK
    wg = jnp.zeros((DEPTH, 2, LANES, qw), _f32)
    wg = wg.at[:, 0, 0:GLA_GATE_RANK].set(gla_gate_up[:, 0])
    wg = wg.at[:, 1, SMALL_GB:SMALL_GB + GLA_GATE_RANK].set(gla_gate_up[:, 1]).astype(_bf16)
    gbias = gla_gate_bias.reshape(DEPTH, 2, 1, qw)
    conv_w = gdn_conv.reshape(DEPTH, CONV_K, 3, GDN_HEADS * HEAD_DIM).transpose(0, 2, 1, 3)
    conv_w = jnp.pad(conv_w, ((0, 0), (0, 0), (0, 8 - CONV_K), (0, 0)))
    alog_lane = _lane_vector(gdn_a_log.reshape(DEPTH, 2 * GDN_HEADS), SMALL_A)
    dtb_lane = _lane_vector(gdn_dt_bias.reshape(DEPTH, 2 * GDN_HEADS), SMALL_A)
    gla_zero = jnp.zeros((BATCH, 2, GLA_HEADS, GLA_DK, GLA_DV), _f32)
    gdn_zero = jnp.zeros((BATCH, 2, GDN_HEADS, GDN_DK, GDN_DV), _f32)
    lat_blk0 = m_ctx // DEC_SEQ

    gla_l, gdn_l = [], []
    caches = None
    for l in range(DEPTH):
        x = ffn(x, mod[l], norm_g[l, 0], w_gate, w_up, down, 2 * l, 0)
        z = in_proj(x, mod[l], norm_g[l, 1], w_in_p, l)

        o_na, o_gqa, caches = ctx_attention(z, na_qk_norm[l], gqa_qk_norm[l], l, caches)
        o_na = na_latent(z, cache_na_k, cache_na_v, bias, na_qk_norm[l], l, o_na)
        o_gqa = gqa_latent(z, cache_gqa_k, cache_gqa_v, cos, sin, gqa_qk_norm[l], l, o_gqa)
        o_gla, sg = gla(z, 0, BATCH, SEQ, wg[l], gbias[l], consts, gla_zero, gla_out_norm[l])
        o_gla, _ = gla(z, lat_blk0, DEC_BATCH, DEC_SEQ, wg[l], gbias[l], consts, state_gla[:, l],
                       gla_out_norm[l], o_gla)
        o_gdn, sd = gdn(z, 0, BATCH, SEQ, conv_w[l], alog_lane[l], dtb_lane[l], consts, gdn_zero,
                        gdn_out_norm[l])
        o_gdn, _ = gdn(z, lat_blk0, DEC_BATCH, DEC_SEQ, conv_w[l], alog_lane[l], dtb_lane[l], consts,
                       state_gdn[:, l], gdn_out_norm[l], o_gdn)

        x = out_proj(x, mod[l], o_na, o_gla, o_gqa, o_gdn, w_out_b, l)
        x = ffn(x, mod[l], norm_g[l, 2], w_gate, w_up, down, 2 * l + 1, 6)

        gla_l.append(sg)
        gdn_l.append(sd)

    y_prompt = x[:m_ctx].reshape(BATCH, SEQ, d)
    y_sample = x[m_ctx:].reshape(DEC_BATCH, DEC_SEQ, d)
    return (y_prompt, y_sample) + caches + (jnp.stack(gla_l, axis=1), jnp.stack(gdn_l, axis=1))
```

```python
import functools
import math

import jax
import jax.numpy as jnp
import numpy as np
from jax import lax
from jax.experimental import pallas as pl
from jax.experimental.pallas import tpu as pltpu

D_MODEL = 2048
BATCH = 16
SEQ = 256
DEPTH = 4
DEC_BATCH = 8
DEC_SEQ = 1024
PAST_LEN = 256

GRID_W = 64
HEAD_DIM = 128
GROUP_WIDTH = D_MODEL // 4
NA_HEADS = GROUP_WIDTH // HEAD_DIM
NA_WIN_ROWS = 8
NA_WIN_COLS = 16
GLA_HEADS = 4
GLA_DV = GROUP_WIDTH // GLA_HEADS
GLA_DK = GLA_DV // 2
GLA_GATE_RANK = 16
GLA_TAU = 16.0
GQA_HEADS = GROUP_WIDTH // HEAD_DIM
GQA_KV_HEADS = GQA_HEADS // 2
GDN_HEADS = GROUP_WIDTH // HEAD_DIM
GDN_DK = HEAD_DIM
GDN_DV = HEAD_DIM
CONV_K = 5
CHUNK = 64
ROPE_THETA = 10000.0
FFN_DIM = ((8 * D_MODEL // 3 + 127) // 128) * 128
N_MOD = 9
EPS = 1e-6
NEG_INF = -1e30

IN_SPLITS = (
    ('na_q', NA_HEADS * HEAD_DIM), ('na_k', NA_HEADS * HEAD_DIM), ('na_v', NA_HEADS * HEAD_DIM),
    ('gla_q', GLA_HEADS * GLA_DK), ('gla_k', GLA_HEADS * GLA_DK), ('gla_v', GLA_HEADS * GLA_DV),
    ('gla_r', GLA_HEADS * GLA_DV), ('gla_gf', GLA_GATE_RANK), ('gla_gb', GLA_GATE_RANK),
    ('gqa_q', GQA_HEADS * HEAD_DIM), ('gqa_k', GQA_KV_HEADS * HEAD_DIM), ('gqa_v', GQA_KV_HEADS * HEAD_DIM),
    ('gdn_qkv', 3 * GDN_HEADS * HEAD_DIM), ('gdn_z', GDN_HEADS * GDN_DV),
    ('gdn_b', 2 * GDN_HEADS), ('gdn_a', 2 * GDN_HEADS),
)

LANES = 128
FFN_TILE = 512
FFN_PAD = ((FFN_DIM + FFN_TILE - 1) // FFN_TILE) * FFN_TILE
TOKEN_TILE = 512
INPROJ_TOKEN_TILE = 1024
FFN_TOKEN_TILE = 512
VMEM_LIMIT = 56 * 1024 * 1024

Z_ORDER = ('na_q', 'na_k', 'na_v', 'gla_q', 'gla_k', 'gla_v', 'gla_r', 'gqa_q', 'gqa_k', 'gqa_v',
           'gdn_qkv', 'gdn_z')
SMALL_ORDER = ('gla_gf', 'gla_gb', 'gdn_b', 'gdn_a')
SMALL_GB = GLA_GATE_RANK
SMALL_BETA = 2 * GLA_GATE_RANK
SMALL_A = SMALL_BETA + 2 * GDN_HEADS
N_LEVELS = int(math.log2(CHUNK))


def _layout():
    src, off = {}, 0
    for name, width in IN_SPLITS:
        src[name] = (off, width)
        off += width
    perm, zoff, pos = [], {}, 0
    for name in Z_ORDER:
        o, w = src[name]
        zoff[name] = pos
        perm.extend(range(o, o + w))
        pos += w
    zoff['small'] = pos
    n_small = 0
    for name in SMALL_ORDER:
        o, w = src[name]
        perm.extend(range(o, o + w))
        n_small += w
    perm.extend([-1] * (LANES - n_small))
    pos += LANES
    perm.extend([-1] * ((-pos) % Z_TILE))
    return np.asarray(perm, np.int32), zoff, len(perm)


Z_TILE = 5 * 256
Z_PERM, Z_OFF, Z_COLS = _layout()

_f32 = jnp.float32
_bf16 = jnp.bfloat16


def _dot(a, b):
    return jnp.dot(a, b, preferred_element_type=_f32)


def _dot_nt(a, b):
    return lax.dot_general(a, b, (((1,), (1,)), ((), ())), preferred_element_type=_f32)


def _dot_tn(a, b):
    return lax.dot_general(a, b, (((0,), (0,)), ((), ())), preferred_element_type=_f32)


def _bf(x):
    return x.astype(_bf16)


def _sigmoid(x):
    return 1.0 / (1.0 + jnp.exp(-x))


def _silu(x):
    return x * _sigmoid(x)


def _softplus(x):
    return jnp.maximum(x, 0.0) + jnp.log(1.0 + jnp.exp(-jnp.abs(x)))


def _rms(x, w):
    return x * lax.rsqrt(jnp.mean(x * x, axis=-1, keepdims=True) + EPS) * w


def _split3(x):
    hi = _bf(x)
    r1 = x - hi.astype(_f32)
    mid = _bf(r1)
    lo = _bf(r1 - mid.astype(_f32))
    return hi, mid, lo


def _split2(x):
    hi = _bf(x)
    return hi, _bf(x - hi.astype(_f32))


def _dot_parts(a, b):
    return _dot(jnp.concatenate([a[0], a[0], a[1]], axis=1), jnp.concatenate([b[0], b[1], b[0]], axis=0))


def _cond_row(i, tile):
    n_ctx = (BATCH * SEQ) // tile
    return jnp.where(i < n_ctx, 0, 1 + (i - n_ctx) // (DEC_SEQ // tile))


def _params(sem):
    return pltpu.CompilerParams(dimension_semantics=sem, vmem_limit_bytes=VMEM_LIMIT)


def _adaln_kernel(c_ref, w_ref, b_ref, o_ref):
    a = _bf(_silu(c_ref[...]))
    o_ref[0] = _dot(a, _bf(w_ref[0])) + b_ref[0]


def adaln_all(cond, w_mod, b_mod):
    nc = cond.shape[0]
    tn = 1024
    n = N_MOD * D_MODEL
    return pl.pallas_call(
        _adaln_kernel,
        grid=(DEPTH, n // tn),
        in_specs=[pl.BlockSpec((nc, D_MODEL), lambda l, j: (0, 0)),
                  pl.BlockSpec((1, D_MODEL, tn), lambda l, j: (l, 0, j)),
                  pl.BlockSpec((1, 1, tn), lambda l, j: (l, 0, j))],
        out_specs=pl.BlockSpec((1, nc, tn), lambda l, j: (l, 0, j)),
        out_shape=jax.ShapeDtypeStruct((DEPTH, nc, n), _f32),
        compiler_params=_params(("parallel", "parallel")),
        name="adaln",
    )(cond, w_mod, b_mod.reshape(DEPTH, 1, n))


CAST_BLK = 256
CAST_CHUNK = 1024


def _cast_gu_kernel(w_ref, g_ref, u_ref):
    f = FFN_DIM
    for t in range(FFN_PAD // FFN_TILE):
        c0 = t * FFN_TILE
        n_real = max(0, min(FFN_TILE, f - c0))
        if n_real:
            g_ref[0, t, :, :n_real] = _bf(w_ref[0, :, c0:c0 + n_real])
            u_ref[0, t, :, :n_real] = _bf(w_ref[0, :, f + c0:f + c0 + n_real])
        if n_real < FFN_TILE:
            g_ref[0, t, :, n_real:] = jnp.zeros((CAST_BLK, FFN_TILE - n_real), _bf16)
            u_ref[0, t, :, n_real:] = jnp.zeros((CAST_BLK, FFN_TILE - n_real), _bf16)


def _cast_down_kernel(w_ref, o_ref):
    f = FFN_DIM
    for r0 in range(0, f, CAST_CHUNK):
        r1 = min(r0 + CAST_CHUNK, f)
        o_ref[0, r0:r1, :] = _bf(w_ref[0, r0:r1, :])
    if FFN_PAD > f:
        o_ref[0, f:, :] = jnp.zeros((FFN_PAD - f, CAST_BLK), _bf16)


def cast_ffn_weights(ffn_gu, ffn_down):
    d, f = D_MODEL, FFN_DIM
    n = ffn_gu.shape[0] * ffn_gu.shape[1]
    nf = FFN_PAD // FFN_TILE
    gate, up = pl.pallas_call(
        _cast_gu_kernel,
        grid=(n, d // CAST_BLK),
        in_specs=[pl.BlockSpec((1, CAST_BLK, 2 * f), lambda i, r: (i, r, 0))],
        out_specs=[pl.BlockSpec((1, nf, CAST_BLK, FFN_TILE), lambda i, r: (i, 0, r, 0))] * 2,
        out_shape=[jax.ShapeDtypeStruct((n, nf, d, FFN_TILE), _bf16)] * 2,
        compiler_params=_params(("parallel", "parallel")),
        name="cast_gate_up",
    )(ffn_gu.reshape(n, d, 2 * f))
    down = pl.pallas_call(
        _cast_down_kernel,
        grid=(n, d // CAST_BLK),
        in_specs=[pl.BlockSpec((1, f, CAST_BLK), lambda i, r: (i, 0, r))],
        out_specs=pl.BlockSpec((1, FFN_PAD, CAST_BLK), lambda i, r: (i, 0, r)),
        out_shape=jax.ShapeDtypeStruct((n, FFN_PAD, d), _bf16),
        compiler_params=_params(("parallel", "parallel")),
        name="cast_down",
    )(ffn_down.reshape(n, f, d))
    return gate, up, down


def _modulated_norm(x, g, m_ref, base):
    sh = m_ref[0, base:base + 1, :]
    sc = m_ref[0, base + 1:base + 2, :]
    return _rms(x, g) * (1.0 + sc) + sh


def _ffn_kernel(x_ref, xn_ref, m_ref, mn_ref, g_ref, wg_ref, wu_ref, wd_ref, o_ref, h0_scr, h1_scr, acc_scr, *,
                base, nsplit):
    i, f = pl.program_id(0), pl.program_id(1)

    @pl.when((i == 0) & (f == 0))
    def _():
        h0_scr[...] = _bf(_modulated_norm(x_ref[...], g_ref[...], m_ref, base))

    @pl.when(f == 0)
    def _():
        acc_scr[...] = jnp.zeros_like(acc_scr)

    rows_per = xn_ref.shape[0]
    rr = pl.ds(pl.multiple_of(jnp.minimum(f, nsplit - 1) * rows_per, rows_per), rows_per)

    def step(h_cur, h_nxt):
        h = h_cur[...]
        gate = _dot(h, wg_ref[...])
        up = _dot(h, wu_ref[...])
        acc_scr[...] += _dot(_bf(_silu(gate) * up), wd_ref[...])
        h_nxt[rr, :] = _bf(_modulated_norm(xn_ref[...], g_ref[...], mn_ref, base))

    @pl.when(i % 2 == 0)
    def _():
        step(h0_scr, h1_scr)

    @pl.when(i % 2 == 1)
    def _():
        step(h1_scr, h0_scr)

    @pl.when(f == pl.num_programs(1) - 1)
    def _():
        o_ref[...] = x_ref[...] + (0.5 * m_ref[0, base + 2:base + 3, :]) * acc_scr[...]


def ffn(x, mod, norm_g, w_gate, w_up, w_down, widx, base):
    m, d = x.shape
    nf = FFN_PAD // FFN_TILE
    tm = FFN_TOKEN_TILE
    nsplit = max(s for s in (1, 2, 4, 8) if s <= nf)
    xn_spec, mn_spec = _lookahead_specs(tm, d, nsplit, m // tm - 1)
    return pl.pallas_call(
        functools.partial(_ffn_kernel, base=base, nsplit=nsplit),
        grid=(m // tm, nf),
        in_specs=[pl.BlockSpec((tm, d), lambda i, f: (i, 0)), xn_spec,
                  pl.BlockSpec((1, N_MOD, d), lambda i, f: (_cond_row(i, tm), 0, 0)), mn_spec,
                  pl.BlockSpec((1, d), lambda i, f: (0, 0)),
                  pl.BlockSpec((None, None, d, FFN_TILE), lambda i, f: (widx, f, 0, 0)),
                  pl.BlockSpec((None, None, d, FFN_TILE), lambda i, f: (widx, f, 0, 0)),
                  pl.BlockSpec((None, FFN_TILE, d), lambda i, f: (widx, f, 0))],
        out_specs=pl.BlockSpec((tm, d), lambda i, f: (i, 0)),
        out_shape=jax.ShapeDtypeStruct((m, d), _f32),
        scratch_shapes=[pltpu.VMEM((tm, d), _bf16), pltpu.VMEM((tm, d), _bf16), pltpu.VMEM((tm, d), _f32)],
        compiler_params=_params(("arbitrary", "arbitrary")),
        name="ffn",
    )(x, x, mod, mod, norm_g.reshape(1, d), w_gate, w_up, w_down)


def _inproj_kernel(x_ref, xn_ref, m_ref, mn_ref, g_ref, w_ref, o_ref, h0_scr, h1_scr, *, nsplit):
    i, j = pl.program_id(0), pl.program_id(1)

    @pl.when((i == 0) & (j == 0))
    def _():
        h0_scr[...] = _bf(_modulated_norm(x_ref[...], g_ref[...], m_ref, 3))

    rows_per = xn_ref.shape[0]
    rr = pl.ds(pl.multiple_of(jnp.minimum(j, nsplit - 1) * rows_per, rows_per), rows_per)

    def step(h_cur, h_nxt):
        o_ref[...] = _dot(h_cur[...], w_ref[...])
        h_nxt[rr, :] = _bf(_modulated_norm(xn_ref[...], g_ref[...], mn_ref, 3))

    @pl.when(i % 2 == 0)
    def _():
        step(h0_scr, h1_scr)

    @pl.when(i % 2 == 1)
    def _():
        step(h1_scr, h0_scr)


def _lookahead_specs(tm, d, nsplit, last):
    rows = tm // nsplit
    return (pl.BlockSpec((rows, d), lambda i, s: (jnp.minimum(i + 1, last) * nsplit + jnp.minimum(s, nsplit - 1), 0)),
            pl.BlockSpec((1, N_MOD, d), lambda i, s: (_cond_row(jnp.minimum(i + 1, last), tm), 0, 0)))


def in_proj(x, mod, norm_g, w_in, layer):
    m, d = x.shape
    tm = INPROJ_TOKEN_TILE
    nj = Z_COLS // Z_TILE
    nsplit = max(s for s in (1, 2, 4, 8) if s <= nj)
    xn_spec, mn_spec = _lookahead_specs(tm, d, nsplit, m // tm - 1)
    return pl.pallas_call(
        functools.partial(_inproj_kernel, nsplit=nsplit),
        grid=(m // tm, nj),
        in_specs=[pl.BlockSpec((tm, d), lambda i, j: (0, 0)), xn_spec,
                  pl.BlockSpec((1, N_MOD, d), lambda i, j: (_cond_row(0, tm), 0, 0)), mn_spec,
                  pl.BlockSpec((1, d), lambda i, j: (0, 0)),
                  pl.BlockSpec((None, None, d, Z_TILE), lambda i, j: (layer, j, 0, 0))],
        out_specs=pl.BlockSpec((tm, Z_TILE), lambda i, j: (i, j)),
        out_shape=jax.ShapeDtypeStruct((m, Z_COLS), _f32),
        scratch_shapes=[pltpu.VMEM((tm, d), _bf16), pltpu.VMEM((tm, d), _bf16)],
        compiler_params=_params(("arbitrary", "arbitrary")),
        name="in_proj",
    )(x, x, mod, mod, norm_g.reshape(1, d), w_in)


def _outproj_kernel(x_ref, m_ref, a_ref, b_ref, c_ref, d_ref, w_ref, o_ref):
    gw = GROUP_WIDTH
    acc = _dot(a_ref[...], w_ref[0:gw, :])
    acc += _dot(b_ref[...], w_ref[gw:2 * gw, :])
    acc += _dot(c_ref[...], w_ref[2 * gw:3 * gw, :])
    acc += _dot(d_ref[...], w_ref[3 * gw:4 * gw, :])
    o_ref[...] = x_ref[...] + m_ref[0, 5:6, :] * acc


def out_proj(x, mod, o_na, o_gla, o_gqa, o_gdn, w_out, layer):
    m, d = x.shape
    tm = TOKEN_TILE
    grp = pl.BlockSpec((tm, GROUP_WIDTH), lambda i: (i, 0))
    return pl.pallas_call(
        _outproj_kernel,
        grid=(m // tm,),
        in_specs=[pl.BlockSpec((tm, d), lambda i: (i, 0)),
                  pl.BlockSpec((1, N_MOD, d), lambda i: (_cond_row(i, tm), 0, 0)),
                  grp, grp, grp, grp,
                  pl.BlockSpec((None, 4 * GROUP_WIDTH, d), lambda i: (layer, 0, 0))],
        out_specs=pl.BlockSpec((tm, d), lambda i: (i, 0)),
        out_shape=jax.ShapeDtypeStruct((m, d), _f32),
        compiler_params=_params(("parallel",)),
        name="out_proj",
    )(x, mod, o_na, o_gla, o_gqa, o_gdn, w_out)


LOG2E = math.log2(math.e)
ATTN_Q_SCALE = HEAD_DIM ** -0.5 * LOG2E


def _softmax_pv(scores, values):
    mx = scores[0].max(axis=-1, keepdims=True)
    for s in scores[1:]:
        mx = jnp.maximum(mx, s.max(axis=-1, keepdims=True))
    num, den = None, None
    for s, v in zip(scores, values):
        p = jnp.exp2(s - mx)
        d_ = p.sum(axis=-1, keepdims=True)
        n_ = _dot(_bf(p), v)
        num = n_ if num is None else num + n_
        den = d_ if den is None else den + d_
    return num / den


def _rope(x, cos, sin):
    lane = lax.broadcasted_iota(jnp.int32, x.shape, 1)
    quarter = HEAD_DIM // 4
    partner = jnp.where((lane % (2 * quarter)) < quarter,
                        pltpu.roll(x, HEAD_DIM - quarter, 1), pltpu.roll(x, quarter, 1))
    return x * cos + partner * sin


def _ctx_attn_kernel(nq_ref, nk_ref, nv_ref, gq_ref, gk_ref, gv_ref, nw_ref, gw_ref, *refs, n_alias):
    ona_ref, ogqa_ref, kn_ref, vn_ref, ka_ref, va_ref = refs[n_alias:]
    hd = HEAD_DIM
    for h in range(NA_HEADS):
        cols = slice(h * hd, (h + 1) * hd)
        k = _rms(nk_ref[:, cols], nw_ref[1:2, :])
        v = nv_ref[:, cols]
        q = _rms(nq_ref[:, cols], nw_ref[0:1, :]) * ATTN_Q_SCALE
        kn_ref[0, h] = k
        vn_ref[0, h] = v
        s = _dot_nt(_bf(q), _bf(k))
        ona_ref[:, cols] = _bf(_softmax_pv([s], [_bf(v)]))
    group = GQA_HEADS // GQA_KV_HEADS
    for kv in range(GQA_KV_HEADS):
        cols = slice(kv * hd, (kv + 1) * hd)
        k = _rms(gk_ref[:, cols], gw_ref[1:2, :])
        v = gv_ref[:, cols]
        ka_ref[0, kv] = k
        va_ref[0, kv] = v
        for g in range(group):
            qcols = slice((kv * group + g) * hd, (kv * group + g + 1) * hd)
            q = _rms(gq_ref[:, qcols], gw_ref[0:1, :]) * ATTN_Q_SCALE
            s = _dot_nt(_bf(q), _bf(k))
            ogqa_ref[:, qcols] = _bf(_softmax_pv([s], [_bf(v)]))


def ctx_attention(z, na_w, gqa_w, layer, caches):
    t, hd = SEQ, HEAD_DIM

    def zspec(name, width):
        blk = Z_OFF[name] // width
        assert Z_OFF[name] % width == 0
        return pl.BlockSpec((t, width), lambda b: (b, blk))

    def cache_spec(heads):
        return pl.BlockSpec((1, None, heads, t, hd), lambda b: (b, layer, 0, 0, 0))

    def cache_shape(heads):
        return jax.ShapeDtypeStruct((BATCH, DEPTH, heads, t, hd), _f32)

    gw = GROUP_WIDTH
    kvw = GQA_KV_HEADS * hd
    in_specs = [zspec('na_q', gw), zspec('na_k', gw), zspec('na_v', gw),
                zspec('gqa_q', gw), zspec('gqa_k', kvw), zspec('gqa_v', kvw),
                pl.BlockSpec((2, hd), lambda b: (0, 0)), pl.BlockSpec((2, hd), lambda b: (0, 0))]
    args = [z, z, z, z, z, z, na_w, gqa_w]
    aliases = {}
    if caches is not None:
        aliases = {len(args) + i: 2 + i for i in range(4)}
        in_specs += [pl.BlockSpec(memory_space=pl.ANY)] * 4
        args += list(caches)
    outs = pl.pallas_call(
        functools.partial(_ctx_attn_kernel, n_alias=len(aliases)),
        grid=(BATCH,),
        in_specs=in_specs,
        out_specs=[pl.BlockSpec((t, gw), lambda b: (b, 0)), pl.BlockSpec((t, gw), lambda b: (b, 0)),
                   cache_spec(NA_HEADS), cache_spec(NA_HEADS), cache_spec(GQA_KV_HEADS), cache_spec(GQA_KV_HEADS)],
        out_shape=[jax.ShapeDtypeStruct((z.shape[0], gw), _bf16), jax.ShapeDtypeStruct((z.shape[0], gw), _bf16),
                   cache_shape(NA_HEADS), cache_shape(NA_HEADS), cache_shape(GQA_KV_HEADS),
                   cache_shape(GQA_KV_HEADS)],
        input_output_aliases=aliases,
        compiler_params=_params(("parallel",)),
        name="ctx_attention",
    )(*args)
    return outs[0], outs[1], tuple(outs[2:])


NA_QROWS = 4
NA_KROWS = NA_QROWS + NA_WIN_ROWS
GRID_ROWS = DEC_SEQ // GRID_W


def _na_key_row0(blk):
    return min(max(blk * NA_QROWS - NA_WIN_ROWS // 2, 0), GRID_ROWS - NA_KROWS)


def _na_bias_kernel(rpb_ref, o_ref):
    l, h = pl.program_id(0), pl.program_id(1)
    n_dr, n_dc = 2 * NA_WIN_ROWS - 1, 2 * NA_WIN_COLS - 1
    base = (l * NA_HEADS + h) * (n_dr * n_dc)
    qc = lax.broadcasted_iota(jnp.int32, (GRID_W, GRID_W), 0)
    kc = lax.broadcasted_iota(jnp.int32, (GRID_W, GRID_W), 1)
    dc = jnp.clip(kc - qc + (NA_WIN_COLS - 1), 0, n_dc - 1)
    c0 = jnp.clip(qc - NA_WIN_COLS // 2, 0, GRID_W - NA_WIN_COLS)
    in_win = (kc >= c0) & (kc < c0 + NA_WIN_COLS)
    masked = jnp.full((GRID_W, GRID_W), NEG_INF, _f32)
    tiles = []
    for dr in range(n_dr):
        t = jnp.zeros((GRID_W, GRID_W), _f32)
        for j in range(n_dc):
            t = jnp.where(dc == j, rpb_ref[base + dr * n_dc + j], t)
        tiles.append(jnp.where(in_win, t * LOG2E, NEG_INF))
    for blk in range(GRID_ROWS // NA_QROWS):
        k0 = _na_key_row0(blk)
        for qi in range(NA_QROWS):
            r = blk * NA_QROWS + qi
            krow0 = min(max(r - NA_WIN_ROWS // 2, 0), GRID_ROWS - NA_WIN_ROWS)
            for kj in range(NA_KROWS):
                kr = k0 + kj
                ok = krow0 <= kr < krow0 + NA_WIN_ROWS
                tile = tiles[kr - r + NA_WIN_ROWS - 1] if ok else masked
                o_ref[0, 0, blk, qi * GRID_W:(qi + 1) * GRID_W, kj * GRID_W:(kj + 1) * GRID_W] = tile


def na_bias_tables(na_rpb):
    nblk = GRID_ROWS // NA_QROWS
    shape = (DEPTH, NA_HEADS, nblk, NA_QROWS * GRID_W, NA_KROWS * GRID_W)
    return pl.pallas_call(
        _na_bias_kernel,
        grid=(DEPTH, NA_HEADS),
        in_specs=[pl.BlockSpec(memory_space=pltpu.SMEM)],
        out_specs=pl.BlockSpec((1, 1) + shape[2:], lambda l, h: (l, h, 0, 0, 0)),
        out_shape=jax.ShapeDtypeStruct(shape, _f32),
        compiler_params=_params(("parallel", "parallel")),
        name="na_bias",
    )(na_rpb.reshape(-1))


def _na_lat_kernel(q_ref, k_ref, v_ref, ck_ref, cv_ref, bias_ref, nw_ref, _prev_ref, o_ref, kn_scr, vb_scr):
    kn_scr[...] = _bf(_rms(k_ref[...], nw_ref[1:2, :]))
    vb_scr[...] = _bf(v_ref[...])
    ck = _bf(ck_ref[0, 0, 0])
    cv = _bf(cv_ref[0, 0, 0])
    tq = NA_QROWS * GRID_W
    nkeys = NA_KROWS * GRID_W
    for blk in range(GRID_ROWS // NA_QROWS):
        start = _na_key_row0(blk) * GRID_W
        rows = slice(blk * tq, (blk + 1) * tq)
        q = _bf(_rms(q_ref[rows, :], nw_ref[0:1, :]) * ATTN_Q_SCALE)
        s_ctx = _dot_nt(q, ck)
        s_loc = _dot_nt(q, kn_scr[start:start + nkeys, :]) + bias_ref[0, 0, blk]
        o_ref[rows, :] = _bf(_softmax_pv([s_ctx, s_loc], [cv, vb_scr[start:start + nkeys, :]]))


def na_latent(z, cache_k, cache_v, bias, na_w, layer, prev):
    t, hd = DEC_SEQ, HEAD_DIM
    tq = NA_QROWS * GRID_W
    nblk = t // tq
    ctx_seqs = (BATCH * SEQ) // t
    qb, kb, vb = Z_OFF['na_q'] // hd, Z_OFF['na_k'] // hd, Z_OFF['na_v'] // hd
    cache_spec = pl.BlockSpec((1, 1, 1, PAST_LEN, hd), lambda b, h: (b, layer, h, 0, 0))
    return pl.pallas_call(
        _na_lat_kernel,
        grid=(DEC_BATCH, NA_HEADS),
        in_specs=[pl.BlockSpec((t, hd), lambda b, h: (ctx_seqs + b, qb + h)),
                  pl.BlockSpec((t, hd), lambda b, h: (ctx_seqs + b, kb + h)),
                  pl.BlockSpec((t, hd), lambda b, h: (ctx_seqs + b, vb + h)),
                  cache_spec, cache_spec,
                  pl.BlockSpec((1, 1, nblk, tq, NA_KROWS * GRID_W), lambda b, h: (layer, h, 0, 0, 0)),
                  pl.BlockSpec((2, hd), lambda b, h: (0, 0)),
                  pl.BlockSpec(memory_space=pl.ANY)],
        out_specs=pl.BlockSpec((t, hd), lambda b, h: (ctx_seqs + b, h)),
        out_shape=jax.ShapeDtypeStruct(prev.shape, prev.dtype),
        input_output_aliases={7: 0},
        scratch_shapes=[pltpu.VMEM((t, hd), _bf16), pltpu.VMEM((t, hd), _bf16)],
        compiler_params=_params(("parallel", "parallel")),
        name="na_latent",
    )(z, z, z, cache_k, cache_v, bias, na_w, prev)


GQA_TQ = 256


def _gqa_lat_kernel(q_ref, k_ref, v_ref, ck_ref, cv_ref, cos_ref, sin_ref, gw_ref, _prev_ref, o_ref,
                    kr_scr, vb_scr):
    hd = HEAD_DIM
    kr_scr[...] = _bf(_rope(_rms(k_ref[...], gw_ref[1:2, :]), cos_ref[...], sin_ref[...]))
    vb_scr[...] = _bf(v_ref[...])
    ck = _bf(ck_ref[0, 0, 0])
    cv = _bf(cv_ref[0, 0, 0])
    for blk in range(q_ref.shape[0] // GQA_TQ):
        rows = slice(blk * GQA_TQ, (blk + 1) * GQA_TQ)
        for g in range(GQA_HEADS // GQA_KV_HEADS):
            cols = slice(g * hd, (g + 1) * hd)
            q = _rope(_rms(q_ref[rows, cols], gw_ref[0:1, :]), cos_ref[rows, :], sin_ref[rows, :])
            q = _bf(q * ATTN_Q_SCALE)
            s_ctx = _dot_nt(q, ck)
            s_loc = _dot_nt(q, kr_scr[...])
            o_ref[rows, cols] = _bf(_softmax_pv([s_ctx, s_loc], [cv, vb_scr[...]]))


def gqa_latent(z, cache_k, cache_v, cos, sin, gqa_w, layer, prev):
    t, hd = DEC_SEQ, HEAD_DIM
    assert t % GQA_TQ == 0
    group = GQA_HEADS // GQA_KV_HEADS
    ctx_seqs = (BATCH * SEQ) // t
    qb = Z_OFF['gqa_q'] // (group * hd)
    kb, vb = Z_OFF['gqa_k'] // hd, Z_OFF['gqa_v'] // hd
    cache_spec = pl.BlockSpec((1, 1, 1, PAST_LEN, hd), lambda b, h: (b, layer, h, 0, 0))
    return pl.pallas_call(
        _gqa_lat_kernel,
        grid=(DEC_BATCH, GQA_KV_HEADS),
        in_specs=[pl.BlockSpec((t, group * hd), lambda b, h: (ctx_seqs + b, qb + h)),
                  pl.BlockSpec((t, hd), lambda b, h: (ctx_seqs + b, kb + h)),
                  pl.BlockSpec((t, hd), lambda b, h: (ctx_seqs + b, vb + h)),
                  cache_spec, cache_spec,
                  pl.BlockSpec((t, hd), lambda b, h: (0, 0)),
                  pl.BlockSpec((t, hd), lambda b, h: (0, 0)),
                  pl.BlockSpec((2, hd), lambda b, h: (0, 0)),
                  pl.BlockSpec(memory_space=pl.ANY)],
        out_specs=pl.BlockSpec((t, group * hd), lambda b, h: (ctx_seqs + b, h)),
        out_shape=jax.ShapeDtypeStruct(prev.shape, prev.dtype),
        input_output_aliases={8: 0},
        scratch_shapes=[pltpu.VMEM((t, hd), _bf16), pltpu.VMEM((t, hd), _bf16)],
        compiler_params=_params(("parallel", "parallel")),
        name="gqa_latent",
    )(z, z, z, cache_k, cache_v, cos, sin, gqa_w, prev)


def _rope_tables():
    t = np.arange(DEC_SEQ)
    row = (t // GRID_W).astype(np.float32)
    col = (t % GRID_W).astype(np.float32)
    half = HEAD_DIM // 2
    inv = jnp.asarray(ROPE_THETA, _f32) ** (-jnp.arange(0, half, 2, dtype=_f32) / half)
    ang_r = jnp.asarray(row)[:, None] * inv[None, :]
    ang_c = jnp.asarray(col)[:, None] * inv[None, :]
    cos = jnp.concatenate([jnp.cos(ang_r), jnp.cos(ang_r), jnp.cos(ang_c), jnp.cos(ang_c)], axis=-1)
    sin = jnp.concatenate([-jnp.sin(ang_r), jnp.sin(ang_r), -jnp.sin(ang_c), jnp.sin(ang_c)], axis=-1)
    return cos, sin


def _chunk_constants():
    c = CHUNK
    t = np.arange(c)
    tri, a_cat, pair, mq, mk, causal, strict = [], [], [], [], [], [], []
    for d in range(2):
        tau = t if d == 0 else c - 1 - t
        incl = (tau[None, :] <= tau[:, None]).astype(np.float32)
        tri.append(incl)
        causal.append(incl)
        strict.append((tau[None, :] < tau[:, None]).astype(np.float32))
        rows, pm, qm, km = [], [], [], []
        for li in range(N_LEVELS):
            s = c >> (li + 1)
            ref_tau = 2 * s * (tau // (2 * s)) + s - 1
            sel = (tau[None, :] == ref_tau[:, None]).astype(np.float32)
            rows.append(incl - sel @ incl)
            odd = ((tau // s) % 2 == 1).astype(np.float32)
            pm.append(((tau[:, None] // (2 * s)) == (tau[None, :] // (2 * s))).astype(np.float32))
            qm.append(np.repeat(odd[:, None], GLA_DK, axis=1))
            km.append(np.repeat((1.0 - odd)[:, None], GLA_DK, axis=1))
        pm.append(np.eye(c, dtype=np.float32))
        rows.append(incl)
        rows.append((tau[None, :] > tau[:, None]).astype(np.float32))
        a = np.concatenate(rows, axis=0)
        a_cat.append(np.concatenate([a, a, a], axis=1))
        pair.append(np.stack(pm))
        mq.append(np.stack(qm))
        mk.append(np.stack(km))
    tri_cat = np.stack([np.concatenate([x, x, x], axis=1) for x in tri])
    sub = np.stack([pair[d][:N_LEVELS] * mq[d][:, :, :1] * np.swapaxes(mk[d][:, :, :1], 1, 2) for d in range(2)])
    return dict(sub=jnp.asarray(sub), sub_b=jnp.asarray(sub, _bf16),
                a_cat=jnp.asarray(np.stack(a_cat), _bf16), pair=jnp.asarray(np.stack(pair)),
                mq=jnp.asarray(np.stack(mq)), mk=jnp.asarray(np.stack(mk)),
                tri_cat=jnp.asarray(tri_cat, _bf16),
                causal=jnp.asarray(np.stack([np.stack([causal[d], strict[d]]) for d in range(2)])))


GLA_STEP_CHUNKS = 2


def _chunk_pos(d, c, n):
    return c if d == 0 else n - 1 - c


def _gla_kernel(q_ref, k_ref, v_ref, r_ref, zs_ref, wg_ref, gb_ref, acat_ref, pair_ref, mq_ref, mk_ref,
                s0_ref, nw_ref, _prev_ref, o_ref, sn_ref, la_scr, oacc_scr, s_scr, *, t):
    c, dk, dv, nh = CHUNK, GLA_DK, GLA_DV, GLA_HEADS
    n = t // c
    zs = _bf(zs_ref[...])
    for d in range(2):
        x = _dot(zs, wg_ref[d]) + gb_ref[d]
        la_scr[d] = (jnp.minimum(x, 0.0) - jnp.log(1.0 + jnp.exp(-jnp.abs(x)))) * (1.0 / GLA_TAU)
        for h in range(nh):
            s_scr[d * nh + h] = s0_ref[0, d, h]
    ones = jnp.ones((3 * c, LANES), _bf16)
    chains = [(d, h) for d in range(2) for h in range(nh)]

    def chunk(ci, carry):
        states = [s_scr[d * nh + h] for d, h in chains]
        jobs, rows, dall, blast = [], [], [], []
        for cc in range(GLA_STEP_CHUNKS):
            rows.append([pl.ds(pl.multiple_of(_chunk_pos(d, ci * GLA_STEP_CHUNKS + cc, n) * c, c), c)
                         for d in range(2)])
            d_cc, b_cc = [], []
            for d in range(2):
                gcat = jnp.concatenate(_split3(la_scr[d, rows[cc][d], :]), axis=0)
                d_cc.append(_dot(acat_ref[d], gcat))
                b_cc.append(_dot_tn(gcat, ones))
            dall.append(d_cc)
            blast.append(b_cc)
            jobs += [(cc, d, h) for d, h in chains]
        qs, ks, vs = [], [], []
        for cc, d, h in jobs:
            kc = slice(h * dk, (h + 1) * dk)
            qs.append(q_ref[rows[cc][d], kc] * (dk ** -0.5))
            ks.append(k_ref[rows[cc][d], kc])
            vs.append(_bf(v_ref[rows[cc][d], h * dv:(h + 1) * dv]))
        atts = [_dot_nt(_bf(q), _bf(k)) * pair_ref[d, N_LEVELS] for (cc, d, h), q, k in zip(jobs, qs, ks)]
        for li in range(N_LEVELS):
            for j, (cc, d, h) in enumerate(jobs):
                f = jnp.exp(-jnp.abs(dall[cc][d][li * c:(li + 1) * c, h * dk:(h + 1) * dk]))
                atts[j] += _dot_nt(_bf(qs[j] * f * mq_ref[d, li]), _bf(ks[j] * f * mk_ref[d, li])) * pair_ref[d, li]
        qes, kes, decs, avs = [], [], [], []
        for j, (cc, d, h) in enumerate(jobs):
            kc = slice(h * dk, (h + 1) * dk)
            qes.append(_bf(qs[j] * jnp.exp(dall[cc][d][N_LEVELS * c:(N_LEVELS + 1) * c, kc])))
            kes.append(_bf(ks[j] * jnp.exp(dall[cc][d][(N_LEVELS + 1) * c:(N_LEVELS + 2) * c, kc])))
            decs.append(jnp.exp(blast[cc][d][kc, :]))
            avs.append(_dot(_bf(atts[j]), vs[j]))
        for j, (cc, d, h) in enumerate(jobs):
            s = states[d * nh + h]
            oacc_scr[d, rows[cc][d], h * dv:(h + 1) * dv] = _dot(qes[j], _bf(s)) + avs[j]
            states[d * nh + h] = s * decs[j] + _dot_tn(kes[j], vs[j])
        for d, h in chains:
            s_scr[d * nh + h] = states[d * nh + h]
        return carry

    lax.fori_loop(0, n // GLA_STEP_CHUNKS, chunk, 0)

    for d, h in chains:
        sn_ref[0, d, h] = s_scr[d * nh + h]
    for h in range(nh):
        vc = slice(h * dv, (h + 1) * dv)
        o_ref[:, vc] = _bf(_rms(oacc_scr[0, :, vc] + oacc_scr[1, :, vc], nw_ref[...]) * _silu(r_ref[:, vc]))


def gla(z, row_blk0, nseq, t, wg, gbias, consts, s0, out_norm, prev=None):
    qw = GLA_HEADS * GLA_DK
    vw = GLA_HEADS * GLA_DV
    if prev is None:
        prev = jnp.zeros((z.shape[0], vw), _bf16)

    def zspec(name, width):
        blk = Z_OFF[name] // width
        assert Z_OFF[name] % width == 0
        return pl.BlockSpec((t, width), lambda b: (row_blk0 + b, blk))

    def full(a):
        return pl.BlockSpec(a.shape, lambda b, nd=a.ndim: (0,) * nd)

    state_spec = pl.BlockSpec((1, 2, GLA_HEADS, GLA_DK, GLA_DV), lambda b: (b, 0, 0, 0, 0))
    cs = [consts['a_cat'], consts['pair'], consts['mq'], consts['mk']]
    return pl.pallas_call(
        functools.partial(_gla_kernel, t=t),
        grid=(nseq,),
        in_specs=[zspec('gla_q', qw), zspec('gla_k', qw), zspec('gla_v', vw), zspec('gla_r', vw),
                  zspec('small', LANES), full(wg), full(gbias)] + [full(a) for a in cs]
                 + [state_spec, pl.BlockSpec((1, GLA_DV), lambda b: (0, 0)), pl.BlockSpec(memory_space=pl.ANY)],
        out_specs=[pl.BlockSpec((t, vw), lambda b: (row_blk0 + b, 0)), state_spec],
        out_shape=[jax.ShapeDtypeStruct(prev.shape, prev.dtype),
                   jax.ShapeDtypeStruct((nseq, 2, GLA_HEADS, GLA_DK, GLA_DV), _f32)],
        input_output_aliases={7 + len(cs) + 2: 0},
        scratch_shapes=[pltpu.VMEM((2, t, qw), _f32), pltpu.VMEM((2, t, vw), _f32),
                        pltpu.VMEM((2 * GLA_HEADS, GLA_DK, GLA_DV), _f32)],
        compiler_params=_params(("parallel",)),
        name="gla",
    )(z, z, z, z, z, wg, gbias, *cs, s0, out_norm.reshape(1, GLA_DV), prev)


CONV_PAD = 8
ROW_BLK = 128
GDN_LOCAL_CHUNKS = 2


def _gdn_kernel(q_ref, k_ref, v_ref, zz_ref, zs_ref, cw_ref, alog_ref, dtb_ref, tri_ref, causal_ref, sub_ref,
                subb_ref, s0_ref, nw_ref, _prev_ref, o_ref, sn_ref,
                xp_scr, qkv_scr, g_scr, beta_scr, b_scr, u_scr, w_scr, att_scr, oacc_scr, s_scr, *, t):
    c, hd, nh = CHUNK, HEAD_DIM, GDN_HEADS
    n = t // c
    w = nh * hd
    half = CONV_K // 2
    nblk = t // ROW_BLK
    eye = (lax.broadcasted_iota(jnp.int32, (c, c), 0) == lax.broadcasted_iota(jnp.int32, (c, c), 1)).astype(_f32)

    xp_scr[0:CONV_PAD, :] = jnp.zeros((CONV_PAD, w), _f32)
    xp_scr[CONV_PAD + t:2 * CONV_PAD + t, :] = jnp.zeros((CONV_PAD, w), _f32)
    for idx, src in enumerate((q_ref, k_ref, v_ref)):
        def copy_in(i, carry, src=src):
            r0 = pl.multiple_of(i * ROW_BLK, ROW_BLK)
            xp_scr[pl.ds(CONV_PAD + r0, ROW_BLK), :] = src[pl.ds(r0, ROW_BLK), :]
            return carry

        lax.fori_loop(0, nblk, copy_in, 0)

        def conv(i, carry, idx=idx):
            r0 = pl.multiple_of(i * ROW_BLK, ROW_BLK)
            win = xp_scr[pl.ds(r0, ROW_BLK + 2 * CONV_PAD), :]
            y = jnp.zeros((ROW_BLK, w), _f32)
            for j in range(CONV_K):
                lo = CONV_PAD + j - half
                y += win[lo:lo + ROW_BLK, :] * cw_ref[idx, j:j + 1, :]
            y = _silu(y)
            for h in range(nh):
                cols = slice(h * hd, (h + 1) * hd)
                yh = y[:, cols]
                if idx == 0:
                    yh = yh * lax.rsqrt(jnp.sum(yh * yh, axis=-1, keepdims=True) + EPS) * (hd ** -0.5)
                elif idx == 1:
                    yh = yh * lax.rsqrt(jnp.sum(yh * yh, axis=-1, keepdims=True) + EPS)
                qkv_scr[idx, pl.ds(r0, ROW_BLK), cols] = yh
            return carry

        lax.fori_loop(0, nblk, conv, 0)

    zs = zs_ref[...]
    g_scr[...] = -jnp.exp(alog_ref[...]) * _softplus(zs + dtb_ref[...])
    beta_scr[...] = _sigmoid(zs)
    for d in range(2):
        for h in range(nh):
            s_scr[d * nh + h] = s0_ref[0, d, h]

    chains = [(d, h) for d in range(2) for h in range(nh)]

    def local(ci, carry):
        jobs, rows, b_all, lows, xs, rhss = [], [], [], [], [], []
        for cc in range(GDN_LOCAL_CHUNKS):
            rr = pl.ds(pl.multiple_of((ci * GDN_LOCAL_CHUNKS + cc) * c, c), c)
            rows.append(rr)
            gcat = jnp.concatenate(_split3(g_scr[rr, :]), axis=0)
            beta_all = beta_scr[rr, :]
            b_cc = [_dot(tri_ref[d], gcat) for d in range(2)]
            b_all.append(b_cc)
            b_t = [b.T for b in b_cc]
            qk, kk, ks, vs = [], [], [], []
            for h in range(nh):
                cols = slice(h * hd, (h + 1) * hd)
                k = qkv_scr[1, rr, cols]
                kbf = _bf(k)
                ks.append(k)
                vs.append(qkv_scr[2, rr, cols])
                kk.append(_dot_nt(kbf, kbf))
                qk.append(_dot_nt(_bf(qkv_scr[0, rr, cols]), kbf))
            for d, h in chains:
                ia = SMALL_A + d * nh + h
                ib = SMALL_BETA + d * nh + h
                bcol = b_cc[d][:, ia:ia + 1]
                decay = jnp.exp(jnp.minimum(bcol - b_t[d][ia:ia + 1, :], 0.0)) * causal_ref[d, 0]
                beta = beta_all[:, ib:ib + 1]
                att_scr[d * nh + h, rr, :] = _bf(qk[h] * decay)
                low = kk[h] * beta * decay * causal_ref[d, 1]
                jobs.append((cc, d, h))
                lows.append(_split2(low))
                xs.append(eye - low * sub_ref[d, N_LEVELS - 1])
                rhss.append(jnp.concatenate([vs[h] * beta, ks[h] * (beta * jnp.exp(bcol))], axis=1))
        for li in range(N_LEVELS - 2, -1, -1):
            for j, (cc, d, h) in enumerate(jobs):
                mask = subb_ref[d, li]
                xp = _split2(xs[j])
                t1 = _dot_parts(xp, (lows[j][0] * mask, lows[j][1] * mask))
                xs[j] = xs[j] - _dot_parts(_split2(t1), xp)
        for j, (cc, d, h) in enumerate(jobs):
            cols = slice(h * hd, (h + 1) * hd)
            sol = _dot_parts(_split2(xs[j]), _split2(rhss[j]))
            u_scr[d, rows[cc], cols] = sol[:, :hd]
            w_scr[d, rows[cc], cols] = _bf(sol[:, hd:])
        for cc in range(GDN_LOCAL_CHUNKS):
            for d in range(2):
                b_scr[d, rows[cc], :] = b_all[cc][d]
        return carry

    lax.fori_loop(0, n // GDN_LOCAL_CHUNKS, local, 0)

    def step(ci, carry):
        rows = [pl.ds(pl.multiple_of(_chunk_pos(d, ci, n) * c, c), c) for d in range(2)]
        states = [s_scr[d * nh + h] for d, h in chains]
        b_all = [b_scr[d, rows[d], :] for d in range(2)]
        us, ws, atts, qs, ks = [], [], [], [], []
        for d, h in chains:
            cols = slice(h * hd, (h + 1) * hd)
            us.append(u_scr[d, rows[d], cols])
            ws.append(w_scr[d, rows[d], cols])
            atts.append(att_scr[d * nh + h, rows[d], :])
            qs.append(qkv_scr[0, rows[d], cols])
            ks.append(qkv_scr[1, rows[d], cols])
        sbs = [_bf(s) for s in states]
        v_new = [_bf(u - _dot(w_, sb)) for u, w_, sb in zip(us, ws, sbs)]
        outs, new_states = [], []
        for j, (d, h) in enumerate(chains):
            ia = SMALL_A + d * nh + h
            last = c - 1 if d == 0 else 0
            bcol = b_all[d][:, ia:ia + 1]
            b_last = b_all[d][last:last + 1, ia:ia + 1]
            outs.append(_dot(_bf(qs[j] * jnp.exp(bcol)), sbs[j]) + _dot(atts[j], v_new[j]))
            new_states.append(states[j] * jnp.exp(b_last) + _dot_tn(_bf(ks[j] * jnp.exp(b_last - bcol)), v_new[j]))
        for j, (d, h) in enumerate(chains):
            s_scr[d * nh + h] = new_states[j]
            oacc_scr[d, rows[d], h * hd:(h + 1) * hd] = outs[j]
        return carry

    lax.fori_loop(0, n, step, 0)

    for d, h in chains:
        sn_ref[0, d, h] = s_scr[d * nh + h]

    def epilogue(i, carry):
        r0 = pl.multiple_of(i * ROW_BLK, ROW_BLK)
        rr = pl.ds(r0, ROW_BLK)
        for h in range(nh):
            cols = slice(h * hd, (h + 1) * hd)
            o = oacc_scr[0, rr, cols] + oacc_scr[1, rr, cols]
            o_ref[rr, cols] = _bf(_rms(o, nw_ref[...]) * _silu(zz_ref[rr, cols]))
        return carry

    lax.fori_loop(0, nblk, epilogue, 0)


def gdn(z, row_blk0, nseq, t, conv_w, alog_lane, dtb_lane, consts, s0, out_norm, prev=None):
    w = GDN_HEADS * HEAD_DIM
    if prev is None:
        prev = jnp.zeros((z.shape[0], w), _bf16)
    qblk = Z_OFF['gdn_qkv'] // w
    assert Z_OFF['gdn_qkv'] % w == 0 and Z_OFF['gdn_z'] % w == 0 and Z_OFF['small'] % LANES == 0

    def zspec(blk, width):
        return pl.BlockSpec((t, width), lambda b: (row_blk0 + b, blk))

    def full(a):
        return pl.BlockSpec(a.shape, lambda b, nd=a.ndim: (0,) * nd)

    state_spec = pl.BlockSpec((1, 2, GDN_HEADS, GDN_DK, GDN_DV), lambda b: (b, 0, 0, 0, 0))
    return pl.pallas_call(
        functools.partial(_gdn_kernel, t=t),
        grid=(nseq,),
        in_specs=[zspec(qblk, w), zspec(qblk + 1, w), zspec(qblk + 2, w), zspec(Z_OFF['gdn_z'] // w, w),
                  zspec(Z_OFF['small'] // LANES, LANES), full(conv_w), full(alog_lane), full(dtb_lane),
                  full(consts['tri_cat']), full(consts['causal']), full(consts['sub']), full(consts['sub_b']),
                  state_spec, pl.BlockSpec((1, GDN_DV), lambda b: (0, 0)), pl.BlockSpec(memory_space=pl.ANY)],
        out_specs=[pl.BlockSpec((t, w), lambda b: (row_blk0 + b, 0)), state_spec],
        out_shape=[jax.ShapeDtypeStruct(prev.shape, prev.dtype),
                   jax.ShapeDtypeStruct((nseq, 2, GDN_HEADS, GDN_DK, GDN_DV), _f32)],
        input_output_aliases={14: 0},
        scratch_shapes=[pltpu.VMEM((t + 2 * CONV_PAD, w), _f32),
                        pltpu.VMEM((3, t, w), _f32),
                        pltpu.VMEM((t, LANES), _f32),
                        pltpu.VMEM((t, LANES), _f32),
                        pltpu.VMEM((2, t, LANES), _f32),
                        pltpu.VMEM((2, t, w), _f32),
                        pltpu.VMEM((2, t, w), _bf16),
                        pltpu.VMEM((2 * GDN_HEADS, t, CHUNK), _bf16),
                        pltpu.VMEM((2, t, w), _f32),
                        pltpu.VMEM((2 * GDN_HEADS, GDN_DK, GDN_DV), _f32)],
        compiler_params=_params(("parallel",)),
        name="gdn",
    )(z, z, z, z, z, conv_w, alog_lane, dtb_lane, consts['tri_cat'], consts['causal'], consts['sub'],
      consts['sub_b'], s0, out_norm.reshape(1, GDN_DV), prev)


def _column_runs():
    runs, start, n = [], 0, len(Z_PERM)
    while start < n:
        stop = start + 1
        while (stop < n and stop - start < CAST_CHUNK and stop % Z_TILE != 0
               and (Z_PERM[stop] == Z_PERM[stop - 1] + 1 if Z_PERM[start] >= 0 else Z_PERM[stop] < 0)):
            stop += 1
        runs.append((start, int(Z_PERM[start]), stop - start))
        start = stop
    return runs


def _permute_kernel(w_ref, o_ref):
    for dst, src, width in _column_runs():
        t, c = divmod(dst, Z_TILE)
        if src < 0:
            o_ref[0, t, :, c:c + width] = jnp.zeros((CAST_BLK, width), _bf16)
        else:
            o_ref[0, t, :, c:c + width] = _bf(w_ref[0, :, src:src + width])


def _permute_columns(w_in):
    depth, d, cols = w_in.shape
    nj = Z_COLS // Z_TILE
    return pl.pallas_call(
        _permute_kernel,
        grid=(depth, d // CAST_BLK),
        in_specs=[pl.BlockSpec((1, CAST_BLK, cols), lambda l, r: (l, r, 0))],
        out_specs=pl.BlockSpec((1, nj, CAST_BLK, Z_TILE), lambda l, r: (l, 0, r, 0)),
        out_shape=jax.ShapeDtypeStruct((depth, nj, d, Z_TILE), _bf16),
        compiler_params=_params(("parallel", "parallel")),
        name="permute_w_in",
    )(w_in)


def _lane_vector(values, offset):
    k = values.shape[-1]
    return jnp.pad(values.astype(_f32), ((0, 0), (offset, LANES - offset - k)))[:, None, :]


def kernel(x_prompt, x_sample, cache_na_k, cache_na_v, cache_gqa_k, cache_gqa_v, state_gla, state_gdn, c, c_ctx, norm_g, w_mod, b_mod, ffn_gu, ffn_down, w_in, w_out, na_qk_norm, na_rpb, gla_gate_up, gla_gate_bias, gla_out_norm, gqa_qk_norm, gdn_conv, gdn_a_log, gdn_dt_bias, gdn_out_norm):
    assert GRID_ROWS % NA_QROWS == 0 and GRID_ROWS >= NA_KROWS and PAST_LEN == SEQ
    d = D_MODEL
    m_ctx = BATCH * SEQ
    assert m_ctx % DEC_SEQ == 0 and DEC_SEQ % TOKEN_TILE == 0
    x = jnp.concatenate([x_prompt.reshape(m_ctx, d), x_sample.reshape(DEC_BATCH * DEC_SEQ, d)], axis=0)

    n_cond = 1 + DEC_BATCH
    cond = jnp.concatenate([c_ctx[None, :], c], axis=0)
    cond = jnp.pad(cond, ((0, (-n_cond) % 8), (0, 0)))
    mod = adaln_all(cond, w_mod, b_mod).reshape(DEPTH, cond.shape[0], N_MOD, d)

    w_gate, w_up, down = cast_ffn_weights(ffn_gu, ffn_down)
    w_in_p = _permute_columns(w_in)
    w_out_b = w_out.astype(_bf16)

    consts = _chunk_constants()
    cos, sin = _rope_tables()
    bias = na_bias_tables(na_rpb)

    qw = GLA_HEADS * GLA_DK
    wg = jnp.zeros((DEPTH, 2, LANES, qw), _f32)
    wg = wg.at[:, 0, 0:GLA_GATE_RANK].set(gla_gate_up[:, 0])
    wg = wg.at[:, 1, SMALL_GB:SMALL_GB + GLA_GATE_RANK].set(gla_gate_up[:, 1]).astype(_bf16)
    gbias = gla_gate_bias.reshape(DEPTH, 2, 1, qw)
    conv_w = gdn_conv.reshape(DEPTH, CONV_K, 3, GDN_HEADS * HEAD_DIM).transpose(0, 2, 1, 3)
    conv_w = jnp.pad(conv_w, ((0, 0), (0, 0), (0, 8 - CONV_K), (0, 0)))
    alog_lane = _lane_vector(gdn_a_log.reshape(DEPTH, 2 * GDN_HEADS), SMALL_A)
    dtb_lane = _lane_vector(gdn_dt_bias.reshape(DEPTH, 2 * GDN_HEADS), SMALL_A)
    gla_zero = jnp.zeros((BATCH, 2, GLA_HEADS, GLA_DK, GLA_DV), _f32)
    gdn_zero = jnp.zeros((BATCH, 2, GDN_HEADS, GDN_DK, GDN_DV), _f32)
    lat_blk0 = m_ctx // DEC_SEQ

    gla_l, gdn_l = [], []
    caches = None
    for l in range(DEPTH):
        x = ffn(x, mod[l], norm_g[l, 0], w_gate, w_up, down, 2 * l, 0)
        z = in_proj(x, mod[l], norm_g[l, 1], w_in_p, l)

        o_na, o_gqa, caches = ctx_attention(z, na_qk_norm[l], gqa_qk_norm[l], l, caches)
        o_na = na_latent(z, cache_na_k, cache_na_v, bias, na_qk_norm[l], l, o_na)
        o_gqa = gqa_latent(z, cache_gqa_k, cache_gqa_v, cos, sin, gqa_qk_norm[l], l, o_gqa)
        o_gla, sg = gla(z, 0, BATCH, SEQ, wg[l], gbias[l], consts, gla_zero, gla_out_norm[l])
        o_gla, _ = gla(z, lat_blk0, DEC_BATCH, DEC_SEQ, wg[l], gbias[l], consts, state_gla[:, l],
                       gla_out_norm[l], o_gla)
        o_gdn, sd = gdn(z, 0, BATCH, SEQ, conv_w[l], alog_lane[l], dtb_lane[l], consts, gdn_zero,
                        gdn_out_norm[l])
        o_gdn, _ = gdn(z, lat_blk0, DEC_BATCH, DEC_SEQ, conv_w[l], alog_lane[l], dtb_lane[l], consts,
                       state_gdn[:, l], gdn_out_norm[l], o_gdn)

        x = out_proj(x, mod[l], o_na, o_gla, o_gqa, o_gdn, w_out_b, l)
        x = ffn(x, mod[l], norm_g[l, 2], w_gate, w_up, down, 2 * l + 1, 6)

        gla_l.append(sg)
        gdn_l.append(sd)

    y_prompt = x[:m_ctx].reshape(BATCH, SEQ, d)
    y_sample = x[m_ctx:].reshape(DEC_BATCH, DEC_SEQ, d)
    return (y_prompt, y_sample) + caches + (jnp.stack(gla_l, axis=1), jnp.stack(gdn_l, axis=1))
```

```python
import functools
import math

import jax
import jax.numpy as jnp
import numpy as np
from jax import lax
from jax.experimental import pallas as pl
from jax.experimental.pallas import tpu as pltpu

D_MODEL = 2048
BATCH = 16
SEQ = 256
DEPTH = 4
DEC_BATCH = 8
DEC_SEQ = 1024
PAST_LEN = 256

GRID_W = 64
HEAD_DIM = 128
GROUP_WIDTH = D_MODEL // 4
NA_HEADS = GROUP_WIDTH // HEAD_DIM
NA_WIN_ROWS = 8
NA_WIN_COLS = 16
GLA_HEADS = 4
GLA_DV = GROUP_WIDTH // GLA_HEADS
GLA_DK = GLA_DV // 2
GLA_GATE_RANK = 16
GLA_TAU = 16.0
GQA_HEADS = GROUP_WIDTH // HEAD_DIM
GQA_KV_HEADS = GQA_HEADS // 2
GDN_HEADS = GROUP_WIDTH // HEAD_DIM
GDN_DK = HEAD_DIM
GDN_DV = HEAD_DIM
CONV_K = 5
CHUNK = 64
ROPE_THETA = 10000.0
FFN_DIM = ((8 * D_MODEL // 3 + 127) // 128) * 128
N_MOD = 9
EPS = 1e-6
NEG_INF = -1e30

IN_SPLITS = (
    ('na_q', NA_HEADS * HEAD_DIM), ('na_k', NA_HEADS * HEAD_DIM), ('na_v', NA_HEADS * HEAD_DIM),
    ('gla_q', GLA_HEADS * GLA_DK), ('gla_k', GLA_HEADS * GLA_DK), ('gla_v', GLA_HEADS * GLA_DV),
    ('gla_r', GLA_HEADS * GLA_DV), ('gla_gf', GLA_GATE_RANK), ('gla_gb', GLA_GATE_RANK),
    ('gqa_q', GQA_HEADS * HEAD_DIM), ('gqa_k', GQA_KV_HEADS * HEAD_DIM), ('gqa_v', GQA_KV_HEADS * HEAD_DIM),
    ('gdn_qkv', 3 * GDN_HEADS * HEAD_DIM), ('gdn_z', GDN_HEADS * GDN_DV),
    ('gdn_b', 2 * GDN_HEADS), ('gdn_a', 2 * GDN_HEADS),
)

LANES = 128
FFN_TILE = 512
FFN_PAD = ((FFN_DIM + FFN_TILE - 1) // FFN_TILE) * FFN_TILE
TOKEN_TILE = 512
INPROJ_TOKEN_TILE = 1024
FFN_TOKEN_TILE = 512
VMEM_LIMIT = 56 * 1024 * 1024

Z_ORDER = ('na_q', 'na_k', 'na_v', 'gla_q', 'gla_k', 'gla_v', 'gla_r', 'gqa_q', 'gqa_k', 'gqa_v',
           'gdn_qkv', 'gdn_z')
SMALL_ORDER = ('gla_gf', 'gla_gb', 'gdn_b', 'gdn_a')
SMALL_GB = GLA_GATE_RANK
SMALL_BETA = 2 * GLA_GATE_RANK
SMALL_A = SMALL_BETA + 2 * GDN_HEADS
N_LEVELS = int(math.log2(CHUNK))


def _layout():
    src, off = {}, 0
    for name, width in IN_SPLITS:
        src[name] = (off, width)
        off += width
    perm, zoff, pos = [], {}, 0
    for name in Z_ORDER:
        o, w = src[name]
        zoff[name] = pos
        perm.extend(range(o, o + w))
        pos += w
    zoff['small'] = pos
    n_small = 0
    for name in SMALL_ORDER:
        o, w = src[name]
        perm.extend(range(o, o + w))
        n_small += w
    perm.extend([-1] * (LANES - n_small))
    pos += LANES
    perm.extend([-1] * ((-pos) % Z_TILE))
    return np.asarray(perm, np.int32), zoff, len(perm)


Z_TILE = 5 * 256
Z_PERM, Z_OFF, Z_COLS = _layout()

_f32 = jnp.float32
_bf16 = jnp.bfloat16


def _dot(a, b):
    return jnp.dot(a, b, preferred_element_type=_f32)


def _dot_nt(a, b):
    return lax.dot_general(a, b, (((1,), (1,)), ((), ())), preferred_element_type=_f32)


def _dot_tn(a, b):
    return lax.dot_general(a, b, (((0,), (0,)), ((), ())), preferred_element_type=_f32)


def _bf(x):
    return x.astype(_bf16)


def _sigmoid(x):
    return 1.0 / (1.0 + jnp.exp(-x))


def _silu(x):
    return x * _sigmoid(x)


def _softplus(x):
    return jnp.maximum(x, 0.0) + jnp.log(1.0 + jnp.exp(-jnp.abs(x)))


def _rms(x, w):
    return x * lax.rsqrt(jnp.mean(x * x, axis=-1, keepdims=True) + EPS) * w


def _split3(x):
    hi = _bf(x)
    r1 = x - hi.astype(_f32)
    mid = _bf(r1)
    lo = _bf(r1 - mid.astype(_f32))
    return hi, mid, lo


def _split2(x):
    hi = _bf(x)
    return hi, _bf(x - hi.astype(_f32))


def _dot_parts(a, b):
    return _dot(jnp.concatenate([a[0], a[0], a[1]], axis=1), jnp.concatenate([b[0], b[1], b[0]], axis=0))


def _cond_row(i, tile):
    n_ctx = (BATCH * SEQ) // tile
    return jnp.where(i < n_ctx, 0, 1 + (i - n_ctx) // (DEC_SEQ // tile))


def _params(sem):
    return pltpu.CompilerParams(dimension_semantics=sem, vmem_limit_bytes=VMEM_LIMIT)


def _adaln_kernel(c_ref, w_ref, b_ref, o_ref):
    a = _bf(_silu(c_ref[...]))
    o_ref[0] = _dot(a, _bf(w_ref[0])) + b_ref[0]


def adaln_all(cond, w_mod, b_mod):
    nc = cond.shape[0]
    tn = 1024
    n = N_MOD * D_MODEL
    return pl.pallas_call(
        _adaln_kernel,
        grid=(DEPTH, n // tn),
        in_specs=[pl.BlockSpec((nc, D_MODEL), lambda l, j: (0, 0)),
                  pl.BlockSpec((1, D_MODEL, tn), lambda l, j: (l, 0, j)),
                  pl.BlockSpec((1, 1, tn), lambda l, j: (l, 0, j))],
        out_specs=pl.BlockSpec((1, nc, tn), lambda l, j: (l, 0, j)),
        out_shape=jax.ShapeDtypeStruct((DEPTH, nc, n), _f32),
        compiler_params=_params(("parallel", "parallel")),
        name="adaln",
    )(cond, w_mod, b_mod.reshape(DEPTH, 1, n))


CAST_BLK = 256
CAST_CHUNK = 1024


def _cast_gu_kernel(w_ref, g_ref, u_ref):
    f = FFN_DIM
    for t in range(FFN_PAD // FFN_TILE):
        c0 = t * FFN_TILE
        n_real = max(0, min(FFN_TILE, f - c0))
        if n_real:
            g_ref[0, t, :, :n_real] = _bf(w_ref[0, :, c0:c0 + n_real])
            u_ref[0, t, :, :n_real] = _bf(w_ref[0, :, f + c0:f + c0 + n_real])
        if n_real < FFN_TILE:
            g_ref[0, t, :, n_real:] = jnp.zeros((CAST_BLK, FFN_TILE - n_real), _bf16)
            u_ref[0, t, :, n_real:] = jnp.zeros((CAST_BLK, FFN_TILE - n_real), _bf16)


def _cast_down_kernel(w_ref, o_ref):
    f = FFN_DIM
    for r0 in range(0, f, CAST_CHUNK):
        r1 = min(r0 + CAST_CHUNK, f)
        o_ref[0, r0:r1, :] = _bf(w_ref[0, r0:r1, :])
    if FFN_PAD > f:
        o_ref[0, f:, :] = jnp.zeros((FFN_PAD - f, CAST_BLK), _bf16)


def cast_ffn_weights(ffn_gu, ffn_down):
    d, f = D_MODEL, FFN_DIM
    n = ffn_gu.shape[0] * ffn_gu.shape[1]
    nf = FFN_PAD // FFN_TILE
    gate, up = pl.pallas_call(
        _cast_gu_kernel,
        grid=(n, d // CAST_BLK),
        in_specs=[pl.BlockSpec((1, CAST_BLK, 2 * f), lambda i, r: (i, r, 0))],
        out_specs=[pl.BlockSpec((1, nf, CAST_BLK, FFN_TILE), lambda i, r: (i, 0, r, 0))] * 2,
        out_shape=[jax.ShapeDtypeStruct((n, nf, d, FFN_TILE), _bf16)] * 2,
        compiler_params=_params(("parallel", "parallel")),
        name="cast_gate_up",
    )(ffn_gu.reshape(n, d, 2 * f))
    down = pl.pallas_call(
        _cast_down_kernel,
        grid=(n, d // CAST_BLK),
        in_specs=[pl.BlockSpec((1, f, CAST_BLK), lambda i, r: (i, 0, r))],
        out_specs=pl.BlockSpec((1, FFN_PAD, CAST_BLK), lambda i, r: (i, 0, r)),
        out_shape=jax.ShapeDtypeStruct((n, FFN_PAD, d), _bf16),
        compiler_params=_params(("parallel", "parallel")),
        name="cast_down",
    )(ffn_down.reshape(n, f, d))
    return gate, up, down


def _modulated_norm(x, g, m_ref, base):
    sh = m_ref[0, base:base + 1, :]
    sc = m_ref[0, base + 1:base + 2, :]
    return _rms(x, g) * (1.0 + sc) + sh


def _ffn_kernel(x_ref, xn_ref, m_ref, mn_ref, g_ref, wg_ref, wu_ref, wd_ref, o_ref, h0_scr, h1_scr, acc_scr, *,
                base, nsplit):
    i, f = pl.program_id(0), pl.program_id(1)

    @pl.when((i == 0) & (f == 0))
    def _():
        h0_scr[...] = _bf(_modulated_norm(x_ref[...], g_ref[...], m_ref, base))

    @pl.when(f == 0)
    def _():
        acc_scr[...] = jnp.zeros_like(acc_scr)

    rows_per = xn_ref.shape[0]
    rr = pl.ds(pl.multiple_of(jnp.minimum(f, nsplit - 1) * rows_per, rows_per), rows_per)

    def step(h_cur, h_nxt):
        h = h_cur[...]
        gate = _dot(h, wg_ref[...])
        up = _dot(h, wu_ref[...])
        acc_scr[...] += _dot(_bf(_silu(gate) * up), wd_ref[...])
        h_nxt[rr, :] = _bf(_modulated_norm(xn_ref[...], g_ref[...], mn_ref, base))

    @pl.when(i % 2 == 0)
    def _():
        step(h0_scr, h1_scr)

    @pl.when(i % 2 == 1)
    def _():
        step(h1_scr, h0_scr)

    @pl.when(f == pl.num_programs(1) - 1)
    def _():
        o_ref[...] = x_ref[...] + (0.5 * m_ref[0, base + 2:base + 3, :]) * acc_scr[...]


def ffn(x, mod, norm_g, w_gate, w_up, w_down, widx, base):
    m, d = x.shape
    nf = FFN_PAD // FFN_TILE
    tm = FFN_TOKEN_TILE
    nsplit = max(s for s in (1, 2, 4, 8) if s <= nf)
    xn_spec, mn_spec = _lookahead_specs(tm, d, nsplit, m // tm - 1)
    return pl.pallas_call(
        functools.partial(_ffn_kernel, base=base, nsplit=nsplit),
        grid=(m // tm, nf),
        in_specs=[pl.BlockSpec((tm, d), lambda i, f: (i, 0)), xn_spec,
                  pl.BlockSpec((1, N_MOD, d), lambda i, f: (_cond_row(i, tm), 0, 0)), mn_spec,
                  pl.BlockSpec((1, d), lambda i, f: (0, 0)),
                  pl.BlockSpec((None, None, d, FFN_TILE), lambda i, f: (widx, f, 0, 0)),
                  pl.BlockSpec((None, None, d, FFN_TILE), lambda i, f: (widx, f, 0, 0)),
                  pl.BlockSpec((None, FFN_TILE, d), lambda i, f: (widx, f, 0))],
        out_specs=pl.BlockSpec((tm, d), lambda i, f: (i, 0)),
        out_shape=jax.ShapeDtypeStruct((m, d), _f32),
        scratch_shapes=[pltpu.VMEM((tm, d), _bf16), pltpu.VMEM((tm, d), _bf16), pltpu.VMEM((tm, d), _f32)],
        compiler_params=_params(("arbitrary", "arbitrary")),
        name="ffn",
    )(x, x, mod, mod, norm_g.reshape(1, d), w_gate, w_up, w_down)


def _inproj_kernel(x_ref, xn_ref, m_ref, mn_ref, g_ref, w_ref, o_ref, h0_scr, h1_scr, *, nsplit):
    i, j = pl.program_id(0), pl.program_id(1)

    @pl.when((i == 0) & (j == 0))
    def _():
        h0_scr[...] = _bf(_modulated_norm(x_ref[...], g_ref[...], m_ref, 3))

    rows_per = xn_ref.shape[0]
    rr = pl.ds(pl.multiple_of(jnp.minimum(j, nsplit - 1) * rows_per, rows_per), rows_per)

    def step(h_cur, h_nxt):
        o_ref[...] = _dot(h_cur[...], w_ref[...])
        h_nxt[rr, :] = _bf(_modulated_norm(xn_ref[...], g_ref[...], mn_ref, 3))

    @pl.when(i % 2 == 0)
    def _():
        step(h0_scr, h1_scr)

    @pl.when(i % 2 == 1)
    def _():
        step(h1_scr, h0_scr)


def _lookahead_specs(tm, d, nsplit, last):
    rows = tm // nsplit
    return (pl.BlockSpec((rows, d), lambda i, s: (jnp.minimum(i + 1, last) * nsplit + jnp.minimum(s, nsplit - 1), 0)),
            pl.BlockSpec((1, N_MOD, d), lambda i, s: (_cond_row(jnp.minimum(i + 1, last), tm), 0, 0)))


def in_proj(x, mod, norm_g, w_in, layer):
    m, d = x.shape
    tm = INPROJ_TOKEN_TILE
    nj = Z_COLS // Z_TILE
    nsplit = max(s for s in (1, 2, 4, 8) if s <= nj)
    xn_spec, mn_spec = _lookahead_specs(tm, d, nsplit, m // tm - 1)
    return pl.pallas_call(
        functools.partial(_inproj_kernel, nsplit=nsplit),
        grid=(m // tm, nj),
        in_specs=[pl.BlockSpec((tm, d), lambda i, j: (0, 0)), xn_spec,
                  pl.BlockSpec((1, N_MOD, d), lambda i, j: (_cond_row(0, tm), 0, 0)), mn_spec,
                  pl.BlockSpec((1, d), lambda i, j: (0, 0)),
                  pl.BlockSpec((None, None, d, Z_TILE), lambda i, j: (layer, j, 0, 0))],
        out_specs=pl.BlockSpec((tm, Z_TILE), lambda i, j: (i, j)),
        out_shape=jax.ShapeDtypeStruct((m, Z_COLS), _f32),
        scratch_shapes=[pltpu.VMEM((tm, d), _bf16), pltpu.VMEM((tm, d), _bf16)],
        compiler_params=_params(("arbitrary", "arbitrary")),
        name="in_proj",
    )(x, x, mod, mod, norm_g.reshape(1, d), w_in)


def _outproj_kernel(x_ref, m_ref, a_ref, b_ref, c_ref, d_ref, w_ref, o_ref):
    gw = GROUP_WIDTH
    acc = _dot(a_ref[...], w_ref[0:gw, :])
    acc += _dot(b_ref[...], w_ref[gw:2 * gw, :])
    acc += _dot(c_ref[...], w_ref[2 * gw:3 * gw, :])
    acc += _dot(d_ref[...], w_ref[3 * gw:4 * gw, :])
    o_ref[...] = x_ref[...] + m_ref[0, 5:6, :] * acc


def out_proj(x, mod, o_na, o_gla, o_gqa, o_gdn, w_out, layer):
    m, d = x.shape
    tm = TOKEN_TILE
    grp = pl.BlockSpec((tm, GROUP_WIDTH), lambda i: (i, 0))
    return pl.pallas_call(
        _outproj_kernel,
        grid=(m // tm,),
        in_specs=[pl.BlockSpec((tm, d), lambda i: (i, 0)),
                  pl.BlockSpec((1, N_MOD, d), lambda i: (_cond_row(i, tm), 0, 0)),
                  grp, grp, grp, grp,
                  pl.BlockSpec((None, 4 * GROUP_WIDTH, d), lambda i: (layer, 0, 0))],
        out_specs=pl.BlockSpec((tm, d), lambda i: (i, 0)),
        out_shape=jax.ShapeDtypeStruct((m, d), _f32),
        compiler_params=_params(("parallel",)),
        name="out_proj",
    )(x, mod, o_na, o_gla, o_gqa, o_gdn, w_out)


LOG2E = math.log2(math.e)
ATTN_Q_SCALE = HEAD_DIM ** -0.5 * LOG2E


def _softmax_pv(scores, values):
    mx = scores[0].max(axis=-1, keepdims=True)
    for s in scores[1:]:
        mx = jnp.maximum(mx, s.max(axis=-1, keepdims=True))
    num, den = None, None
    for s, v in zip(scores, values):
        p = jnp.exp2(s - mx)
        d_ = p.sum(axis=-1, keepdims=True)
        n_ = _dot(_bf(p), v)
        num = n_ if num is None else num + n_
        den = d_ if den is None else den + d_
    return num / den


def _rope(x, cos, sin):
    lane = lax.broadcasted_iota(jnp.int32, x.shape, 1)
    quarter = HEAD_DIM // 4
    partner = jnp.where((lane % (2 * quarter)) < quarter,
                        pltpu.roll(x, HEAD_DIM - quarter, 1), pltpu.roll(x, quarter, 1))
    return x * cos + partner * sin


def _ctx_attn_kernel(nq_ref, nk_ref, nv_ref, gq_ref, gk_ref, gv_ref, nw_ref, gw_ref, *refs, n_alias):
    ona_ref, ogqa_ref, kn_ref, vn_ref, ka_ref, va_ref = refs[n_alias:]
    hd = HEAD_DIM
    for h in range(NA_HEADS):
        cols = slice(h * hd, (h + 1) * hd)
        k = _rms(nk_ref[:, cols], nw_ref[1:2, :])
        v = nv_ref[:, cols]
        q = _rms(nq_ref[:, cols], nw_ref[0:1, :]) * ATTN_Q_SCALE
        kn_ref[0, h] = k
        vn_ref[0, h] = v
        s = _dot_nt(_bf(q), _bf(k))
        ona_ref[:, cols] = _bf(_softmax_pv([s], [_bf(v)]))
    group = GQA_HEADS // GQA_KV_HEADS
    for kv in range(GQA_KV_HEADS):
        cols = slice(kv * hd, (kv + 1) * hd)
        k = _rms(gk_ref[:, cols], gw_ref[1:2, :])
        v = gv_ref[:, cols]
        ka_ref[0, kv] = k
        va_ref[0, kv] = v
        for g in range(group):
            qcols = slice((kv * group + g) * hd, (kv * group + g + 1) * hd)
            q = _rms(gq_ref[:, qcols], gw_ref[0:1, :]) * ATTN_Q_SCALE
            s = _dot_nt(_bf(q), _bf(k))
            ogqa_ref[:, qcols] = _bf(_softmax_pv([s], [_bf(v)]))


def ctx_attention(z, na_w, gqa_w, layer, caches):
    t, hd = SEQ, HEAD_DIM

    def zspec(name, width):
        blk = Z_OFF[name] // width
        assert Z_OFF[name] % width == 0
        return pl.BlockSpec((t, width), lambda b: (b, blk))

    def cache_spec(heads):
        return pl.BlockSpec((1, None, heads, t, hd), lambda b: (b, layer, 0, 0, 0))

    def cache_shape(heads):
        return jax.ShapeDtypeStruct((BATCH, DEPTH, heads, t, hd), _f32)

    gw = GROUP_WIDTH
    kvw = GQA_KV_HEADS * hd
    in_specs = [zspec('na_q', gw), zspec('na_k', gw), zspec('na_v', gw),
                zspec('gqa_q', gw), zspec('gqa_k', kvw), zspec('gqa_v', kvw),
                pl.BlockSpec((2, hd), lambda b: (0, 0)), pl.BlockSpec((2, hd), lambda b: (0, 0))]
    args = [z, z, z, z, z, z, na_w, gqa_w]
    aliases = {}
    if caches is not None:
        aliases = {len(args) + i: 2 + i for i in range(4)}
        in_specs += [pl.BlockSpec(memory_space=pl.ANY)] * 4
        args += list(caches)
    outs = pl.pallas_call(
        functools.partial(_ctx_attn_kernel, n_alias=len(aliases)),
        grid=(BATCH,),
        in_specs=in_specs,
        out_specs=[pl.BlockSpec((t, gw), lambda b: (b, 0)), pl.BlockSpec((t, gw), lambda b: (b, 0)),
                   cache_spec(NA_HEADS), cache_spec(NA_HEADS), cache_spec(GQA_KV_HEADS), cache_spec(GQA_KV_HEADS)],
        out_shape=[jax.ShapeDtypeStruct((z.shape[0], gw), _bf16), jax.ShapeDtypeStruct((z.shape[0], gw), _bf16),
                   cache_shape(NA_HEADS), cache_shape(NA_HEADS), cache_shape(GQA_KV_HEADS),
                   cache_shape(GQA_KV_HEADS)],
        input_output_aliases=aliases,
        compiler_params=_params(("parallel",)),
        name="ctx_attention",
    )(*args)
    return outs[0], outs[1], tuple(outs[2:])


NA_QROWS = 4
NA_KROWS = NA_QROWS + NA_WIN_ROWS
GRID_ROWS = DEC_SEQ // GRID_W


def _na_key_row0(blk):
    return min(max(blk * NA_QROWS - NA_WIN_ROWS // 2, 0), GRID_ROWS - NA_KROWS)


def _na_bias_kernel(rpb_ref, o_ref):
    l, h = pl.program_id(0), pl.program_id(1)
    n_dr, n_dc = 2 * NA_WIN_ROWS - 1, 2 * NA_WIN_COLS - 1
    base = (l * NA_HEADS + h) * (n_dr * n_dc)
    qc = lax.broadcasted_iota(jnp.int32, (GRID_W, GRID_W), 0)
    kc = lax.broadcasted_iota(jnp.int32, (GRID_W, GRID_W), 1)
    dc = jnp.clip(kc - qc + (NA_WIN_COLS - 1), 0, n_dc - 1)
    c0 = jnp.clip(qc - NA_WIN_COLS // 2, 0, GRID_W - NA_WIN_COLS)
    in_win = (kc >= c0) & (kc < c0 + NA_WIN_COLS)
    masked = jnp.full((GRID_W, GRID_W), NEG_INF, _f32)
    tiles = []
    for dr in range(n_dr):
        t = jnp.zeros((GRID_W, GRID_W), _f32)
        for j in range(n_dc):
            t = jnp.where(dc == j, rpb_ref[base + dr * n_dc + j], t)
        tiles.append(jnp.where(in_win, t * LOG2E, NEG_INF))
    for blk in range(GRID_ROWS // NA_QROWS):
        k0 = _na_key_row0(blk)
        for qi in range(NA_QROWS):
            r = blk * NA_QROWS + qi
            krow0 = min(max(r - NA_WIN_ROWS // 2, 0), GRID_ROWS - NA_WIN_ROWS)
            for kj in range(NA_KROWS):
                kr = k0 + kj
                ok = krow0 <= kr < krow0 + NA_WIN_ROWS
                tile = tiles[kr - r + NA_WIN_ROWS - 1] if ok else masked
                o_ref[0, 0, blk, qi * GRID_W:(qi + 1) * GRID_W, kj * GRID_W:(kj + 1) * GRID_W] = tile


def na_bias_tables(na_rpb):
    nblk = GRID_ROWS // NA_QROWS
    shape = (DEPTH, NA_HEADS, nblk, NA_QROWS * GRID_W, NA_KROWS * GRID_W)
    return pl.pallas_call(
        _na_bias_kernel,
        grid=(DEPTH, NA_HEADS),
        in_specs=[pl.BlockSpec(memory_space=pltpu.SMEM)],
        out_specs=pl.BlockSpec((1, 1) + shape[2:], lambda l, h: (l, h, 0, 0, 0)),
        out_shape=jax.ShapeDtypeStruct(shape, _f32),
        compiler_params=_params(("parallel", "parallel")),
        name="na_bias",
    )(na_rpb.reshape(-1))


def _na_lat_kernel(q_ref, k_ref, v_ref, ck_ref, cv_ref, bias_ref, nw_ref, _prev_ref, o_ref, kn_scr, vb_scr):
    kn_scr[...] = _bf(_rms(k_ref[...], nw_ref[1:2, :]))
    vb_scr[...] = _bf(v_ref[...])
    ck = _bf(ck_ref[0, 0, 0])
    cv = _bf(cv_ref[0, 0, 0])
    tq = NA_QROWS * GRID_W
    nkeys = NA_KROWS * GRID_W
    for blk in range(GRID_ROWS // NA_QROWS):
        start = _na_key_row0(blk) * GRID_W
        rows = slice(blk * tq, (blk + 1) * tq)
        q = _bf(_rms(q_ref[rows, :], nw_ref[0:1, :]) * ATTN_Q_SCALE)
        s_ctx = _dot_nt(q, ck)
        s_loc = _dot_nt(q, kn_scr[start:start + nkeys, :]) + bias_ref[0, 0, blk]
        o_ref[rows, :] = _bf(_softmax_pv([s_ctx, s_loc], [cv, vb_scr[start:start + nkeys, :]]))


def na_latent(z, cache_k, cache_v, bias, na_w, layer, prev):
    t, hd = DEC_SEQ, HEAD_DIM
    tq = NA_QROWS * GRID_W
    nblk = t // tq
    ctx_seqs = (BATCH * SEQ) // t
    qb, kb, vb = Z_OFF['na_q'] // hd, Z_OFF['na_k'] // hd, Z_OFF['na_v'] // hd
    cache_spec = pl.BlockSpec((1, 1, 1, PAST_LEN, hd), lambda b, h: (b, layer, h, 0, 0))
    return pl.pallas_call(
        _na_lat_kernel,
        grid=(DEC_BATCH, NA_HEADS),
        in_specs=[pl.BlockSpec((t, hd), lambda b, h: (ctx_seqs + b, qb + h)),
                  pl.BlockSpec((t, hd), lambda b, h: (ctx_seqs + b, kb + h)),
                  pl.BlockSpec((t, hd), lambda b, h: (ctx_seqs + b, vb + h)),
                  cache_spec, cache_spec,
                  pl.BlockSpec((1, 1, nblk, tq, NA_KROWS * GRID_W), lambda b, h: (layer, h, 0, 0, 0)),
                  pl.BlockSpec((2, hd), lambda b, h: (0, 0)),
                  pl.BlockSpec(memory_space=pl.ANY)],
        out_specs=pl.BlockSpec((t, hd), lambda b, h: (ctx_seqs + b, h)),
        out_shape=jax.ShapeDtypeStruct(prev.shape, prev.dtype),
        input_output_aliases={7: 0},
        scratch_shapes=[pltpu.VMEM((t, hd), _bf16), pltpu.VMEM((t, hd), _bf16)],
        compiler_params=_params(("parallel", "parallel")),
        name="na_latent",
    )(z, z, z, cache_k, cache_v, bias, na_w, prev)


GQA_TQ = 256


def _gqa_lat_kernel(q_ref, k_ref, v_ref, ck_ref, cv_ref, cos_ref, sin_ref, gw_ref, _prev_ref, o_ref,
                    kr_scr, vb_scr):
    hd = HEAD_DIM
    kr_scr[...] = _bf(_rope(_rms(k_ref[...], gw_ref[1:2, :]), cos_ref[...], sin_ref[...]))
    vb_scr[...] = _bf(v_ref[...])
    ck = _bf(ck_ref[0, 0, 0])
    cv = _bf(cv_ref[0, 0, 0])
    for blk in range(q_ref.shape[0] // GQA_TQ):
        rows = slice(blk * GQA_TQ, (blk + 1) * GQA_TQ)
        for g in range(GQA_HEADS // GQA_KV_HEADS):
            cols = slice(g * hd, (g + 1) * hd)
            q = _rope(_rms(q_ref[rows, cols], gw_ref[0:1, :]), cos_ref[rows, :], sin_ref[rows, :])
            q = _bf(q * ATTN_Q_SCALE)
            s_ctx = _dot_nt(q, ck)
            s_loc = _dot_nt(q, kr_scr[...])
            o_ref[rows, cols] = _bf(_softmax_pv([s_ctx, s_loc], [cv, vb_scr[...]]))


def gqa_latent(z, cache_k, cache_v, cos, sin, gqa_w, layer, prev):
    t, hd = DEC_SEQ, HEAD_DIM
    assert t % GQA_TQ == 0
    group = GQA_HEADS // GQA_KV_HEADS
    ctx_seqs = (BATCH * SEQ) // t
    qb = Z_OFF['gqa_q'] // (group * hd)
    kb, vb = Z_OFF['gqa_k'] // hd, Z_OFF['gqa_v'] // hd
    cache_spec = pl.BlockSpec((1, 1, 1, PAST_LEN, hd), lambda b, h: (b, layer, h, 0, 0))
    return pl.pallas_call(
        _gqa_lat_kernel,
        grid=(DEC_BATCH, GQA_KV_HEADS),
        in_specs=[pl.BlockSpec((t, group * hd), lambda b, h: (ctx_seqs + b, qb + h)),
                  pl.BlockSpec((t, hd), lambda b, h: (ctx_seqs + b, kb + h)),
                  pl.BlockSpec((t, hd), lambda b, h: (ctx_seqs + b, vb + h)),
                  cache_spec, cache_spec,
                  pl.BlockSpec((t, hd), lambda b, h: (0, 0)),
                  pl.BlockSpec((t, hd), lambda b, h: (0, 0)),
                  pl.BlockSpec((2, hd), lambda b, h: (0, 0)),
                  pl.BlockSpec(memory_space=pl.ANY)],
        out_specs=pl.BlockSpec((t, group * hd), lambda b, h: (ctx_seqs + b, h)),
        out_shape=jax.ShapeDtypeStruct(prev.shape, prev.dtype),
        input_output_aliases={8: 0},
        scratch_shapes=[pltpu.VMEM((t, hd), _bf16), pltpu.VMEM((t, hd), _bf16)],
        compiler_params=_params(("parallel", "parallel")),
        name="gqa_latent",
    )(z, z, z, cache_k, cache_v, cos, sin, gqa_w, prev)


def _rope_tables():
    t = np.arange(DEC_SEQ)
    row = (t // GRID_W).astype(np.float32)
    col = (t % GRID_W).astype(np.float32)
    half = HEAD_DIM // 2
    inv = jnp.asarray(ROPE_THETA, _f32) ** (-jnp.arange(0, half, 2, dtype=_f32) / half)
    ang_r = jnp.asarray(row)[:, None] * inv[None, :]
    ang_c = jnp.asarray(col)[:, None] * inv[None, :]
    cos = jnp.concatenate([jnp.cos(ang_r), jnp.cos(ang_r), jnp.cos(ang_c), jnp.cos(ang_c)], axis=-1)
    sin = jnp.concatenate([-jnp.sin(ang_r), jnp.sin(ang_r), -jnp.sin(ang_c), jnp.sin(ang_c)], axis=-1)
    return cos, sin


def _chunk_constants():
    c = CHUNK
    t = np.arange(c)
    tri, a_cat, pair, mq, mk, causal, strict = [], [], [], [], [], [], []
    for d in range(2):
        tau = t if d == 0 else c - 1 - t
        incl = (tau[None, :] <= tau[:, None]).astype(np.float32)
        tri.append(incl)
        causal.append(incl)
        strict.append((tau[None, :] < tau[:, None]).astype(np.float32))
        rows, pm, qm, km = [], [], [], []
        for li in range(N_LEVELS):
            s = c >> (li + 1)
            ref_tau = 2 * s * (tau // (2 * s)) + s - 1
            sel = (tau[None, :] == ref_tau[:, None]).astype(np.float32)
            rows.append(incl - sel @ incl)
            odd = ((tau // s) % 2 == 1).astype(np.float32)
            pm.append(((tau[:, None] // (2 * s)) == (tau[None, :] // (2 * s))).astype(np.float32))
            qm.append(np.repeat(odd[:, None], GLA_DK, axis=1))
            km.append(np.repeat((1.0 - odd)[:, None], GLA_DK, axis=1))
        pm.append(np.eye(c, dtype=np.float32))
        rows.append(incl)
        rows.append((tau[None, :] > tau[:, None]).astype(np.float32))
        a = np.concatenate(rows, axis=0)
        a_cat.append(np.concatenate([a, a, a], axis=1))
        pair.append(np.stack(pm))
        mq.append(np.stack(qm))
        mk.append(np.stack(km))
    tri_cat = np.stack([np.concatenate([x, x, x], axis=1) for x in tri])
    sub = np.stack([pair[d][:N_LEVELS] * mq[d][:, :, :1] * np.swapaxes(mk[d][:, :, :1], 1, 2) for d in range(2)])
    return dict(sub=jnp.asarray(sub), sub_b=jnp.asarray(sub, _bf16),
                a_cat=jnp.asarray(np.stack(a_cat), _bf16), pair=jnp.asarray(np.stack(pair)),
                mq=jnp.asarray(np.stack(mq)), mk=jnp.asarray(np.stack(mk)),
                tri_cat=jnp.asarray(tri_cat, _bf16),
                causal=jnp.asarray(np.stack([np.stack([causal[d], strict[d]]) for d in range(2)])))


GLA_STEP_CHUNKS = 2


def _chunk_pos(d, c, n):
    return c if d == 0 else n - 1 - c


def _gla_kernel(q_ref, k_ref, v_ref, r_ref, zs_ref, wg_ref, gb_ref, acat_ref, pair_ref, mq_ref, mk_ref,
                s0_ref, nw_ref, _prev_ref, o_ref, sn_ref, la_scr, oacc_scr, s_scr, *, t):
    c, dk, dv, nh = CHUNK, GLA_DK, GLA_DV, GLA_HEADS
    n = t // c
    zs = _bf(zs_ref[...])
    for d in range(2):
        x = _dot(zs, wg_ref[d]) + gb_ref[d]
        la_scr[d] = (jnp.minimum(x, 0.0) - jnp.log(1.0 + jnp.exp(-jnp.abs(x)))) * (1.0 / GLA_TAU)
        for h in range(nh):
            s_scr[d * nh + h] = s0_ref[0, d, h]
    ones = jnp.ones((3 * c, LANES), _bf16)
    chains = [(d, h) for d in range(2) for h in range(nh)]

    def chunk(ci, carry):
        states = [s_scr[d * nh + h] for d, h in chains]
        jobs, rows, dall, blast = [], [], [], []
        for cc in range(GLA_STEP_CHUNKS):
            rows.append([pl.ds(pl.multiple_of(_chunk_pos(d, ci * GLA_STEP_CHUNKS + cc, n) * c, c), c)
                         for d in range(2)])
            d_cc, b_cc = [], []
            for d in range(2):
                gcat = jnp.concatenate(_split3(la_scr[d, rows[cc][d], :]), axis=0)
                d_cc.append(_dot(acat_ref[d], gcat))
                b_cc.append(_dot_tn(gcat, ones))
            dall.append(d_cc)
            blast.append(b_cc)
            jobs += [(cc, d, h) for d, h in chains]
        qs, ks, vs = [], [], []
        for cc, d, h in jobs:
            kc = slice(h * dk, (h + 1) * dk)
            qs.append(q_ref[rows[cc][d], kc] * (dk ** -0.5))
            ks.append(k_ref[rows[cc][d], kc])
            vs.append(_bf(v_ref[rows[cc][d], h * dv:(h + 1) * dv]))
        atts = [_dot_nt(_bf(q), _bf(k)) * pair_ref[d, N_LEVELS] for (cc, d, h), q, k in zip(jobs, qs, ks)]
        for li in range(N_LEVELS):
            for j, (cc, d, h) in enumerate(jobs):
                f = jnp.exp(-jnp.abs(dall[cc][d][li * c:(li + 1) * c, h * dk:(h + 1) * dk]))
                atts[j] += _dot_nt(_bf(qs[j] * f * mq_ref[d, li]), _bf(ks[j] * f * mk_ref[d, li])) * pair_ref[d, li]
        qes, kes, decs, avs = [], [], [], []
        for j, (cc, d, h) in enumerate(jobs):
            kc = slice(h * dk, (h + 1) * dk)
            qes.append(_bf(qs[j] * jnp.exp(dall[cc][d][N_LEVELS * c:(N_LEVELS + 1) * c, kc])))
            kes.append(_bf(ks[j] * jnp.exp(dall[cc][d][(N_LEVELS + 1) * c:(N_LEVELS + 2) * c, kc])))
            decs.append(jnp.exp(blast[cc][d][kc, :]))
            avs.append(_dot(_bf(atts[j]), vs[j]))
        for j, (cc, d, h) in enumerate(jobs):
            s = states[d * nh + h]
            oacc_scr[d, rows[cc][d], h * dv:(h + 1) * dv] = _dot(qes[j], _bf(s)) + avs[j]
            states[d * nh + h] = s * decs[j] + _dot_tn(kes[j], vs[j])
        for d, h in chains:
            s_scr[d * nh + h] = states[d * nh + h]
        return carry

    lax.fori_loop(0, n // GLA_STEP_CHUNKS, chunk, 0)

    for d, h in chains:
        sn_ref[0, d, h] = s_scr[d * nh + h]
    for h in range(nh):
        vc = slice(h * dv, (h + 1) * dv)
        o_ref[:, vc] = _bf(_rms(oacc_scr[0, :, vc] + oacc_scr[1, :, vc], nw_ref[...]) * _silu(r_ref[:, vc]))


def gla(z, row_blk0, nseq, t, wg, gbias, consts, s0, out_norm, prev=None):
    qw = GLA_HEADS * GLA_DK
    vw = GLA_HEADS * GLA_DV
    if prev is None:
        prev = jnp.zeros((z.shape[0], vw), _bf16)

    def zspec(name, width):
        blk = Z_OFF[name] // width
        assert Z_OFF[name] % width == 0
        return pl.BlockSpec((t, width), lambda b: (row_blk0 + b, blk))

    def full(a):
        return pl.BlockSpec(a.shape, lambda b, nd=a.ndim: (0,) * nd)

    state_spec = pl.BlockSpec((1, 2, GLA_HEADS, GLA_DK, GLA_DV), lambda b: (b, 0, 0, 0, 0))
    cs = [consts['a_cat'], consts['pair'], consts['mq'], consts['mk']]
    return pl.pallas_call(
        functools.partial(_gla_kernel, t=t),
        grid=(nseq,),
        in_specs=[zspec('gla_q', qw), zspec('gla_k', qw), zspec('gla_v', vw), zspec('gla_r', vw),
                  zspec('small', LANES), full(wg), full(gbias)] + [full(a) for a in cs]
                 + [state_spec, pl.BlockSpec((1, GLA_DV), lambda b: (0, 0)), pl.BlockSpec(memory_space=pl.ANY)],
        out_specs=[pl.BlockSpec((t, vw), lambda b: (row_blk0 + b, 0)), state_spec],
        out_shape=[jax.ShapeDtypeStruct(prev.shape, prev.dtype),
                   jax.ShapeDtypeStruct((nseq, 2, GLA_HEADS, GLA_DK, GLA_DV), _f32)],
        input_output_aliases={7 + len(cs) + 2: 0},
        scratch_shapes=[pltpu.VMEM((2, t, qw), _f32), pltpu.VMEM((2, t, vw), _f32),
                        pltpu.VMEM((2 * GLA_HEADS, GLA_DK, GLA_DV), _f32)],
        compiler_params=_params(("parallel",)),
        name="gla",
    )(z, z, z, z, z, wg, gbias, *cs, s0, out_norm.reshape(1, GLA_DV), prev)


CONV_PAD = 8
ROW_BLK = 128
GDN_LOCAL_CHUNKS = 4


def _gdn_kernel(q_ref, k_ref, v_ref, zz_ref, zs_ref, cw_ref, alog_ref, dtb_ref, tri_ref, causal_ref, sub_ref,
                subb_ref, s0_ref, nw_ref, _prev_ref, o_ref, sn_ref,
                xp_scr, qkv_scr, g_scr, beta_scr, b_scr, u_scr, w_scr, att_scr, oacc_scr, s_scr, *, t):
    c, hd, nh = CHUNK, HEAD_DIM, GDN_HEADS
    n = t // c
    w = nh * hd
    half = CONV_K // 2
    nblk = t // ROW_BLK
    eye = (lax.broadcasted_iota(jnp.int32, (c, c), 0) == lax.broadcasted_iota(jnp.int32, (c, c), 1)).astype(_f32)

    xp_scr[0:CONV_PAD, :] = jnp.zeros((CONV_PAD, w), _f32)
    xp_scr[CONV_PAD + t:2 * CONV_PAD + t, :] = jnp.zeros((CONV_PAD, w), _f32)
    for idx, src in enumerate((q_ref, k_ref, v_ref)):
        def copy_in(i, carry, src=src):
            r0 = pl.multiple_of(i * ROW_BLK, ROW_BLK)
            xp_scr[pl.ds(CONV_PAD + r0, ROW_BLK), :] = src[pl.ds(r0, ROW_BLK), :]
            return carry

        lax.fori_loop(0, nblk, copy_in, 0)

        def conv(i, carry, idx=idx):
            r0 = pl.multiple_of(i * ROW_BLK, ROW_BLK)
            win = xp_scr[pl.ds(r0, ROW_BLK + 2 * CONV_PAD), :]
            y = jnp.zeros((ROW_BLK, w), _f32)
            for j in range(CONV_K):
                lo = CONV_PAD + j - half
                y += win[lo:lo + ROW_BLK, :] * cw_ref[idx, j:j + 1, :]
            y = _silu(y)
            for h in range(nh):
                cols = slice(h * hd, (h + 1) * hd)
                yh = y[:, cols]
                if idx == 0:
                    yh = yh * lax.rsqrt(jnp.sum(yh * yh, axis=-1, keepdims=True) + EPS) * (hd ** -0.5)
                elif idx == 1:
                    yh = yh * lax.rsqrt(jnp.sum(yh * yh, axis=-1, keepdims=True) + EPS)
                qkv_scr[idx, pl.ds(r0, ROW_BLK), cols] = yh
            return carry

        lax.fori_loop(0, nblk, conv, 0)

    zs = zs_ref[...]
    g_scr[...] = -jnp.exp(alog_ref[...]) * _softplus(zs + dtb_ref[...])
    beta_scr[...] = _sigmoid(zs)
    for d in range(2):
        for h in range(nh):
            s_scr[d * nh + h] = s0_ref[0, d, h]

    chains = [(d, h) for d in range(2) for h in range(nh)]

    def local(ci, carry):
        jobs, rows, b_all, lows, xs, rhss = [], [], [], [], [], []
        for cc in range(GDN_LOCAL_CHUNKS):
            rr = pl.ds(pl.multiple_of((ci * GDN_LOCAL_CHUNKS + cc) * c, c), c)
            rows.append(rr)
            gcat = jnp.concatenate(_split3(g_scr[rr, :]), axis=0)
            beta_all = beta_scr[rr, :]
            b_cc = [_dot(tri_ref[d], gcat) for d in range(2)]
            b_all.append(b_cc)
            b_t = [b.T for b in b_cc]
            qk, kk, ks, vs = [], [], [], []
            for h in range(nh):
                cols = slice(h * hd, (h + 1) * hd)
                k = qkv_scr[1, rr, cols]
                kbf = _bf(k)
                ks.append(k)
                vs.append(qkv_scr[2, rr, cols])
                kk.append(_dot_nt(kbf, kbf))
                qk.append(_dot_nt(_bf(qkv_scr[0, rr, cols]), kbf))
            for d, h in chains:
                ia = SMALL_A + d * nh + h
                ib = SMALL_BETA + d * nh + h
                bcol = b_cc[d][:, ia:ia + 1]
                decay = jnp.exp(jnp.minimum(bcol - b_t[d][ia:ia + 1, :], 0.0)) * causal_ref[d, 0]
                beta = beta_all[:, ib:ib + 1]
                att_scr[d * nh + h, rr, :] = _bf(qk[h] * decay)
                low = kk[h] * beta * decay * causal_ref[d, 1]
                jobs.append((cc, d, h))
                lows.append(_split2(low))
                xs.append(eye - low * sub_ref[d, N_LEVELS - 1])
                rhss.append(jnp.concatenate([vs[h] * beta, ks[h] * (beta * jnp.exp(bcol))], axis=1))
        for li in range(N_LEVELS - 2, -1, -1):
            for j, (cc, d, h) in enumerate(jobs):
                mask = subb_ref[d, li]
                xp = _split2(xs[j])
                t1 = _dot_parts(xp, (lows[j][0] * mask, lows[j][1] * mask))
                xs[j] = xs[j] - _dot_parts(_split2(t1), xp)
        for j, (cc, d, h) in enumerate(jobs):
            cols = slice(h * hd, (h + 1) * hd)
            sol = _dot_parts(_split2(xs[j]), _split2(rhss[j]))
            u_scr[d, rows[cc], cols] = sol[:, :hd]
            w_scr[d, rows[cc], cols] = _bf(sol[:, hd:])
        for cc in range(GDN_LOCAL_CHUNKS):
            for d in range(2):
                b_scr[d, rows[cc], :] = b_all[cc][d]
        return carry

    lax.fori_loop(0, n // GDN_LOCAL_CHUNKS, local, 0)

    def step(ci, carry):
        rows = [pl.ds(pl.multiple_of(_chunk_pos(d, ci, n) * c, c), c) for d in range(2)]
        states = [s_scr[d * nh + h] for d, h in chains]
        b_all = [b_scr[d, rows[d], :] for d in range(2)]
        us, ws, atts, qs, ks = [], [], [], [], []
        for d, h in chains:
            cols = slice(h * hd, (h + 1) * hd)
            us.append(u_scr[d, rows[d], cols])
            ws.append(w_scr[d, rows[d], cols])
            atts.append(att_scr[d * nh + h, rows[d], :])
            qs.append(qkv_scr[0, rows[d], cols])
            ks.append(qkv_scr[1, rows[d], cols])
        sbs = [_bf(s) for s in states]
        v_new = [_bf(u - _dot(w_, sb)) for u, w_, sb in zip(us, ws, sbs)]
        outs, new_states = [], []
        for j, (d, h) in enumerate(chains):
            ia = SMALL_A + d * nh + h
            last = c - 1 if d == 0 else 0
            bcol = b_all[d][:, ia:ia + 1]
            b_last = b_all[d][last:last + 1, ia:ia + 1]
            outs.append(_dot(_bf(qs[j] * jnp.exp(bcol)), sbs[j]) + _dot(atts[j], v_new[j]))
            new_states.append(states[j] * jnp.exp(b_last) + _dot_tn(_bf(ks[j] * jnp.exp(b_last - bcol)), v_new[j]))
        for j, (d, h) in enumerate(chains):
            s_scr[d * nh + h] = new_states[j]
            oacc_scr[d, rows[d], h * hd:(h + 1) * hd] = outs[j]
        return carry

    lax.fori_loop(0, n, step, 0)

    for d, h in chains:
        sn_ref[0, d, h] = s_scr[d * nh + h]

    def epilogue(i, carry):
        r0 = pl.multiple_of(i * ROW_BLK, ROW_BLK)
        rr = pl.ds(r0, ROW_BLK)
        for h in range(nh):
            cols = slice(h * hd, (h + 1) * hd)
            o = oacc_scr[0, rr, cols] + oacc_scr[1, rr, cols]
            o_ref[rr, cols] = _bf(_rms(o, nw_ref[...]) * _silu(zz_ref[rr, cols]))
        return carry

    lax.fori_loop(0, nblk, epilogue, 0)


def gdn(z, row_blk0, nseq, t, conv_w, alog_lane, dtb_lane, consts, s0, out_norm, prev=None):
    w = GDN_HEADS * HEAD_DIM
    if prev is None:
        prev = jnp.zeros((z.shape[0], w), _bf16)
    qblk = Z_OFF['gdn_qkv'] // w
    assert Z_OFF['gdn_qkv'] % w == 0 and Z_OFF['gdn_z'] % w == 0 and Z_OFF['small'] % LANES == 0

    def zspec(blk, width):
        return pl.BlockSpec((t, width), lambda b: (row_blk0 + b, blk))

    def full(a):
        return pl.BlockSpec(a.shape, lambda b, nd=a.ndim: (0,) * nd)

    state_spec = pl.BlockSpec((1, 2, GDN_HEADS, GDN_DK, GDN_DV), lambda b: (b, 0, 0, 0, 0))
    return pl.pallas_call(
        functools.partial(_gdn_kernel, t=t),
        grid=(nseq,),
        in_specs=[zspec(qblk, w), zspec(qblk + 1, w), zspec(qblk + 2, w), zspec(Z_OFF['gdn_z'] // w, w),
                  zspec(Z_OFF['small'] // LANES, LANES), full(conv_w), full(alog_lane), full(dtb_lane),
                  full(consts['tri_cat']), full(consts['causal']), full(consts['sub']), full(consts['sub_b']),
                  state_spec, pl.BlockSpec((1, GDN_DV), lambda b: (0, 0)), pl.BlockSpec(memory_space=pl.ANY)],
        out_specs=[pl.BlockSpec((t, w), lambda b: (row_blk0 + b, 0)), state_spec],
        out_shape=[jax.ShapeDtypeStruct(prev.shape, prev.dtype),
                   jax.ShapeDtypeStruct((nseq, 2, GDN_HEADS, GDN_DK, GDN_DV), _f32)],
        input_output_aliases={14: 0},
        scratch_shapes=[pltpu.VMEM((t + 2 * CONV_PAD, w), _f32),
                        pltpu.VMEM((3, t, w), _f32),
                        pltpu.VMEM((t, LANES), _f32),
                        pltpu.VMEM((t, LANES), _f32),
                        pltpu.VMEM((2, t, LANES), _f32),
                        pltpu.VMEM((2, t, w), _f32),
                        pltpu.VMEM((2, t, w), _bf16),
                        pltpu.VMEM((2 * GDN_HEADS, t, CHUNK), _bf16),
                        pltpu.VMEM((2, t, w), _f32),
                        pltpu.VMEM((2 * GDN_HEADS, GDN_DK, GDN_DV), _f32)],
        compiler_params=_params(("parallel",)),
        name="gdn",
    )(z, z, z, z, z, conv_w, alog_lane, dtb_lane, consts['tri_cat'], consts['causal'], consts['sub'],
      consts['sub_b'], s0, out_norm.reshape(1, GDN_DV), prev)


def _column_runs():
    runs, start, n = [], 0, len(Z_PERM)
    while start < n:
        stop = start + 1
        while (stop < n and stop - start < CAST_CHUNK and stop % Z_TILE != 0
               and (Z_PERM[stop] == Z_PERM[stop - 1] + 1 if Z_PERM[start] >= 0 else Z_PERM[stop] < 0)):
            stop += 1
        runs.append((start, int(Z_PERM[start]), stop - start))
        start = stop
    return runs


def _permute_kernel(w_ref, o_ref):
    for dst, src, width in _column_runs():
        t, c = divmod(dst, Z_TILE)
        if src < 0:
            o_ref[0, t, :, c:c + width] = jnp.zeros((CAST_BLK, width), _bf16)
        else:
            o_ref[0, t, :, c:c + width] = _bf(w_ref[0, :, src:src + width])


def _permute_columns(w_in):
    depth, d, cols = w_in.shape
    nj = Z_COLS // Z_TILE
    return pl.pallas_call(
        _permute_kernel,
        grid=(depth, d // CAST_BLK),
        in_specs=[pl.BlockSpec((1, CAST_BLK, cols), lambda l, r: (l, r, 0))],
        out_specs=pl.BlockSpec((1, nj, CAST_BLK, Z_TILE), lambda l, r: (l, 0, r, 0)),
        out_shape=jax.ShapeDtypeStruct((depth, nj, d, Z_TILE), _bf16),
        compiler_params=_params(("parallel", "parallel")),
        name="permute_w_in",
    )(w_in)


def _lane_vector(values, offset):
    k = values.shape[-1]
    return jnp.pad(values.astype(_f32), ((0, 0), (offset, LANES - offset - k)))[:, None, :]


def kernel(x_prompt, x_sample, cache_na_k, cache_na_v, cache_gqa_k, cache_gqa_v, state_gla, state_gdn, c, c_ctx, norm_g, w_mod, b_mod, ffn_gu, ffn_down, w_in, w_out, na_qk_norm, na_rpb, gla_gate_up, gla_gate_bias, gla_out_norm, gqa_qk_norm, gdn_conv, gdn_a_log, gdn_dt_bias, gdn_out_norm):
    assert GRID_ROWS % NA_QROWS == 0 and GRID_ROWS >= NA_KROWS and PAST_LEN == SEQ
    d = D_MODEL
    m_ctx = BATCH * SEQ
    assert m_ctx % DEC_SEQ == 0 and DEC_SEQ % TOKEN_TILE == 0
    x = jnp.concatenate([x_prompt.reshape(m_ctx, d), x_sample.reshape(DEC_BATCH * DEC_SEQ, d)], axis=0)

    n_cond = 1 + DEC_BATCH
    cond = jnp.concatenate([c_ctx[None, :], c], axis=0)
    cond = jnp.pad(cond, ((0, (-n_cond) % 8), (0, 0)))
    mod = adaln_all(cond, w_mod, b_mod).reshape(DEPTH, cond.shape[0], N_MOD, d)

    w_gate, w_up, down = cast_ffn_weights(ffn_gu, ffn_down)
    w_in_p = _permute_columns(w_in)
    w_out_b = w_out.astype(_bf16)

    consts = _chunk_constants()
    cos, sin = _rope_tables()
    bias = na_bias_tables(na_rpb)

    qw = GLA_HEADS * GLA_DK
    wg = jnp.zeros((DEPTH, 2, LANES, qw), _f32)
    wg = wg.at[:, 0, 0:GLA_GATE_RANK].set(gla_gate_up[:, 0])
    wg = wg.at[:, 1, SMALL_GB:SMALL_GB + GLA_GATE_RANK].set(gla_gate_up[:, 1]).astype(_bf16)
    gbias = gla_gate_bias.reshape(DEPTH, 2, 1, qw)
    conv_w = gdn_conv.reshape(DEPTH, CONV_K, 3, GDN_HEADS * HEAD_DIM).transpose(0, 2, 1, 3)
    conv_w = jnp.pad(conv_w, ((0, 0), (0, 0), (0, 8 - CONV_K), (0, 0)))
    alog_lane = _lane_vector(gdn_a_log.reshape(DEPTH, 2 * GDN_HEADS), SMALL_A)
    dtb_lane = _lane_vector(gdn_dt_bias.reshape(DEPTH, 2 * GDN_HEADS), SMALL_A)
    gla_zero = jnp.zeros((BATCH, 2, GLA_HEADS, GLA_DK, GLA_DV), _f32)
    gdn_zero = jnp.zeros((BATCH, 2, GDN_HEADS, GDN_DK, GDN_DV), _f32)
    lat_blk0 = m_ctx // DEC_SEQ

    gla_l, gdn_l = [], []
    caches = None
    for l in range(DEPTH):
        x = ffn(x, mod[l], norm_g[l, 0], w_gate, w_up, down, 2 * l, 0)
        z = in_proj(x, mod[l], norm_g[l, 1], w_in_p, l)

        o_na, o_gqa, caches = ctx_attention(z, na_qk_norm[l], gqa_qk_norm[l], l, caches)
        o_na = na_latent(z, cache_na_k, cache_na_v, bias, na_qk_norm[l], l, o_na)
        o_gqa = gqa_latent(z, cache_gqa_k, cache_gqa_v, cos, sin, gqa_qk_norm[l], l, o_gqa)
        o_gla, sg = gla(z, 0, BATCH, SEQ, wg[l], gbias[l], consts, gla_zero, gla_out_norm[l])
        o_gla, _ = gla(z, lat_blk0, DEC_BATCH, DEC_SEQ, wg[l], gbias[l], consts, state_gla[:, l],
                       gla_out_norm[l], o_gla)
        o_gdn, sd = gdn(z, 0, BATCH, SEQ, conv_w[l], alog_lane[l], dtb_lane[l], consts, gdn_zero,
                        gdn_out_norm[l])
        o_gdn, _ = gdn(z, lat_blk0, DEC_BATCH, DEC_SEQ, conv_w[l], alog_lane[l], dtb_lane[l], consts,
                       state_gdn[:, l], gdn_out_norm[l], o_gdn)

        x = out_proj(x, mod[l], o_na, o_gla, o_gqa, o_gdn, w_out_b, l)
        x = ffn(x, mod[l], norm_g[l, 2], w_gate, w_up, down, 2 * l + 1, 6)

        gla_l.append(sg)
        gdn_l.append(sd)

    y_prompt = x[:m_ctx].reshape(BATCH, SEQ, d)
    y_sample = x[m_ctx:].reshape(DEC_BATCH, DEC_SEQ, d)
    return (y_prompt, y_sample) + caches + (jnp.stack(gla_l, axis=1), jnp.stack(gdn_l, axis=1))
```

```python
import functools
import math

import jax
import jax.numpy as jnp
import numpy as np
from jax import lax
from jax.experimental import pallas as pl
from jax.experimental.pallas import tpu as pltpu

D_MODEL = 2048
BATCH = 16
SEQ = 256
DEPTH = 4
DEC_BATCH = 8
DEC_SEQ = 1024
PAST_LEN = 256

GRID_W = 64
HEAD_DIM = 128
GROUP_WIDTH = D_MODEL // 4
NA_HEADS = GROUP_WIDTH // HEAD_DIM
NA_WIN_ROWS = 8
NA_WIN_COLS = 16
GLA_HEADS = 4
GLA_DV = GROUP_WIDTH // GLA_HEADS
GLA_DK = GLA_DV // 2
GLA_GATE_RANK = 16
GLA_TAU = 16.0
GQA_HEADS = GROUP_WIDTH // HEAD_DIM
GQA_KV_HEADS = GQA_HEADS // 2
GDN_HEADS = GROUP_WIDTH // HEAD_DIM
GDN_DK = HEAD_DIM
GDN_DV = HEAD_DIM
CONV_K = 5
CHUNK = 64
ROPE_THETA = 10000.0
FFN_DIM = ((8 * D_MODEL // 3 + 127) // 128) * 128
N_MOD = 9
EPS = 1e-6
NEG_INF = -1e30

IN_SPLITS = (
    ('na_q', NA_HEADS * HEAD_DIM), ('na_k', NA_HEADS * HEAD_DIM), ('na_v', NA_HEADS * HEAD_DIM),
    ('gla_q', GLA_HEADS * GLA_DK), ('gla_k', GLA_HEADS * GLA_DK), ('gla_v', GLA_HEADS * GLA_DV),
    ('gla_r', GLA_HEADS * GLA_DV), ('gla_gf', GLA_GATE_RANK), ('gla_gb', GLA_GATE_RANK),
    ('gqa_q', GQA_HEADS * HEAD_DIM), ('gqa_k', GQA_KV_HEADS * HEAD_DIM), ('gqa_v', GQA_KV_HEADS * HEAD_DIM),
    ('gdn_qkv', 3 * GDN_HEADS * HEAD_DIM), ('gdn_z', GDN_HEADS * GDN_DV),
    ('gdn_b', 2 * GDN_HEADS), ('gdn_a', 2 * GDN_HEADS),
)

LANES = 128
FFN_TILE = 512
FFN_PAD = ((FFN_DIM + FFN_TILE - 1) // FFN_TILE) * FFN_TILE
TOKEN_TILE = 512
INPROJ_TOKEN_TILE = 1024
FFN_TOKEN_TILE = 512
VMEM_LIMIT = 56 * 1024 * 1024

Z_ORDER = ('na_q', 'na_k', 'na_v', 'gla_q', 'gla_k', 'gla_v', 'gla_r', 'gqa_q', 'gqa_k', 'gqa_v',
           'gdn_qkv', 'gdn_z')
SMALL_ORDER = ('gla_gf', 'gla_gb', 'gdn_b', 'gdn_a')
SMALL_GB = GLA_GATE_RANK
SMALL_BETA = 2 * GLA_GATE_RANK
SMALL_A = SMALL_BETA + 2 * GDN_HEADS
N_LEVELS = int(math.log2(CHUNK))


def _layout():
    src, off = {}, 0
    for name, width in IN_SPLITS:
        src[name] = (off, width)
        off += width
    perm, zoff, pos = [], {}, 0
    for name in Z_ORDER:
        o, w = src[name]
        zoff[name] = pos
        perm.extend(range(o, o + w))
        pos += w
    zoff['small'] = pos
    n_small = 0
    for name in SMALL_ORDER:
        o, w = src[name]
        perm.extend(range(o, o + w))
        n_small += w
    perm.extend([-1] * (LANES - n_small))
    pos += LANES
    perm.extend([-1] * ((-pos) % Z_TILE))
    return np.asarray(perm, np.int32), zoff, len(perm)


Z_TILE = 5 * 256
Z_PERM, Z_OFF, Z_COLS = _layout()

_f32 = jnp.float32
_bf16 = jnp.bfloat16


def _dot(a, b):
    return jnp.dot(a, b, preferred_element_type=_f32)


def _dot_nt(a, b):
    return lax.dot_general(a, b, (((1,), (1,)), ((), ())), preferred_element_type=_f32)


def _dot_tn(a, b):
    return lax.dot_general(a, b, (((0,), (0,)), ((), ())), preferred_element_type=_f32)


def _bf(x):
    return x.astype(_bf16)


def _sigmoid(x):
    return 1.0 / (1.0 + jnp.exp(-x))


def _silu(x):
    return x * _sigmoid(x)


def _softplus(x):
    return jnp.maximum(x, 0.0) + jnp.log(1.0 + jnp.exp(-jnp.abs(x)))


def _rms(x, w):
    return x * lax.rsqrt(jnp.mean(x * x, axis=-1, keepdims=True) + EPS) * w


def _split3(x):
    hi = _bf(x)
    r1 = x - hi.astype(_f32)
    mid = _bf(r1)
    lo = _bf(r1 - mid.astype(_f32))
    return hi, mid, lo


def _split2(x):
    hi = _bf(x)
    return hi, _bf(x - hi.astype(_f32))


def _dot_parts(a, b):
    return _dot(jnp.concatenate([a[0], a[0], a[1]], axis=1), jnp.concatenate([b[0], b[1], b[0]], axis=0))


def _cond_row(i, tile):
    n_ctx = (BATCH * SEQ) // tile
    return jnp.where(i < n_ctx, 0, 1 + (i - n_ctx) // (DEC_SEQ // tile))


def _params(sem, fuse_inputs=None):
    return pltpu.CompilerParams(dimension_semantics=sem, vmem_limit_bytes=VMEM_LIMIT,
                                allow_input_fusion=fuse_inputs)


def _adaln_kernel(c_ref, w_ref, b_ref, o_ref):
    a = _bf(_silu(c_ref[...]))
    o_ref[0] = _dot(a, _bf(w_ref[0])) + b_ref[0]


def adaln_all(cond, w_mod, b_mod):
    nc = cond.shape[0]
    tn = 1024
    n = N_MOD * D_MODEL
    return pl.pallas_call(
        _adaln_kernel,
        grid=(DEPTH, n // tn),
        in_specs=[pl.BlockSpec((nc, D_MODEL), lambda l, j: (0, 0)),
                  pl.BlockSpec((1, D_MODEL, tn), lambda l, j: (l, 0, j)),
                  pl.BlockSpec((1, 1, tn), lambda l, j: (l, 0, j))],
        out_specs=pl.BlockSpec((1, nc, tn), lambda l, j: (l, 0, j)),
        out_shape=jax.ShapeDtypeStruct((DEPTH, nc, n), _f32),
        compiler_params=_params(("parallel", "parallel")),
        name="adaln",
    )(cond, w_mod, b_mod.reshape(DEPTH, 1, n))


CAST_BLK = 256
CAST_CHUNK = 1024


def _cast_gu_kernel(w_ref, g_ref, u_ref):
    f = FFN_DIM
    for t in range(FFN_PAD // FFN_TILE):
        c0 = t * FFN_TILE
        n_real = max(0, min(FFN_TILE, f - c0))
        if n_real:
            g_ref[0, t, :, :n_real] = _bf(w_ref[0, :, c0:c0 + n_real])
            u_ref[0, t, :, :n_real] = _bf(w_ref[0, :, f + c0:f + c0 + n_real])
        if n_real < FFN_TILE:
            g_ref[0, t, :, n_real:] = jnp.zeros((CAST_BLK, FFN_TILE - n_real), _bf16)
            u_ref[0, t, :, n_real:] = jnp.zeros((CAST_BLK, FFN_TILE - n_real), _bf16)


def _cast_down_kernel(w_ref, o_ref):
    f = FFN_DIM
    for r0 in range(0, f, CAST_CHUNK):
        r1 = min(r0 + CAST_CHUNK, f)
        o_ref[0, r0:r1, :] = _bf(w_ref[0, r0:r1, :])
    if FFN_PAD > f:
        o_ref[0, f:, :] = jnp.zeros((FFN_PAD - f, CAST_BLK), _bf16)


def cast_ffn_weights(ffn_gu, ffn_down):
    d, f = D_MODEL, FFN_DIM
    n = ffn_gu.shape[0] * ffn_gu.shape[1]
    nf = FFN_PAD // FFN_TILE
    gate, up = pl.pallas_call(
        _cast_gu_kernel,
        grid=(n, d // CAST_BLK),
        in_specs=[pl.BlockSpec((1, CAST_BLK, 2 * f), lambda i, r: (i, r, 0))],
        out_specs=[pl.BlockSpec((1, nf, CAST_BLK, FFN_TILE), lambda i, r: (i, 0, r, 0))] * 2,
        out_shape=[jax.ShapeDtypeStruct((n, nf, d, FFN_TILE), _bf16)] * 2,
        compiler_params=_params(("parallel", "parallel")),
        name="cast_gate_up",
    )(ffn_gu.reshape(n, d, 2 * f))
    down = pl.pallas_call(
        _cast_down_kernel,
        grid=(n, d // CAST_BLK),
        in_specs=[pl.BlockSpec((1, f, CAST_BLK), lambda i, r: (i, 0, r))],
        out_specs=pl.BlockSpec((1, FFN_PAD, CAST_BLK), lambda i, r: (i, 0, r)),
        out_shape=jax.ShapeDtypeStruct((n, FFN_PAD, d), _bf16),
        compiler_params=_params(("parallel", "parallel")),
        name="cast_down",
    )(ffn_down.reshape(n, f, d))
    return gate, up, down


def _modulated_norm(x, g, m_ref, base):
    sh = m_ref[0, base:base + 1, :]
    sc = m_ref[0, base + 1:base + 2, :]
    return _rms(x, g) * (1.0 + sc) + sh


def _ffn_kernel(x_ref, xn_ref, m_ref, mn_ref, g_ref, wg_ref, wu_ref, wd_ref, o_ref, h0_scr, h1_scr, acc_scr, *,
                base, nsplit):
    i, f = pl.program_id(0), pl.program_id(1)

    @pl.when((i == 0) & (f == 0))
    def _():
        h0_scr[...] = _bf(_modulated_norm(x_ref[...], g_ref[...], m_ref, base))

    @pl.when(f == 0)
    def _():
        acc_scr[...] = jnp.zeros_like(acc_scr)

    rows_per = xn_ref.shape[0]
    rr = pl.ds(pl.multiple_of(jnp.minimum(f, nsplit - 1) * rows_per, rows_per), rows_per)

    def step(h_cur, h_nxt):
        h = h_cur[...]
        gate = _dot(h, wg_ref[...])
        up = _dot(h, wu_ref[...])
        acc_scr[...] += _dot(_bf(_silu(gate) * up), wd_ref[...])
        h_nxt[rr, :] = _bf(_modulated_norm(xn_ref[...], g_ref[...], mn_ref, base))

    @pl.when(i % 2 == 0)
    def _():
        step(h0_scr, h1_scr)

    @pl.when(i % 2 == 1)
    def _():
        step(h1_scr, h0_scr)

    @pl.when(f == pl.num_programs(1) - 1)
    def _():
        o_ref[...] = x_ref[...] + (0.5 * m_ref[0, base + 2:base + 3, :]) * acc_scr[...]


def ffn(x, mod, norm_g, w_gate, w_up, w_down, widx, base):
    m, d = x.shape
    nf = FFN_PAD // FFN_TILE
    tm = FFN_TOKEN_TILE
    nsplit = max(s for s in (1, 2, 4, 8) if s <= nf)
    xn_spec, mn_spec = _lookahead_specs(tm, d, nsplit, m // tm - 1)
    return pl.pallas_call(
        functools.partial(_ffn_kernel, base=base, nsplit=nsplit),
        grid=(m // tm, nf),
        in_specs=[pl.BlockSpec((tm, d), lambda i, f: (i, 0)), xn_spec,
                  pl.BlockSpec((1, N_MOD, d), lambda i, f: (_cond_row(i, tm), 0, 0)), mn_spec,
                  pl.BlockSpec((1, d), lambda i, f: (0, 0)),
                  pl.BlockSpec((None, None, d, FFN_TILE), lambda i, f: (widx, f, 0, 0)),
                  pl.BlockSpec((None, None, d, FFN_TILE), lambda i, f: (widx, f, 0, 0)),
                  pl.BlockSpec((None, FFN_TILE, d), lambda i, f: (widx, f, 0))],
        out_specs=pl.BlockSpec((tm, d), lambda i, f: (i, 0)),
        out_shape=jax.ShapeDtypeStruct((m, d), _f32),
        scratch_shapes=[pltpu.VMEM((tm, d), _bf16), pltpu.VMEM((tm, d), _bf16), pltpu.VMEM((tm, d), _f32)],
        compiler_params=_params(("arbitrary", "arbitrary"), fuse_inputs=[True, True] + [False] * 6),
        name="ffn",
    )(x, x, mod, mod, norm_g.reshape(1, d), w_gate, w_up, w_down)


def _inproj_kernel(x_ref, xn_ref, m_ref, mn_ref, g_ref, w_ref, o_ref, h0_scr, h1_scr, *, nsplit):
    i, j = pl.program_id(0), pl.program_id(1)

    @pl.when((i == 0) & (j == 0))
    def _():
        h0_scr[...] = _bf(_modulated_norm(x_ref[...], g_ref[...], m_ref, 3))

    rows_per = xn_ref.shape[0]
    rr = pl.ds(pl.multiple_of(jnp.minimum(j, nsplit - 1) * rows_per, rows_per), rows_per)

    def step(h_cur, h_nxt):
        o_ref[...] = _dot(h_cur[...], w_ref[...])
        h_nxt[rr, :] = _bf(_modulated_norm(xn_ref[...], g_ref[...], mn_ref, 3))

    @pl.when(i % 2 == 0)
    def _():
        step(h0_scr, h1_scr)

    @pl.when(i % 2 == 1)
    def _():
        step(h1_scr, h0_scr)


def _lookahead_specs(tm, d, nsplit, last):
    rows = tm // nsplit
    return (pl.BlockSpec((rows, d), lambda i, s: (jnp.minimum(i + 1, last) * nsplit + jnp.minimum(s, nsplit - 1), 0)),
            pl.BlockSpec((1, N_MOD, d), lambda i, s: (_cond_row(jnp.minimum(i + 1, last), tm), 0, 0)))


def in_proj(x, mod, norm_g, w_in, layer):
    m, d = x.shape
    tm = INPROJ_TOKEN_TILE
    nj = Z_COLS // Z_TILE
    nsplit = max(s for s in (1, 2, 4, 8) if s <= nj)
    xn_spec, mn_spec = _lookahead_specs(tm, d, nsplit, m // tm - 1)
    return pl.pallas_call(
        functools.partial(_inproj_kernel, nsplit=nsplit),
        grid=(m // tm, nj),
        in_specs=[pl.BlockSpec((tm, d), lambda i, j: (0, 0)), xn_spec,
                  pl.BlockSpec((1, N_MOD, d), lambda i, j: (_cond_row(0, tm), 0, 0)), mn_spec,
                  pl.BlockSpec((1, d), lambda i, j: (0, 0)),
                  pl.BlockSpec((None, None, d, Z_TILE), lambda i, j: (layer, j, 0, 0))],
        out_specs=pl.BlockSpec((tm, Z_TILE), lambda i, j: (i, j)),
        out_shape=jax.ShapeDtypeStruct((m, Z_COLS), _f32),
        scratch_shapes=[pltpu.VMEM((tm, d), _bf16), pltpu.VMEM((tm, d), _bf16)],
        compiler_params=_params(("arbitrary", "arbitrary")),
        name="in_proj",
    )(x, x, mod, mod, norm_g.reshape(1, d), w_in)


def _outproj_kernel(x_ref, m_ref, a_ref, b_ref, c_ref, d_ref, w_ref, o_ref):
    gw = GROUP_WIDTH
    acc = _dot(a_ref[...], w_ref[0:gw, :])
    acc += _dot(b_ref[...], w_ref[gw:2 * gw, :])
    acc += _dot(c_ref[...], w_ref[2 * gw:3 * gw, :])
    acc += _dot(d_ref[...], w_ref[3 * gw:4 * gw, :])
    o_ref[...] = x_ref[...] + m_ref[0, 5:6, :] * acc


def out_proj(x, mod, o_na, o_gla, o_gqa, o_gdn, w_out, layer):
    m, d = x.shape
    tm = TOKEN_TILE
    grp = pl.BlockSpec((tm, GROUP_WIDTH), lambda i: (i, 0))
    return pl.pallas_call(
        _outproj_kernel,
        grid=(m // tm,),
        in_specs=[pl.BlockSpec((tm, d), lambda i: (i, 0)),
                  pl.BlockSpec((1, N_MOD, d), lambda i: (_cond_row(i, tm), 0, 0)),
                  grp, grp, grp, grp,
                  pl.BlockSpec((None, 4 * GROUP_WIDTH, d), lambda i: (layer, 0, 0))],
        out_specs=pl.BlockSpec((tm, d), lambda i: (i, 0)),
        out_shape=jax.ShapeDtypeStruct((m, d), _f32),
        compiler_params=_params(("parallel",), fuse_inputs=[False] * 6 + [True]),
        name="out_proj",
    )(x, mod, o_na, o_gla, o_gqa, o_gdn, w_out)


LOG2E = math.log2(math.e)
ATTN_Q_SCALE = HEAD_DIM ** -0.5 * LOG2E


def _softmax_pv(scores, values):
    mx = scores[0].max(axis=-1, keepdims=True)
    for s in scores[1:]:
        mx = jnp.maximum(mx, s.max(axis=-1, keepdims=True))
    num, den = None, None
    for s, v in zip(scores, values):
        p = jnp.exp2(s - mx)
        d_ = p.sum(axis=-1, keepdims=True)
        n_ = _dot(_bf(p), v)
        num = n_ if num is None else num + n_
        den = d_ if den is None else den + d_
    return num / den


def _rope(x, cos, sin):
    lane = lax.broadcasted_iota(jnp.int32, x.shape, 1)
    quarter = HEAD_DIM // 4
    partner = jnp.where((lane % (2 * quarter)) < quarter,
                        pltpu.roll(x, HEAD_DIM - quarter, 1), pltpu.roll(x, quarter, 1))
    return x * cos + partner * sin


def _ctx_attn_kernel(nq_ref, nk_ref, nv_ref, gq_ref, gk_ref, gv_ref, nw_ref, gw_ref, *refs, n_alias):
    ona_ref, ogqa_ref, kn_ref, vn_ref, ka_ref, va_ref = refs[n_alias:]
    hd = HEAD_DIM
    for h in range(NA_HEADS):
        cols = slice(h * hd, (h + 1) * hd)
        k = _rms(nk_ref[:, cols], nw_ref[1:2, :])
        v = nv_ref[:, cols]
        q = _rms(nq_ref[:, cols], nw_ref[0:1, :]) * ATTN_Q_SCALE
        kn_ref[0, h] = k
        vn_ref[0, h] = v
        s = _dot_nt(_bf(q), _bf(k))
        ona_ref[:, cols] = _bf(_softmax_pv([s], [_bf(v)]))
    group = GQA_HEADS // GQA_KV_HEADS
    for kv in range(GQA_KV_HEADS):
        cols = slice(kv * hd, (kv + 1) * hd)
        k = _rms(gk_ref[:, cols], gw_ref[1:2, :])
        v = gv_ref[:, cols]
        ka_ref[0, kv] = k
        va_ref[0, kv] = v
        for g in range(group):
            qcols = slice((kv * group + g) * hd, (kv * group + g + 1) * hd)
            q = _rms(gq_ref[:, qcols], gw_ref[0:1, :]) * ATTN_Q_SCALE
            s = _dot_nt(_bf(q), _bf(k))
            ogqa_ref[:, qcols] = _bf(_softmax_pv([s], [_bf(v)]))


def ctx_attention(z, na_w, gqa_w, layer, caches):
    t, hd = SEQ, HEAD_DIM

    def zspec(name, width):
        blk = Z_OFF[name] // width
        assert Z_OFF[name] % width == 0
        return pl.BlockSpec((t, width), lambda b: (b, blk))

    def cache_spec(heads):
        return pl.BlockSpec((1, None, heads, t, hd), lambda b: (b, layer, 0, 0, 0))

    def cache_shape(heads):
        return jax.ShapeDtypeStruct((BATCH, DEPTH, heads, t, hd), _f32)

    gw = GROUP_WIDTH
    kvw = GQA_KV_HEADS * hd
    in_specs = [zspec('na_q', gw), zspec('na_k', gw), zspec('na_v', gw),
                zspec('gqa_q', gw), zspec('gqa_k', kvw), zspec('gqa_v', kvw),
                pl.BlockSpec((2, hd), lambda b: (0, 0)), pl.BlockSpec((2, hd), lambda b: (0, 0))]
    args = [z, z, z, z, z, z, na_w, gqa_w]
    aliases = {}
    if caches is not None:
        aliases = {len(args) + i: 2 + i for i in range(4)}
        in_specs += [pl.BlockSpec(memory_space=pl.ANY)] * 4
        args += list(caches)
    outs = pl.pallas_call(
        functools.partial(_ctx_attn_kernel, n_alias=len(aliases)),
        grid=(BATCH,),
        in_specs=in_specs,
        out_specs=[pl.BlockSpec((t, gw), lambda b: (b, 0)), pl.BlockSpec((t, gw), lambda b: (b, 0)),
                   cache_spec(NA_HEADS), cache_spec(NA_HEADS), cache_spec(GQA_KV_HEADS), cache_spec(GQA_KV_HEADS)],
        out_shape=[jax.ShapeDtypeStruct((z.shape[0], gw), _bf16), jax.ShapeDtypeStruct((z.shape[0], gw), _bf16),
                   cache_shape(NA_HEADS), cache_shape(NA_HEADS), cache_shape(GQA_KV_HEADS),
                   cache_shape(GQA_KV_HEADS)],
        input_output_aliases=aliases,
        compiler_params=_params(("parallel",)),
        name="ctx_attention",
    )(*args)
    return outs[0], outs[1], tuple(outs[2:])


NA_QROWS = 4
NA_KROWS = NA_QROWS + NA_WIN_ROWS
GRID_ROWS = DEC_SEQ // GRID_W


def _na_key_row0(blk):
    return min(max(blk * NA_QROWS - NA_WIN_ROWS // 2, 0), GRID_ROWS - NA_KROWS)


def _na_bias_kernel(rpb_ref, o_ref):
    l, h = pl.program_id(0), pl.program_id(1)
    n_dr, n_dc = 2 * NA_WIN_ROWS - 1, 2 * NA_WIN_COLS - 1
    base = (l * NA_HEADS + h) * (n_dr * n_dc)
    qc = lax.broadcasted_iota(jnp.int32, (GRID_W, GRID_W), 0)
    kc = lax.broadcasted_iota(jnp.int32, (GRID_W, GRID_W), 1)
    dc = jnp.clip(kc - qc + (NA_WIN_COLS - 1), 0, n_dc - 1)
    c0 = jnp.clip(qc - NA_WIN_COLS // 2, 0, GRID_W - NA_WIN_COLS)
    in_win = (kc >= c0) & (kc < c0 + NA_WIN_COLS)
    masked = jnp.full((GRID_W, GRID_W), NEG_INF, _f32)
    tiles = []
    for dr in range(n_dr):
        t = jnp.zeros((GRID_W, GRID_W), _f32)
        for j in range(n_dc):
            t = jnp.where(dc == j, rpb_ref[base + dr * n_dc + j], t)
        tiles.append(jnp.where(in_win, t * LOG2E, NEG_INF))
    for blk in range(GRID_ROWS // NA_QROWS):
        k0 = _na_key_row0(blk)
        for qi in range(NA_QROWS):
            r = blk * NA_QROWS + qi
            krow0 = min(max(r - NA_WIN_ROWS // 2, 0), GRID_ROWS - NA_WIN_ROWS)
            for kj in range(NA_KROWS):
                kr = k0 + kj
                ok = krow0 <= kr < krow0 + NA_WIN_ROWS
                tile = tiles[kr - r + NA_WIN_ROWS - 1] if ok else masked
                o_ref[0, 0, blk, qi * GRID_W:(qi + 1) * GRID_W, kj * GRID_W:(kj + 1) * GRID_W] = tile


def na_bias_tables(na_rpb):
    nblk = GRID_ROWS // NA_QROWS
    shape = (DEPTH, NA_HEADS, nblk, NA_QROWS * GRID_W, NA_KROWS * GRID_W)
    return pl.pallas_call(
        _na_bias_kernel,
        grid=(DEPTH, NA_HEADS),
        in_specs=[pl.BlockSpec(memory_space=pltpu.SMEM)],
        out_specs=pl.BlockSpec((1, 1) + shape[2:], lambda l, h: (l, h, 0, 0, 0)),
        out_shape=jax.ShapeDtypeStruct(shape, _f32),
        compiler_params=_params(("parallel", "parallel")),
        name="na_bias",
    )(na_rpb.reshape(-1))


def _na_lat_kernel(q_ref, k_ref, v_ref, ck_ref, cv_ref, bias_ref, nw_ref, _prev_ref, o_ref, kn_scr, vb_scr):
    kn_scr[...] = _bf(_rms(k_ref[...], nw_ref[1:2, :]))
    vb_scr[...] = _bf(v_ref[...])
    ck = _bf(ck_ref[0, 0, 0])
    cv = _bf(cv_ref[0, 0, 0])
    tq = NA_QROWS * GRID_W
    nkeys = NA_KROWS * GRID_W
    for blk in range(GRID_ROWS // NA_QROWS):
        start = _na_key_row0(blk) * GRID_W
        rows = slice(blk * tq, (blk + 1) * tq)
        q = _bf(_rms(q_ref[rows, :], nw_ref[0:1, :]) * ATTN_Q_SCALE)
        s_ctx = _dot_nt(q, ck)
        s_loc = _dot_nt(q, kn_scr[start:start + nkeys, :]) + bias_ref[0, 0, blk]
        o_ref[rows, :] = _bf(_softmax_pv([s_ctx, s_loc], [cv, vb_scr[start:start + nkeys, :]]))


def na_latent(z, cache_k, cache_v, bias, na_w, layer, prev):
    t, hd = DEC_SEQ, HEAD_DIM
    tq = NA_QROWS * GRID_W
    nblk = t // tq
    ctx_seqs = (BATCH * SEQ) // t
    qb, kb, vb = Z_OFF['na_q'] // hd, Z_OFF['na_k'] // hd, Z_OFF['na_v'] // hd
    cache_spec = pl.BlockSpec((1, 1, 1, PAST_LEN, hd), lambda b, h: (b, layer, h, 0, 0))
    return pl.pallas_call(
        _na_lat_kernel,
        grid=(DEC_BATCH, NA_HEADS),
        in_specs=[pl.BlockSpec((t, hd), lambda b, h: (ctx_seqs + b, qb + h)),
                  pl.BlockSpec((t, hd), lambda b, h: (ctx_seqs + b, kb + h)),
                  pl.BlockSpec((t, hd), lambda b, h: (ctx_seqs + b, vb + h)),
                  cache_spec, cache_spec,
                  pl.BlockSpec((1, 1, nblk, tq, NA_KROWS * GRID_W), lambda b, h: (layer, h, 0, 0, 0)),
                  pl.BlockSpec((2, hd), lambda b, h: (0, 0)),
                  pl.BlockSpec(memory_space=pl.ANY)],
        out_specs=pl.BlockSpec((t, hd), lambda b, h: (ctx_seqs + b, h)),
        out_shape=jax.ShapeDtypeStruct(prev.shape, prev.dtype),
        input_output_aliases={7: 0},
        scratch_shapes=[pltpu.VMEM((t, hd), _bf16), pltpu.VMEM((t, hd), _bf16)],
        compiler_params=_params(("parallel", "parallel")),
        name="na_latent",
    )(z, z, z, cache_k, cache_v, bias, na_w, prev)


GQA_TQ = 256


def _gqa_lat_kernel(q_ref, k_ref, v_ref, ck_ref, cv_ref, cos_ref, sin_ref, gw_ref, _prev_ref, o_ref,
                    kr_scr, vb_scr):
    hd = HEAD_DIM
    kr_scr[...] = _bf(_rope(_rms(k_ref[...], gw_ref[1:2, :]), cos_ref[...], sin_ref[...]))
    vb_scr[...] = _bf(v_ref[...])
    ck = _bf(ck_ref[0, 0, 0])
    cv = _bf(cv_ref[0, 0, 0])
    for blk in range(q_ref.shape[0] // GQA_TQ):
        rows = slice(blk * GQA_TQ, (blk + 1) * GQA_TQ)
        for g in range(GQA_HEADS // GQA_KV_HEADS):
            cols = slice(g * hd, (g + 1) * hd)
            q = _rope(_rms(q_ref[rows, cols], gw_ref[0:1, :]), cos_ref[rows, :], sin_ref[rows, :])
            q = _bf(q * ATTN_Q_SCALE)
            s_ctx = _dot_nt(q, ck)
            s_loc = _dot_nt(q, kr_scr[...])
            o_ref[rows, cols] = _bf(_softmax_pv([s_ctx, s_loc], [cv, vb_scr[...]]))


def gqa_latent(z, cache_k, cache_v, cos, sin, gqa_w, layer, prev):
    t, hd = DEC_SEQ, HEAD_DIM
    assert t % GQA_TQ == 0
    group = GQA_HEADS // GQA_KV_HEADS
    ctx_seqs = (BATCH * SEQ) // t
    qb = Z_OFF['gqa_q'] // (group * hd)
    kb, vb = Z_OFF['gqa_k'] // hd, Z_OFF['gqa_v'] // hd
    cache_spec = pl.BlockSpec((1, 1, 1, PAST_LEN, hd), lambda b, h: (b, layer, h, 0, 0))
    return pl.pallas_call(
        _gqa_lat_kernel,
        grid=(DEC_BATCH, GQA_KV_HEADS),
        in_specs=[pl.BlockSpec((t, group * hd), lambda b, h: (ctx_seqs + b, qb + h)),
                  pl.BlockSpec((t, hd), lambda b, h: (ctx_seqs + b, kb + h)),
                  pl.BlockSpec((t, hd), lambda b, h: (ctx_seqs + b, vb + h)),
                  cache_spec, cache_spec,
                  pl.BlockSpec((t, hd), lambda b, h: (0, 0)),
                  pl.BlockSpec((t, hd), lambda b, h: (0, 0)),
                  pl.BlockSpec((2, hd), lambda b, h: (0, 0)),
                  pl.BlockSpec(memory_space=pl.ANY)],
        out_specs=pl.BlockSpec((t, group * hd), lambda b, h: (ctx_seqs + b, h)),
        out_shape=jax.ShapeDtypeStruct(prev.shape, prev.dtype),
        input_output_aliases={8: 0},
        scratch_shapes=[pltpu.VMEM((t, hd), _bf16), pltpu.VMEM((t, hd), _bf16)],
        compiler_params=_params(("parallel", "parallel")),
        name="gqa_latent",
    )(z, z, z, cache_k, cache_v, cos, sin, gqa_w, prev)


def _rope_tables():
    t = np.arange(DEC_SEQ)
    row = (t // GRID_W).astype(np.float32)
    col = (t % GRID_W).astype(np.float32)
    half = HEAD_DIM // 2
    inv = jnp.asarray(ROPE_THETA, _f32) ** (-jnp.arange(0, half, 2, dtype=_f32) / half)
    ang_r = jnp.asarray(row)[:, None] * inv[None, :]
    ang_c = jnp.asarray(col)[:, None] * inv[None, :]
    cos = jnp.concatenate([jnp.cos(ang_r), jnp.cos(ang_r), jnp.cos(ang_c), jnp.cos(ang_c)], axis=-1)
    sin = jnp.concatenate([-jnp.sin(ang_r), jnp.sin(ang_r), -jnp.sin(ang_c), jnp.sin(ang_c)], axis=-1)
    return cos, sin


def _chunk_constants():
    c = CHUNK
    t = np.arange(c)
    tri, a_cat, pair, mq, mk, causal, strict = [], [], [], [], [], [], []
    for d in range(2):
        tau = t if d == 0 else c - 1 - t
        incl = (tau[None, :] <= tau[:, None]).astype(np.float32)
        tri.append(incl)
        causal.append(incl)
        strict.append((tau[None, :] < tau[:, None]).astype(np.float32))
        rows, pm, qm, km = [], [], [], []
        for li in range(N_LEVELS):
            s = c >> (li + 1)
            ref_tau = 2 * s * (tau // (2 * s)) + s - 1
            sel = (tau[None, :] == ref_tau[:, None]).astype(np.float32)
            rows.append(incl - sel @ incl)
            odd = ((tau // s) % 2 == 1).astype(np.float32)
            pm.append(((tau[:, None] // (2 * s)) == (tau[None, :] // (2 * s))).astype(np.float32))
            qm.append(np.repeat(odd[:, None], GLA_DK, axis=1))
            km.append(np.repeat((1.0 - odd)[:, None], GLA_DK, axis=1))
        pm.append(np.eye(c, dtype=np.float32))
        rows.append(incl)
        rows.append((tau[None, :] > tau[:, None]).astype(np.float32))
        a = np.concatenate(rows, axis=0)
        a_cat.append(np.concatenate([a, a, a], axis=1))
        pair.append(np.stack(pm))
        mq.append(np.stack(qm))
        mk.append(np.stack(km))
    tri_cat = np.stack([np.concatenate([x, x, x], axis=1) for x in tri])
    sub = np.stack([pair[d][:N_LEVELS] * mq[d][:, :, :1] * np.swapaxes(mk[d][:, :, :1], 1, 2) for d in range(2)])
    return dict(sub=jnp.asarray(sub), sub_b=jnp.asarray(sub, _bf16),
                a_cat=jnp.asarray(np.stack(a_cat), _bf16), pair=jnp.asarray(np.stack(pair)),
                mq=jnp.asarray(np.stack(mq)), mk=jnp.asarray(np.stack(mk)),
                tri_cat=jnp.asarray(tri_cat, _bf16),
                causal=jnp.asarray(np.stack([np.stack([causal[d], strict[d]]) for d in range(2)])))


GLA_STEP_CHUNKS = 2


def _chunk_pos(d, c, n):
    return c if d == 0 else n - 1 - c


def _gla_kernel(q_ref, k_ref, v_ref, r_ref, zs_ref, wg_ref, gb_ref, acat_ref, pair_ref, mq_ref, mk_ref,
                s0_ref, nw_ref, _prev_ref, o_ref, sn_ref, la_scr, oacc_scr, s_scr, *, t):
    c, dk, dv, nh = CHUNK, GLA_DK, GLA_DV, GLA_HEADS
    n = t // c
    zs = _bf(zs_ref[...])
    for d in range(2):
        x = _dot(zs, wg_ref[d]) + gb_ref[d]
        la_scr[d] = (jnp.minimum(x, 0.0) - jnp.log(1.0 + jnp.exp(-jnp.abs(x)))) * (1.0 / GLA_TAU)
        for h in range(nh):
            s_scr[d * nh + h] = s0_ref[0, d, h]
    ones = jnp.ones((3 * c, LANES), _bf16)
    chains = [(d, h) for d in range(2) for h in range(nh)]

    def chunk(ci, carry):
        states = [s_scr[d * nh + h] for d, h in chains]
        jobs, rows, dall, blast = [], [], [], []
        for cc in range(GLA_STEP_CHUNKS):
            rows.append([pl.ds(pl.multiple_of(_chunk_pos(d, ci * GLA_STEP_CHUNKS + cc, n) * c, c), c)
                         for d in range(2)])
            d_cc, b_cc = [], []
            for d in range(2):
                gcat = jnp.concatenate(_split3(la_scr[d, rows[cc][d], :]), axis=0)
                d_cc.append(_dot(acat_ref[d], gcat))
                b_cc.append(_dot_tn(gcat, ones))
            dall.append(d_cc)
            blast.append(b_cc)
            jobs += [(cc, d, h) for d, h in chains]
        qs, ks, vs = [], [], []
        for cc, d, h in jobs:
            kc = slice(h * dk, (h + 1) * dk)
            qs.append(q_ref[rows[cc][d], kc] * (dk ** -0.5))
            ks.append(k_ref[rows[cc][d], kc])
            vs.append(_bf(v_ref[rows[cc][d], h * dv:(h + 1) * dv]))
        atts = [_dot_nt(_bf(q), _bf(k)) * pair_ref[d, N_LEVELS] for (cc, d, h), q, k in zip(jobs, qs, ks)]
        for li in range(N_LEVELS):
            for j, (cc, d, h) in enumerate(jobs):
                f = jnp.exp(-jnp.abs(dall[cc][d][li * c:(li + 1) * c, h * dk:(h + 1) * dk]))
                atts[j] += _dot_nt(_bf(qs[j] * f * mq_ref[d, li]), _bf(ks[j] * f * mk_ref[d, li])) * pair_ref[d, li]
        qes, kes, decs, avs = [], [], [], []
        for j, (cc, d, h) in enumerate(jobs):
            kc = slice(h * dk, (h + 1) * dk)
            qes.append(_bf(qs[j] * jnp.exp(dall[cc][d][N_LEVELS * c:(N_LEVELS + 1) * c, kc])))
            kes.append(_bf(ks[j] * jnp.exp(dall[cc][d][(N_LEVELS + 1) * c:(N_LEVELS + 2) * c, kc])))
            decs.append(jnp.exp(blast[cc][d][kc, :]))
            avs.append(_dot(_bf(atts[j]), vs[j]))
        for j, (cc, d, h) in enumerate(jobs):
            s = states[d * nh + h]
            oacc_scr[d, rows[cc][d], h * dv:(h + 1) * dv] = _dot(qes[j], _bf(s)) + avs[j]
            states[d * nh + h] = s * decs[j] + _dot_tn(kes[j], vs[j])
        for d, h in chains:
            s_scr[d * nh + h] = states[d * nh + h]
        return carry

    lax.fori_loop(0, n // GLA_STEP_CHUNKS, chunk, 0)

    for d, h in chains:
        sn_ref[0, d, h] = s_scr[d * nh + h]
    for h in range(nh):
        vc = slice(h * dv, (h + 1) * dv)
        o_ref[:, vc] = _bf(_rms(oacc_scr[0, :, vc] + oacc_scr[1, :, vc], nw_ref[...]) * _silu(r_ref[:, vc]))


def gla(z, row_blk0, nseq, t, wg, gbias, consts, s0, out_norm, prev=None):
    qw = GLA_HEADS * GLA_DK
    vw = GLA_HEADS * GLA_DV
    if prev is None:
        prev = jnp.zeros((z.shape[0], vw), _bf16)

    def zspec(name, width):
        blk = Z_OFF[name] // width
        assert Z_OFF[name] % width == 0
        return pl.BlockSpec((t, width), lambda b: (row_blk0 + b, blk))

    def full(a):
        return pl.BlockSpec(a.shape, lambda b, nd=a.ndim: (0,) * nd)

    state_spec = pl.BlockSpec((1, 2, GLA_HEADS, GLA_DK, GLA_DV), lambda b: (b, 0, 0, 0, 0))
    cs = [consts['a_cat'], consts['pair'], consts['mq'], consts['mk']]
    return pl.pallas_call(
        functools.partial(_gla_kernel, t=t),
        grid=(nseq,),
        in_specs=[zspec('gla_q', qw), zspec('gla_k', qw), zspec('gla_v', vw), zspec('gla_r', vw),
                  zspec('small', LANES), full(wg), full(gbias)] + [full(a) for a in cs]
                 + [state_spec, pl.BlockSpec((1, GLA_DV), lambda b: (0, 0)), pl.BlockSpec(memory_space=pl.ANY)],
        out_specs=[pl.BlockSpec((t, vw), lambda b: (row_blk0 + b, 0)), state_spec],
        out_shape=[jax.ShapeDtypeStruct(prev.shape, prev.dtype),
                   jax.ShapeDtypeStruct((nseq, 2, GLA_HEADS, GLA_DK, GLA_DV), _f32)],
        input_output_aliases={7 + len(cs) + 2: 0},
        scratch_shapes=[pltpu.VMEM((2, t, qw), _f32), pltpu.VMEM((2, t, vw), _f32),
                        pltpu.VMEM((2 * GLA_HEADS, GLA_DK, GLA_DV), _f32)],
        compiler_params=_params(("parallel",)),
        name="gla",
    )(z, z, z, z, z, wg, gbias, *cs, s0, out_norm.reshape(1, GLA_DV), prev)


CONV_PAD = 8
ROW_BLK = 128
GDN_LOCAL_CHUNKS = 4


def _gdn_kernel(q_ref, k_ref, v_ref, zz_ref, zs_ref, cw_ref, alog_ref, dtb_ref, tri_ref, causal_ref, sub_ref,
                subb_ref, s0_ref, nw_ref, _prev_ref, o_ref, sn_ref,
                xp_scr, qkv_scr, g_scr, beta_scr, b_scr, u_scr, w_scr, att_scr, oacc_scr, s_scr, *, t):
    c, hd, nh = CHUNK, HEAD_DIM, GDN_HEADS
    n = t // c
    w = nh * hd
    half = CONV_K // 2
    nblk = t // ROW_BLK
    eye = (lax.broadcasted_iota(jnp.int32, (c, c), 0) == lax.broadcasted_iota(jnp.int32, (c, c), 1)).astype(_f32)

    xp_scr[0:CONV_PAD, :] = jnp.zeros((CONV_PAD, w), _f32)
    xp_scr[CONV_PAD + t:2 * CONV_PAD + t, :] = jnp.zeros((CONV_PAD, w), _f32)
    for idx, src in enumerate((q_ref, k_ref, v_ref)):
        def copy_in(i, carry, src=src):
            r0 = pl.multiple_of(i * ROW_BLK, ROW_BLK)
            xp_scr[pl.ds(CONV_PAD + r0, ROW_BLK), :] = src[pl.ds(r0, ROW_BLK), :]
            return carry

        lax.fori_loop(0, nblk, copy_in, 0)

        def conv(i, carry, idx=idx):
            r0 = pl.multiple_of(i * ROW_BLK, ROW_BLK)
            win = xp_scr[pl.ds(r0, ROW_BLK + 2 * CONV_PAD), :]
            y = jnp.zeros((ROW_BLK, w), _f32)
            for j in range(CONV_K):
                lo = CONV_PAD + j - half
                y += win[lo:lo + ROW_BLK, :] * cw_ref[idx, j:j + 1, :]
            y = _silu(y)
            for h in range(nh):
                cols = slice(h * hd, (h + 1) * hd)
                yh = y[:, cols]
                if idx == 0:
                    yh = yh * lax.rsqrt(jnp.sum(yh * yh, axis=-1, keepdims=True) + EPS) * (hd ** -0.5)
                elif idx == 1:
                    yh = yh * lax.rsqrt(jnp.sum(yh * yh, axis=-1, keepdims=True) + EPS)
                qkv_scr[idx, pl.ds(r0, ROW_BLK), cols] = yh
            return carry

        lax.fori_loop(0, nblk, conv, 0)

    zs = zs_ref[...]
    g_scr[...] = -jnp.exp(alog_ref[...]) * _softplus(zs + dtb_ref[...])
    beta_scr[...] = _sigmoid(zs)
    for d in range(2):
        for h in range(nh):
            s_scr[d * nh + h] = s0_ref[0, d, h]

    chains = [(d, h) for d in range(2) for h in range(nh)]

    def local(ci, carry):
        jobs, rows, b_all, lows, xs, rhss = [], [], [], [], [], []
        for cc in range(GDN_LOCAL_CHUNKS):
            rr = pl.ds(pl.multiple_of((ci * GDN_LOCAL_CHUNKS + cc) * c, c), c)
            rows.append(rr)
            gcat = jnp.concatenate(_split3(g_scr[rr, :]), axis=0)
            beta_all = beta_scr[rr, :]
            b_cc = [_dot(tri_ref[d], gcat) for d in range(2)]
            b_all.append(b_cc)
            b_t = [b.T for b in b_cc]
            qk, kk, ks, vs = [], [], [], []
            for h in range(nh):
                cols = slice(h * hd, (h + 1) * hd)
                k = qkv_scr[1, rr, cols]
                kbf = _bf(k)
                ks.append(k)
                vs.append(qkv_scr[2, rr, cols])
                kk.append(_dot_nt(kbf, kbf))
                qk.append(_dot_nt(_bf(qkv_scr[0, rr, cols]), kbf))
            for d, h in chains:
                ia = SMALL_A + d * nh + h
                ib = SMALL_BETA + d * nh + h
                bcol = b_cc[d][:, ia:ia + 1]
                decay = jnp.exp(jnp.minimum(bcol - b_t[d][ia:ia + 1, :], 0.0)) * causal_ref[d, 0]
                beta = beta_all[:, ib:ib + 1]
                att_scr[d * nh + h, rr, :] = _bf(qk[h] * decay)
                low = kk[h] * beta * decay * causal_ref[d, 1]
                jobs.append((cc, d, h))
                lows.append(_split2(low))
                xs.append(eye - low * sub_ref[d, N_LEVELS - 1])
                rhss.append(jnp.concatenate([vs[h] * beta, ks[h] * (beta * jnp.exp(bcol))], axis=1))
        for li in range(N_LEVELS - 2, -1, -1):
            for j, (cc, d, h) in enumerate(jobs):
                mask = subb_ref[d, li]
                xp = _split2(xs[j])
                t1 = _dot_parts(xp, (lows[j][0] * mask, lows[j][1] * mask))
                xs[j] = xs[j] - _dot_parts(_split2(t1), xp)
        for j, (cc, d, h) in enumerate(jobs):
            cols = slice(h * hd, (h + 1) * hd)
            sol = _dot_parts(_split2(xs[j]), _split2(rhss[j]))
            u_scr[d, rows[cc], cols] = sol[:, :hd]
            w_scr[d, rows[cc], cols] = _bf(sol[:, hd:])
        for cc in range(GDN_LOCAL_CHUNKS):
            for d in range(2):
                b_scr[d, rows[cc], :] = b_all[cc][d]
        return carry

    lax.fori_loop(0, n // GDN_LOCAL_CHUNKS, local, 0)

    def step(ci, carry):
        rows = [pl.ds(pl.multiple_of(_chunk_pos(d, ci, n) * c, c), c) for d in range(2)]
        states = [s_scr[d * nh + h] for d, h in chains]
        b_all = [b_scr[d, rows[d], :] for d in range(2)]
        us, ws, atts, qs, ks = [], [], [], [], []
        for d, h in chains:
            cols = slice(h * hd, (h + 1) * hd)
            us.append(u_scr[d, rows[d], cols])
            ws.append(w_scr[d, rows[d], cols])
            atts.append(att_scr[d * nh + h, rows[d], :])
            qs.append(qkv_scr[0, rows[d], cols])
            ks.append(qkv_scr[1, rows[d], cols])
        sbs = [_bf(s) for s in states]
        v_new = [_bf(u - _dot(w_, sb)) for u, w_, sb in zip(us, ws, sbs)]
        outs, new_states = [], []
        for j, (d, h) in enumerate(chains):
            ia = SMALL_A + d * nh + h
            last = c - 1 if d == 0 else 0
            bcol = b_all[d][:, ia:ia + 1]
            b_last = b_all[d][last:last + 1, ia:ia + 1]
            outs.append(_dot(_bf(qs[j] * jnp.exp(bcol)), sbs[j]) + _dot(atts[j], v_new[j]))
            new_states.append(states[j] * jnp.exp(b_last) + _dot_tn(_bf(ks[j] * jnp.exp(b_last - bcol)), v_new[j]))
        for j, (d, h) in enumerate(chains):
            s_scr[d * nh + h] = new_states[j]
            oacc_scr[d, rows[d], h * hd:(h + 1) * hd] = outs[j]
        return carry

    lax.fori_loop(0, n, step, 0)

    for d, h in chains:
        sn_ref[0, d, h] = s_scr[d * nh + h]

    def epilogue(i, carry):
        r0 = pl.multiple_of(i * ROW_BLK, ROW_BLK)
        rr = pl.ds(r0, ROW_BLK)
        for h in range(nh):
            cols = slice(h * hd, (h + 1) * hd)
            o = oacc_scr[0, rr, cols] + oacc_scr[1, rr, cols]
            o_ref[rr, cols] = _bf(_rms(o, nw_ref[...]) * _silu(zz_ref[rr, cols]))
        return carry

    lax.fori_loop(0, nblk, epilogue, 0)


def gdn(z, row_blk0, nseq, t, conv_w, alog_lane, dtb_lane, consts, s0, out_norm, prev=None):
    w = GDN_HEADS * HEAD_DIM
    if prev is None:
        prev = jnp.zeros((z.shape[0], w), _bf16)
    qblk = Z_OFF['gdn_qkv'] // w
    assert Z_OFF['gdn_qkv'] % w == 0 and Z_OFF['gdn_z'] % w == 0 and Z_OFF['small'] % LANES == 0

    def zspec(blk, width):
        return pl.BlockSpec((t, width), lambda b: (row_blk0 + b, blk))

    def full(a):
        return pl.BlockSpec(a.shape, lambda b, nd=a.ndim: (0,) * nd)

    state_spec = pl.BlockSpec((1, 2, GDN_HEADS, GDN_DK, GDN_DV), lambda b: (b, 0, 0, 0, 0))
    return pl.pallas_call(
        functools.partial(_gdn_kernel, t=t),
        grid=(nseq,),
        in_specs=[zspec(qblk, w), zspec(qblk + 1, w), zspec(qblk + 2, w), zspec(Z_OFF['gdn_z'] // w, w),
                  zspec(Z_OFF['small'] // LANES, LANES), full(conv_w), full(alog_lane), full(dtb_lane),
                  full(consts['tri_cat']), full(consts['causal']), full(consts['sub']), full(consts['sub_b']),
                  state_spec, pl.BlockSpec((1, GDN_DV), lambda b: (0, 0)), pl.BlockSpec(memory_space=pl.ANY)],
        out_specs=[pl.BlockSpec((t, w), lambda b: (row_blk0 + b, 0)), state_spec],
        out_shape=[jax.ShapeDtypeStruct(prev.shape, prev.dtype),
                   jax.ShapeDtypeStruct((nseq, 2, GDN_HEADS, GDN_DK, GDN_DV), _f32)],
        input_output_aliases={14: 0},
        scratch_shapes=[pltpu.VMEM((t + 2 * CONV_PAD, w), _f32),
                        pltpu.VMEM((3, t, w), _f32),
                        pltpu.VMEM((t, LANES), _f32),
                        pltpu.VMEM((t, LANES), _f32),
                        pltpu.VMEM((2, t, LANES), _f32),
                        pltpu.VMEM((2, t, w), _f32),
                        pltpu.VMEM((2, t, w), _bf16),
                        pltpu.VMEM((2 * GDN_HEADS, t, CHUNK), _bf16),
                        pltpu.VMEM((2, t, w), _f32),
                        pltpu.VMEM((2 * GDN_HEADS, GDN_DK, GDN_DV), _f32)],
        compiler_params=_params(("parallel",)),
        name="gdn",
    )(z, z, z, z, z, conv_w, alog_lane, dtb_lane, consts['tri_cat'], consts['causal'], consts['sub'],
      consts['sub_b'], s0, out_norm.reshape(1, GDN_DV), prev)


def _column_runs():
    runs, start, n = [], 0, len(Z_PERM)
    while start < n:
        stop = start + 1
        while (stop < n and stop - start < CAST_CHUNK and stop % Z_TILE != 0
               and (Z_PERM[stop] == Z_PERM[stop - 1] + 1 if Z_PERM[start] >= 0 else Z_PERM[stop] < 0)):
            stop += 1
        runs.append((start, int(Z_PERM[start]), stop - start))
        start = stop
    return runs


def _permute_kernel(w_ref, o_ref):
    for dst, src, width in _column_runs():
        t, c = divmod(dst, Z_TILE)
        if src < 0:
            o_ref[0, t, :, c:c + width] = jnp.zeros((CAST_BLK, width), _bf16)
        else:
            o_ref[0, t, :, c:c + width] = _bf(w_ref[0, :, src:src + width])


def _permute_columns(w_in):
    depth, d, cols = w_in.shape
    nj = Z_COLS // Z_TILE
    return pl.pallas_call(
        _permute_kernel,
        grid=(depth, d // CAST_BLK),
        in_specs=[pl.BlockSpec((1, CAST_BLK, cols), lambda l, r: (l, r, 0))],
        out_specs=pl.BlockSpec((1, nj, CAST_BLK, Z_TILE), lambda l, r: (l, 0, r, 0)),
        out_shape=jax.ShapeDtypeStruct((depth, nj, d, Z_TILE), _bf16),
        compiler_params=_params(("parallel", "parallel"), fuse_inputs=[True]),
        name="permute_w_in",
    )(w_in)


def _lane_vector(values, offset):
    k = values.shape[-1]
    return jnp.pad(values.astype(_f32), ((0, 0), (offset, LANES - offset - k)))[:, None, :]


def kernel(x_prompt, x_sample, cache_na_k, cache_na_v, cache_gqa_k, cache_gqa_v, state_gla, state_gdn, c, c_ctx, norm_g, w_mod, b_mod, ffn_gu, ffn_down, w_in, w_out, na_qk_norm, na_rpb, gla_gate_up, gla_gate_bias, gla_out_norm, gqa_qk_norm, gdn_conv, gdn_a_log, gdn_dt_bias, gdn_out_norm):
    assert GRID_ROWS % NA_QROWS == 0 and GRID_ROWS >= NA_KROWS and PAST_LEN == SEQ
    d = D_MODEL
    m_ctx = BATCH * SEQ
    assert m_ctx % DEC_SEQ == 0 and DEC_SEQ % TOKEN_TILE == 0
    x = jnp.concatenate([x_prompt.reshape(m_ctx, d), x_sample.reshape(DEC_BATCH * DEC_SEQ, d)], axis=0)

    n_cond = 1 + DEC_BATCH
    cond = jnp.concatenate([c_ctx[None, :], c], axis=0)
    cond = jnp.pad(cond, ((0, (-n_cond) % 8), (0, 0)))
    mod = adaln_all(cond, w_mod, b_mod).reshape(DEPTH, cond.shape[0], N_MOD, d)

    w_gate, w_up, down = cast_ffn_weights(ffn_gu, ffn_down)
    w_in_p = _permute_columns(w_in)
    w_out_b = w_out.astype(_bf16)

    consts = _chunk_constants()
    cos, sin = _rope_tables()
    bias = na_bias_tables(na_rpb)

    qw = GLA_HEADS * GLA_DK
    wg = jnp.zeros((DEPTH, 2, LANES, qw), _f32)
    wg = wg.at[:, 0, 0:GLA_GATE_RANK].set(gla_gate_up[:, 0])
    wg = wg.at[:, 1, SMALL_GB:SMALL_GB + GLA_GATE_RANK].set(gla_gate_up[:, 1]).astype(_bf16)
    gbias = gla_gate_bias.reshape(DEPTH, 2, 1, qw)
    conv_w = gdn_conv.reshape(DEPTH, CONV_K, 3, GDN_HEADS * HEAD_DIM).transpose(0, 2, 1, 3)
    conv_w = jnp.pad(conv_w, ((0, 0), (0, 0), (0, 8 - CONV_K), (0, 0)))
    alog_lane = _lane_vector(gdn_a_log.reshape(DEPTH, 2 * GDN_HEADS), SMALL_A)
    dtb_lane = _lane_vector(gdn_dt_bias.reshape(DEPTH, 2 * GDN_HEADS), SMALL_A)
    gla_zero = jnp.zeros((BATCH, 2, GLA_HEADS, GLA_DK, GLA_DV), _f32)
    gdn_zero = jnp.zeros((BATCH, 2, GDN_HEADS, GDN_DK, GDN_DV), _f32)
    lat_blk0 = m_ctx // DEC_SEQ

    gla_l, gdn_l = [], []
    caches = None
    for l in range(DEPTH):
        x = ffn(x, mod[l], norm_g[l, 0], w_gate, w_up, down, 2 * l, 0)
        z = in_proj(x, mod[l], norm_g[l, 1], w_in_p, l)

        o_na, o_gqa, caches = ctx_attention(z, na_qk_norm[l], gqa_qk_norm[l], l, caches)
        o_na = na_latent(z, cache_na_k, cache_na_v, bias, na_qk_norm[l], l, o_na)
        o_gqa = gqa_latent(z, cache_gqa_k, cache_gqa_v, cos, sin, gqa_qk_norm[l], l, o_gqa)
        o_gla, sg = gla(z, 0, BATCH, SEQ, wg[l], gbias[l], consts, gla_zero, gla_out_norm[l])
        o_gla, _ = gla(z, lat_blk0, DEC_BATCH, DEC_SEQ, wg[l], gbias[l], consts, state_gla[:, l],
                       gla_out_norm[l], o_gla)
        o_gdn, sd = gdn(z, 0, BATCH, SEQ, conv_w[l], alog_lane[l], dtb_lane[l], consts, gdn_zero,
                        gdn_out_norm[l])
        o_gdn, _ = gdn(z, lat_blk0, DEC_BATCH, DEC_SEQ, conv_w[l], alog_lane[l], dtb_lane[l], consts,
                       state_gdn[:, l], gdn_out_norm[l], o_gdn)

        x = out_proj(x, mod[l], o_na, o_gla, o_gqa, o_gdn, w_out_b, l)
        x = ffn(x, mod[l], norm_g[l, 2], w_gate, w_up, down, 2 * l + 1, 6)

        gla_l.append(sg)
        gdn_l.append(sd)

    y_prompt = x[:m_ctx].reshape(BATCH, SEQ, d)
    y_sample = x[m_ctx:].reshape(DEC_BATCH, DEC_SEQ, d)
    return (y_prompt, y_sample) + caches + (jnp.stack(gla_l, axis=1), jnp.stack(gdn_l, axis=1))
```
